```python
import math
import jax, jax.numpy as jnp
from jax import lax
import numpy as np

D_MODEL = 1024
BATCH = 1
SEQ = 16384
DEPTH = 4

F32 = jnp.float32
N_MIXERS = 3
ROPE_THETA = 10000.0
LN_EPS = 1e-5
RMS_EPS = 1e-6
Q_BLOCK = 128
NEG = -1e30
ALPHA_DN = (2 * DEPTH) ** 0.25
BETA_DN = (8 * DEPTH) ** -0.25

DA_HEADS = 8
DA_HEAD_DIM = D_MODEL // (2 * DA_HEADS)
MLA_HEADS = 16
MLA_NOPE = 64
MLA_ROPE = 32
MLA_V = 64
MLA_Q_RANK = 256
MLA_KV_RANK = 128
NSA_HEADS = 16
NSA_GROUPS = 4
NSA_HEAD_DIM = 64
NSA_CMP_LEN = 32
NSA_CMP_STRIDE = 16
NSA_CMP_HIDDEN = 256
NSA_SLC_LEN = 64
NSA_SLC_TOPK = 16
NSA_WINDOW = 512
NSA_FORCE = 1e9
N_EXPERTS = 32
TOP_K = 4
D_EXPERT = 1024
SWIGLU_LIMIT = 7.0
SWIGLU_ALPHA = 1.702
MOE_BLOCK = 128

kernel_name = 'hybrid_diff_mla_nsa_moe_deepnorm'


def layer_norm(x, g, b):
    xf = x.astype(F32)
    mu = jnp.mean(xf, -1, keepdims=True)
    var = jnp.mean(jnp.square(xf - mu), -1, keepdims=True)
    return ((xf - mu) * lax.rsqrt(var + LN_EPS) * g.astype(F32) + b.astype(F32)).astype(x.dtype)


def rms_norm(x, g, eps):
    xf = x.astype(F32)
    return (xf * lax.rsqrt(jnp.mean(xf * xf, -1, keepdims=True) + eps) * g.astype(F32)).astype(x.dtype)


def rope_angles(pos, dim):
    inv = ROPE_THETA ** (-jnp.arange(0, dim, 2, dtype=F32) / dim)
    ang = pos.astype(F32)[:, None] * inv[None, :]
    return jnp.cos(ang), jnp.sin(ang)


def apply_rope(t, cos, sin):
    c = cos[None, :, None, :].astype(t.dtype)
    s = sin[None, :, None, :].astype(t.dtype)
    t1, t2 = jnp.split(t, 2, axis=-1)
    return jnp.concatenate([t1 * c - t2 * s, t2 * c + t1 * s], axis=-1)


def to_blocks(t):
    B, S = t.shape[:2]
    return jnp.moveaxis(t.reshape(B, S // Q_BLOCK, Q_BLOCK, *t.shape[2:]), 1, 0)


def from_blocks(t):
    nb, B, qb = t.shape[:3]
    return jnp.moveaxis(t, 0, 1).reshape(B, nb * qb, *t.shape[3:])


def causal_mask(bi, n_keys):
    qpos = bi * Q_BLOCK + jnp.arange(Q_BLOCK)
    return jnp.arange(n_keys)[None, :] <= qpos[:, None]


def diff_attention(x, w_in, lq1, lk1, lq2, lk2, subln_g, w_out, cos, sin, layer_idx):
    B, S, _ = x.shape
    H, d = DA_HEADS, DA_HEAD_DIM
    q, k, v = jnp.split(x @ w_in, 3, axis=-1)
    q = apply_rope(q.reshape(B, S, 2 * H, d), cos, sin)
    k = apply_rope(k.reshape(B, S, 2 * H, d), cos, sin)
    v = v.reshape(B, S, H, 2 * d)
    lam_init = 0.8 - 0.6 * math.exp(-0.3 * layer_idx)
    lam = (jnp.exp(jnp.sum(lq1.astype(F32) * lk1.astype(F32)))
           - jnp.exp(jnp.sum(lq2.astype(F32) * lk2.astype(F32))) + lam_init)
    scale = d ** -0.5

    def block(args):
        qi, bi = args
        s = jnp.einsum('bqhd,bkhd->bhqk', qi, k, preferred_element_type=F32) * scale
        s = jnp.where(causal_mask(bi, S), s, NEG)
        p = jax.nn.softmax(s, axis=-1).reshape(B, H, 2, Q_BLOCK, S)
        p = p[:, :, 0] - lam * p[:, :, 1]
        return jnp.einsum('bhqk,bkhe->bqhe', p.astype(v.dtype), v)

    o = from_blocks(lax.map(block, (to_blocks(q), jnp.arange(S // Q_BLOCK))))
    o = rms_norm(o, subln_g, LN_EPS) * (1.0 - lam_init)
    return o.reshape(B, S, H * 2 * d) @ w_out


def mla(x, w_in, q_norm_g, kv_norm_g, w_uq, w_ukv, w_out, cos, sin):
    B, S, _ = x.shape
    H = MLA_HEADS
    c_q, c_kv, k_rope = jnp.split(x @ w_in, [MLA_Q_RANK, MLA_Q_RANK + MLA_KV_RANK], axis=-1)
    q = (rms_norm(c_q, q_norm_g, RMS_EPS) @ w_uq).reshape(B, S, H, MLA_NOPE + MLA_ROPE)
    q_nope, q_rope = jnp.split(q, [MLA_NOPE], axis=-1)
    q_rope = apply_rope(q_rope, cos, sin)
    k_rope = apply_rope(k_rope[:, :, None, :], cos, sin)[:, :, 0]
    kv = (rms_norm(c_kv, kv_norm_g, RMS_EPS) @ w_ukv).reshape(B, S, H, MLA_NOPE + MLA_V)
    k_nope, v = jnp.split(kv, [MLA_NOPE], axis=-1)
    scale = (MLA_NOPE + MLA_ROPE) ** -0.5

    def block(args):
        qn, qr, bi = args
        s = (jnp.einsum('bqhd,bkhd->bhqk', qn, k_nope, preferred_element_type=F32)
             + jnp.einsum('bqhr,bkr->bhqk', qr, k_rope, preferred_element_type=F32)) * scale
        s = jnp.where(causal_mask(bi, S), s, NEG)
        p = jax.nn.softmax(s, axis=-1)
        return jnp.einsum('bhqk,bkhd->bqhd', p.astype(v.dtype), v)

    o = from_blocks(lax.map(block, (to_blocks(q_nope), to_blocks(q_rope), jnp.arange(S // Q_BLOCK))))
    return o.reshape(B, S, H * MLA_V) @ w_out


def nsa(x, w_in, pos_k, pos_v, ck_w1, ck_w2, cv_w1, cv_w2, w_out, cos, sin):
    B, S, _ = x.shape
    H, G, d = NSA_HEADS, NSA_GROUPS, NSA_HEAD_DIM
    R = H // G
    L, st, Ls, W = NSA_CMP_LEN, NSA_CMP_STRIDE, NSA_SLC_LEN, NSA_WINDOW
    n_cmp = (S - L) // st + 1
    n_slc = S // Ls
    n_top = min(NSA_SLC_TOPK, n_slc)
    ratio = Ls // st
    off = L // st - 1
    sizes = [H * d] + [G * d] * 6 + [3 * H]
    q, kc, vc, ks, vs, kw, vw, gl = jnp.split(x @ w_in, np.cumsum(sizes)[:-1].tolist(), axis=-1)
    q = apply_rope(q.reshape(B, S, H, d), cos, sin).reshape(B, S, G, R, d)
    ks = apply_rope(ks.reshape(B, S, G, d), cos, sin)
    kw = apply_rope(kw.reshape(B, S, G, d), cos, sin)
    vs = vs.reshape(B, S, G, d)
    vw = vw.reshape(B, S, G, d)
    gate = jax.nn.sigmoid(gl.astype(F32)).reshape(B, S, G, R, 3)

    blk_idx = jnp.arange(n_cmp)[:, None] * st + jnp.arange(L)[None, :]

    def compress(t, pos, w1, w2):
        tb = t.reshape(B, S, G, d)[:, blk_idx] + pos[:, None, :]
        tb = tb.transpose(0, 1, 3, 2, 4).reshape(B, n_cmp, G, L * d)
        return jax.nn.gelu(tb @ w1) @ w2

    cmp_end = jnp.arange(n_cmp) * st + L - 1
    cos_c, sin_c = rope_angles(cmp_end, d)
    k_cmp = apply_rope(compress(kc, pos_k, ck_w1, ck_w2), cos_c, sin_c)
    v_cmp = compress(vc, pos_v, cv_w1, cv_w2)

    ksT = ks.transpose(0, 2, 1, 3)
    vsT = vs.transpose(0, 2, 1, 3)
    kw_pad = jnp.pad(kw, ((0, 0), (W, 0), (0, 0), (0, 0)))
    vw_pad = jnp.pad(vw, ((0, 0), (W, 0), (0, 0), (0, 0)))
    b_ix = jnp.arange(B)[:, None, None, None]
    g_ix = jnp.arange(G)[None, :, None, None]
    scale = d ** -0.5
    back = ratio * n_slc - n_cmp
    jb = jnp.arange(n_slc)

    def block(args):
        qi, gi, bi = args
        qpos = bi * Q_BLOCK + jnp.arange(Q_BLOCK)
        s = jnp.einsum('bqgrd,bngd->bgrqn', qi, k_cmp, preferred_element_type=F32) * scale
        m_c = cmp_end[None, :] <= qpos[:, None]
        p_c = jax.nn.softmax(jnp.where(m_c, s, NEG), axis=-1) * jnp.any(m_c, -1).astype(F32)[:, None]
        o_c = jnp.einsum('bgrqn,bngd->bqgrd', p_c.astype(v_cmp.dtype), v_cmp)
        p_g = jnp.pad(p_c.sum(axis=2), ((0, 0), (0, 0), (0, 0), (off, back)))
        imp = jnp.zeros(p_g.shape[:3] + (n_slc,), F32)
        for mm in range(ratio):
            for nn in range(L // st):
                start = off + mm - nn
                imp = imp + lax.slice_in_dim(p_g, start, start + ratio * (n_slc - 1) + 1, stride=ratio, axis=-1)
        cur = qpos // Ls
        valid = jb[None, :] <= cur[:, None]
        forced = valid & ((jb[None, :] == 0) | (jb[None, :] >= cur[:, None] - 1))
        score = jnp.where(forced, NSA_FORCE, jnp.where(valid, imp, -1.0))
        _, sel = lax.top_k(score, n_top)
        tok = (sel[..., None] * Ls + jnp.arange(Ls)).reshape(B, G, Q_BLOCK, n_top * Ls)
        k_sel = ksT[b_ix, g_ix, tok]
        v_sel = vsT[b_ix, g_ix, tok]
        s = jnp.einsum('bqgrd,bgqtd->bgrqt', qi, k_sel, preferred_element_type=F32) * scale
        m_s = (tok <= qpos[None, None, :, None])[:, :, None]
        p_s = jax.nn.softmax(jnp.where(m_s, s, NEG), axis=-1)
        o_s = jnp.einsum('bgrqt,bgqtd->bqgrd', p_s.astype(v_sel.dtype), v_sel)
        k_w = lax.dynamic_slice_in_dim(kw_pad, bi * Q_BLOCK, W + Q_BLOCK, axis=1)
        v_w = lax.dynamic_slice_in_dim(vw_pad, bi * Q_BLOCK, W + Q_BLOCK, axis=1)
        kpos = bi * Q_BLOCK - W + jnp.arange(W + Q_BLOCK)
        m_w = ((kpos[None, :] <= qpos[:, None]) & (kpos[None, :] > qpos[:, None] - W)
               & (kpos[None, :] >= 0))
        s = jnp.einsum('bqgrd,bkgd->bgrqk', qi, k_w, preferred_element_type=F32) * scale
        p_w = jax.nn.softmax(jnp.where(m_w, s, NEG), axis=-1)
        o_w = jnp.einsum('bgrqk,bkgd->bqgrd', p_w.astype(v_w.dtype), v_w)
        o = gi[..., 0:1] * o_c + gi[..., 1:2] * o_s + gi[..., 2:3] * o_w
        return o.astype(qi.dtype)

    o = from_blocks(lax.map(block, (to_blocks(q), to_blocks(gate), jnp.arange(S // Q_BLOCK))))
    return o.reshape(B, S, H * d) @ w_out


def moe(x, w_router, b_router, w_gu, b_gu, w_down, b_down):
    B, S, Dm = x.shape
    N = B * S
    xt = x.reshape(N, Dm)
    logits = (xt @ w_router + b_router).astype(F32)
    top_logit, top_e = lax.top_k(logits, TOP_K)
    gate = jax.nn.softmax(top_logit, axis=-1)
    A = N * TOP_K
    e_flat = top_e.reshape(A)
    tok_flat = jnp.arange(A, dtype=jnp.int32) // TOP_K
    g_flat = gate.reshape(A)
    order = jnp.argsort(e_flat)
    e_sorted = e_flat[order]
    counts = jnp.zeros((N_EXPERTS,), jnp.int32).at[e_flat].add(1)
    starts = jnp.cumsum(counts) - counts
    padded = (counts + MOE_BLOCK - 1) // MOE_BLOCK * MOE_BLOCK
    pends = jnp.cumsum(padded)
    pstarts = pends - padded
    dest = pstarts[e_sorted] + (jnp.arange(A, dtype=jnp.int32) - starts[e_sorted])
    n_rows = (A + MOE_BLOCK - 1) // MOE_BLOCK * MOE_BLOCK + N_EXPERTS * MOE_BLOCK
    row_tok = jnp.full((n_rows,), N, jnp.int32).at[dest].set(tok_flat[order])
    row_gate = jnp.zeros((n_rows,), F32).at[dest].set(g_flat[order])
    nblk = n_rows // MOE_BLOCK
    blk_e = jnp.minimum(jnp.searchsorted(pends, jnp.arange(nblk, dtype=jnp.int32) * MOE_BLOCK, side='right'),
                        N_EXPERTS - 1)
    x_pad = jnp.concatenate([xt, jnp.zeros((1, Dm), xt.dtype)], axis=0)

    def expert_block(args):
        tok, e, gw = args
        hgu = x_pad[tok] @ w_gu[e] + b_gu[e]
        hg = jnp.minimum(hgu[:, 0::2], SWIGLU_LIMIT)
        hl = jnp.clip(hgu[:, 1::2], -SWIGLU_LIMIT, SWIGLU_LIMIT)
        act = hg * jax.nn.sigmoid(SWIGLU_ALPHA * hg) * (hl + 1.0)
        y = act @ w_down[e] + b_down[e]
        return y * gw[:, None].astype(y.dtype)

    y = lax.map(expert_block, (row_tok.reshape(nblk, MOE_BLOCK), blk_e, row_gate.reshape(nblk, MOE_BLOCK)))
    out = jnp.zeros((N + 1, Dm), y.dtype).at[row_tok].add(y.reshape(n_rows, Dm))
    return out[:N].reshape(B, S, Dm).astype(x.dtype)


def setup_inputs(seed: int = 0) -> dict:
    key = jax.random.key(seed)
    k = jax.random.split(key, 32)

    def nrm(i, shape, scale):
        return scale * jax.random.normal(k[i], shape, F32)

    nA = len(range(0, DEPTH, N_MIXERS))
    nB = len(range(1, DEPTH, N_MIXERS))
    nC = len(range(2, DEPTH, N_MIXERS))
    D = D_MODEL
    da_v = DA_HEADS * 2 * DA_HEAD_DIM
    mla_in = MLA_Q_RANK + MLA_KV_RANK + MLA_ROPE
    nsa_in = NSA_HEADS * NSA_HEAD_DIM + 6 * NSA_GROUPS * NSA_HEAD_DIM + 3 * NSA_HEADS
    cmp_in = NSA_CMP_LEN * NSA_HEAD_DIM
    return {
        'x': nrm(0, (BATCH, SEQ, D), 1.0),
        'da_w_in': nrm(1, (nA, D, 3 * da_v), D ** -0.5),
        'da_lambda_q1': nrm(2, (nA, DA_HEAD_DIM), 0.1),
        'da_lambda_k1': nrm(3, (nA, DA_HEAD_DIM), 0.1),
        'da_lambda_q2': nrm(4, (nA, DA_HEAD_DIM), 0.1),
        'da_lambda_k2': nrm(5, (nA, DA_HEAD_DIM), 0.1),
        'da_subln': 1.0 + nrm(6, (nA, 2 * DA_HEAD_DIM), 0.02),
        'da_w_out': nrm(7, (nA, da_v, D), BETA_DN * da_v ** -0.5),
        'mla_w_in': nrm(8, (nB, D, mla_in), D ** -0.5),
        'mla_q_norm': 1.0 + nrm(9, (nB, MLA_Q_RANK), 0.02),
        'mla_kv_norm': 1.0 + nrm(10, (nB, MLA_KV_RANK), 0.02),
        'mla_w_uq': nrm(11, (nB, MLA_Q_RANK, MLA_HEADS * (MLA_NOPE + MLA_ROPE)), MLA_Q_RANK ** -0.5),
        'mla_w_ukv': nrm(12, (nB, MLA_KV_RANK, MLA_HEADS * (MLA_NOPE + MLA_V)), MLA_KV_RANK ** -0.5),
        'mla_w_out': nrm(13, (nB, MLA_HEADS * MLA_V, D), BETA_DN * (MLA_HEADS * MLA_V) ** -0.5),
        'nsa_w_in': nrm(14, (nC, D, nsa_in), D ** -0.5),
        'nsa_cmp_pos_k': nrm(15, (nC, NSA_CMP_LEN, NSA_HEAD_DIM), 0.02),
        'nsa_cmp_pos_v': nrm(16, (nC, NSA_CMP_LEN, NSA_HEAD_DIM), 0.02),
        'nsa_cmp_k_w1': nrm(17, (nC, cmp_in, NSA_CMP_HIDDEN), cmp_in ** -0.5),
        'nsa_cmp_k_w2': nrm(18, (nC, NSA_CMP_HIDDEN, NSA_HEAD_DIM), NSA_CMP_HIDDEN ** -0.5),
        'nsa_cmp_v_w1': nrm(19, (nC, cmp_in, NSA_CMP_HIDDEN), cmp_in ** -0.5),
        'nsa_cmp_v_w2': nrm(20, (nC, NSA_CMP_HIDDEN, NSA_HEAD_DIM), NSA_CMP_HIDDEN ** -0.5),
        'nsa_w_out': nrm(21, (nC, NSA_HEADS * NSA_HEAD_DIM, D), BETA_DN * (NSA_HEADS * NSA_HEAD_DIM) ** -0.5),
        'ln1_g': 1.0 + nrm(22, (DEPTH, D), 0.02),
        'ln1_b': nrm(23, (DEPTH, D), 0.02),
        'ln2_g': 1.0 + nrm(24, (DEPTH, D), 0.02),
        'ln2_b': nrm(25, (DEPTH, D), 0.02),
        'moe_w_router': nrm(26, (DEPTH, D, N_EXPERTS), D ** -0.5),
        'moe_b_router': nrm(27, (DEPTH, N_EXPERTS), 0.01),
        'moe_w_gu': nrm(28, (DEPTH, N_EXPERTS, D, 2 * D_EXPERT), D ** -0.5),
        'moe_b_gu': nrm(29, (DEPTH, N_EXPERTS, 2 * D_EXPERT), 0.02),
        'moe_w_down': nrm(30, (DEPTH, N_EXPERTS, D_EXPERT, D), BETA_DN * D_EXPERT ** -0.5),
        'moe_b_down': nrm(31, (DEPTH, N_EXPERTS, D), 0.02),
    }


def reference(x, da_w_in, da_lambda_q1, da_lambda_k1, da_lambda_q2, da_lambda_k2, da_subln, da_w_out,
              mla_w_in, mla_q_norm, mla_kv_norm, mla_w_uq, mla_w_ukv, mla_w_out,
              nsa_w_in, nsa_cmp_pos_k, nsa_cmp_pos_v, nsa_cmp_k_w1, nsa_cmp_k_w2, nsa_cmp_v_w1,
              nsa_cmp_v_w2, nsa_w_out,
              ln1_g, ln1_b, ln2_g, ln2_b,
              moe_w_router, moe_b_router, moe_w_gu, moe_b_gu, moe_w_down, moe_b_down):
    S = x.shape[1]
    pos = jnp.arange(S)
    cos64, sin64 = rope_angles(pos, DA_HEAD_DIM)
    cos32, sin32 = rope_angles(pos, MLA_ROPE)
    h = x
    for i in range(DEPTH):
        m, j = i % N_MIXERS, i // N_MIXERS
        if m == 0:
            y = diff_attention(h, da_w_in[j], da_lambda_q1[j], da_lambda_k1[j], da_lambda_q2[j],
                               da_lambda_k2[j], da_subln[j], da_w_out[j], cos64, sin64, i)
        elif m == 1:
            y = mla(h, mla_w_in[j], mla_q_norm[j], mla_kv_norm[j], mla_w_uq[j], mla_w_ukv[j],
                    mla_w_out[j], cos32, sin32)
        else:
            y = nsa(h, nsa_w_in[j], nsa_cmp_pos_k[j], nsa_cmp_pos_v[j], nsa_cmp_k_w1[j],
                    nsa_cmp_k_w2[j], nsa_cmp_v_w1[j], nsa_cmp_v_w2[j], nsa_w_out[j], cos64, sin64)
        h = layer_norm(ALPHA_DN * h + y, ln1_g[i], ln1_b[i])
        y = moe(h, moe_w_router[i], moe_b_router[i], moe_w_gu[i], moe_b_gu[i], moe_w_down[i], moe_b_down[i])
        h = layer_norm(ALPHA_DN * h + y, ln2_g[i], ln2_b[i])
    return h
```

```python
import collections
import functools
import math

import numpy as np
import jax
import jax.numpy as jnp
from jax import lax
from jax.experimental import pallas as pl
from jax.experimental.pallas import tpu as pltpu

F32 = jnp.float32
BF16 = jnp.bfloat16
I32 = jnp.int32

D_MODEL = 1024
DEPTH = 4
N_MIXERS = 3
ROPE_THETA = 10000.0
LN_EPS = 1e-5
RMS_EPS = 1e-6
NEG = -1e30
ALPHA_DN = (2 * DEPTH) ** 0.25

DA_HEADS = 8
DA_HEAD_DIM = 64
MLA_HEADS = 16
MLA_NOPE = 64
MLA_ROPE = 32
MLA_V = 64
MLA_Q_RANK = 256
MLA_KV_RANK = 128
NSA_HEADS = 16
NSA_GROUPS = 4
NSA_HEAD_DIM = 64
NSA_CMP_LEN = 32
NSA_CMP_STRIDE = 16
NSA_CMP_HIDDEN = 256
NSA_SLC_LEN = 64
NSA_SLC_TOPK = 16
NSA_WINDOW = 512
NSA_FORCE = 1e9
N_EXPERTS = 32
TOP_K = 4
D_EXPERT = 1024
SWIGLU_LIMIT = 7.0
SWIGLU_ALPHA = 1.702

LANES = 128
VMEM_LIMIT = 48 * 1024 * 1024

PROJ_TM = 256
ATT_TQ = 256
ATT_TK = 512
NSA_TQ = 128
NSA_TK = 512
MOE_TM = 512
LN_TM = 512
GATHER_CHUNK = 512


def _cparams(sem):
    return pltpu.CompilerParams(dimension_semantics=sem, vmem_limit_bytes=VMEM_LIMIT)


def _rope_angles(pos, dim):
    inv = ROPE_THETA ** (-jnp.arange(0, dim, 2, dtype=F32) / dim)
    ang = pos.astype(F32)[:, None] * inv[None, :]
    return jnp.cos(ang), jnp.sin(ang)


def _rope128(t, c, sg, half):
    lane = lax.broadcasted_iota(I32, t.shape, 1)
    first = (lane % (2 * half)) < half
    partner = jnp.where(first, pltpu.roll(t, LANES - half, 1), pltpu.roll(t, half, 1))
    return t * c + partner * sg


Seg = collections.namedtuple("Seg", "start width mode scale dtype")


def _proj_kernel(*refs, segs, half):
    x_ref, w_ref, c_ref, s_ref = refs[:4]
    out_refs = refs[4:]
    xb = x_ref[...].astype(BF16)
    for seg, o_ref in zip(segs, out_refs):
        acc = jnp.dot(xb, w_ref[:, seg.start:seg.start + seg.width], preferred_element_type=F32)
        if seg.mode == "rope":
            c = c_ref[...]
            sg = s_ref[...]
            for g in range(seg.width // LANES):
                t = _rope128(acc[:, g * LANES:(g + 1) * LANES], c, sg, half)
                o_ref[:, g * LANES:(g + 1) * LANES] = (t * seg.scale).astype(o_ref.dtype)
        elif seg.mode == "sigmoid":
            o_ref[...] = jax.nn.sigmoid(acc).astype(o_ref.dtype)
        else:
            o_ref[...] = (acc * seg.scale).astype(o_ref.dtype) if seg.scale != 1.0 else acc.astype(o_ref.dtype)


def _project(x, w, tab_c, tab_s, segs, half, name):
    S, K = x.shape
    tm = min(PROJ_TM, S)
    return pl.pallas_call(
        functools.partial(_proj_kernel, segs=segs, half=half),
        grid=(S // tm,),
        in_specs=[
            pl.BlockSpec((tm, K), lambda i: (i, 0)),
            pl.BlockSpec(w.shape, lambda i: (0, 0)),
            pl.BlockSpec((tm, LANES), lambda i: (i, 0)),
            pl.BlockSpec((tm, LANES), lambda i: (i, 0)),
        ],
        out_specs=[pl.BlockSpec((tm, s.width), lambda i: (i, 0)) for s in segs],
        out_shape=[jax.ShapeDtypeStruct((S, s.width), s.dtype) for s in segs],
        compiler_params=_cparams(("parallel",)),
        name=name,
    )(x, w, tab_c, tab_s)


def _softmax_step(s, v, m_ref, l_ref, a_ref):
    m_old = m_ref[...]
    m_new = jnp.maximum(m_old, jnp.max(s, axis=1, keepdims=True))
    alpha = jnp.exp(m_old - m_new)
    p = jnp.exp(s - m_new)
    l_ref[...] = alpha * l_ref[...] + jnp.sum(p, axis=1, keepdims=True)
    a_ref[...] = alpha * a_ref[...] + jnp.dot(p.astype(BF16), v, preferred_element_type=F32)
    m_ref[...] = m_new


def _qk(q, k):
    return lax.dot_general(q, k, (((1,), (1,)), ((), ())), preferred_element_type=F32)


def _flash_kernel(*refs, variant, tq, tk, nk, lam_init):
    if variant == "da":
        lam_ref, g_ref, q_ref, k_ref, v_ref, o_ref, qa, qb, m1, l1, a1, m2, l2, a2 = refs
    else:
        q_ref, k_ref, v_ref, o_ref, m1, l1, a1, m2, l2, a2 = refs
    i = pl.program_id(1)
    j = pl.program_id(2)

    @pl.when(j == 0)
    def _init():
        if variant == "da":
            q = q_ref[...]
            lane = lax.broadcasted_iota(I32, q.shape, 1)
            qa[...] = jnp.where(lane < DA_HEAD_DIM, q, jnp.zeros_like(q))
            qb[...] = jnp.where(lane >= DA_HEAD_DIM, q, jnp.zeros_like(q))
        for m, l, a in ((m1, l1, a1), (m2, l2, a2)):
            m[...] = jnp.full(m.shape, NEG, F32)
            l[...] = jnp.zeros(l.shape, F32)
            a[...] = jnp.zeros(a.shape, F32)

    def step(masked):
        k = k_ref[...]
        v = v_ref[...]
        if variant == "da":
            ops = ((qa[...], k, m1, l1, a1), (qb[...], k, m2, l2, a2))
        else:
            ops = ((q_ref[:, :LANES], k[:, :LANES], m1, l1, a1), (q_ref[:, LANES:], k[:, LANES:], m2, l2, a2))
        for qq, kk, m, l, a in ops:
            s = _qk(qq, kk)
            if masked:
                qpos = i * tq + lax.broadcasted_iota(I32, s.shape, 0)
                kpos = j * tk + lax.broadcasted_iota(I32, s.shape, 1)
                s = jnp.where(kpos <= qpos, s, NEG)
            _softmax_step(s, v, m, l, a)

    last_j = (i * tq + tq - 1) // tk
    crosses = (j * tk + tk - 1) > (i * tq)

    @pl.when((j <= last_j) & crosses)
    def _masked():
        step(True)

    @pl.when((j <= last_j) & jnp.logical_not(crosses))
    def _plain():
        step(False)

    @pl.when(j == nk - 1)
    def _fin():
        o1 = a1[...] / l1[...]
        o2 = a2[...] / l2[...]
        if variant == "da":
            lam4 = lam_ref[...]
            lam = (jnp.exp(jnp.sum(lam4[0:1] * lam4[1:2], axis=1, keepdims=True))
                   - jnp.exp(jnp.sum(lam4[2:3] * lam4[3:4], axis=1, keepdims=True)) + lam_init)
            o = o1 - lam * o2
            ms = jnp.mean(o * o, axis=-1, keepdims=True)
            o = o * lax.rsqrt(ms + LN_EPS) * g_ref[...] * (1.0 - lam_init)
        else:
            lane = lax.broadcasted_iota(I32, o1.shape, 1)
            o = jnp.where(lane < MLA_V, o1, o2)
        o_ref[...] = o.astype(o_ref.dtype)


def _flash(variant, q, k, v, extra, lam_init, name):
    S = q.shape[0]
    tq, tk = min(ATT_TQ, S), min(ATT_TK, S)
    nq, nk = S // tq, S // tk
    qw = LANES if variant == "da" else 2 * LANES
    n_groups = q.shape[1] // qw

    def kv_map(h, i, j):
        return (jnp.minimum(j, (i * tq + tq - 1) // tk), h)

    in_specs = [
        pl.BlockSpec((tq, qw), lambda h, i, j: (i, h)),
        pl.BlockSpec((tk, qw), kv_map),
        pl.BlockSpec((tk, LANES), kv_map),
    ]
    scratch = [pltpu.VMEM((tq, 1), F32), pltpu.VMEM((tq, 1), F32), pltpu.VMEM((tq, LANES), F32)] * 2
    if variant == "da":
        in_specs = [pl.BlockSpec(extra[0].shape, lambda h, i, j: (0, 0)),
                    pl.BlockSpec(extra[1].shape, lambda h, i, j: (0, 0))] + in_specs
        scratch = [pltpu.VMEM((tq, LANES), BF16)] * 2 + scratch
    return pl.pallas_call(
        functools.partial(_flash_kernel, variant=variant, tq=tq, tk=tk, nk=nk, lam_init=lam_init),
        grid=(n_groups, nq, nk),
        in_specs=in_specs,
        out_specs=pl.BlockSpec((tq, LANES), lambda h, i, j: (i, h)),
        out_shape=jax.ShapeDtypeStruct((S, n_groups * LANES), BF16),
        scratch_shapes=scratch,
        compiler_params=_cparams(("parallel", "parallel", "arbitrary")),
        name=name,
    )(*extra, q, k, v)


def _layer_norm(x, g, b):
    mu = jnp.mean(x, axis=-1, keepdims=True)
    xc = x - mu
    var = jnp.mean(xc * xc, axis=-1, keepdims=True)
    return xc * lax.rsqrt(var + LN_EPS) * g + b


def _out_ln_kernel(o_ref, w_ref, h_ref, g_ref, b_ref, out_ref):
    y = jnp.dot(o_ref[...], w_ref[...], preferred_element_type=F32)
    out_ref[...] = _layer_norm(ALPHA_DN * h_ref[...] + y, g_ref[...], b_ref[...])


def _out_ln(o, w, h, g, b, name):
    S, K = o.shape
    D = h.shape[1]
    tm = min(LN_TM, S)
    return pl.pallas_call(
        _out_ln_kernel,
        grid=(S // tm,),
        in_specs=[
            pl.BlockSpec((tm, K), lambda i: (i, 0)),
            pl.BlockSpec((K, D), lambda i: (0, 0)),
            pl.BlockSpec((tm, D), lambda i: (i, 0)),
            pl.BlockSpec((1, D), lambda i: (0, 0)),
            pl.BlockSpec((1, D), lambda i: (0, 0)),
        ],
        out_specs=pl.BlockSpec((tm, D), lambda i: (i, 0)),
        out_shape=jax.ShapeDtypeStruct((S, D), F32),
        compiler_params=_cparams(("parallel",)),
        name=name,
    )(o, w, h, g.reshape(1, D), b.reshape(1, D))


def _rms(x, g, eps):
    return x * lax.rsqrt(jnp.mean(x * x, axis=-1, keepdims=True) + eps) * g


def _mla_proj_kernel(x_ref, win_ref, wuq_ref, wuk_ref, wuv_ref, gq_ref, gkv_ref, c_ref, s_ref,
                     q_ref, k_ref, v_ref, *, scale):
    xb = x_ref[...].astype(BF16)
    p1 = jnp.dot(xb, win_ref[...], preferred_element_type=F32)
    c_q = _rms(p1[:, :MLA_Q_RANK], gq_ref[...], RMS_EPS).astype(BF16)
    c_kv = _rms(p1[:, MLA_Q_RANK:MLA_Q_RANK + MLA_KV_RANK], gkv_ref[...], RMS_EPS).astype(BF16)
    c = c_ref[...]
    sg = s_ref[...]
    half = MLA_ROPE // 2
    k_rope = _rope128(p1[:, MLA_Q_RANK + MLA_KV_RANK:], c, sg, half)
    q = jnp.dot(c_q, wuq_ref[...], preferred_element_type=F32)
    k = jnp.dot(c_kv, wuk_ref[...], preferred_element_type=F32)
    for h in range(MLA_HEADS):
        sl = slice(h * LANES, (h + 1) * LANES)
        q_ref[:, sl] = (_rope128(q[:, sl], c, sg, half) * scale).astype(q_ref.dtype)
        k_ref[:, sl] = (k[:, sl] + k_rope).astype(k_ref.dtype)
    v_ref[...] = jnp.dot(c_kv, wuv_ref[...], preferred_element_type=F32).astype(v_ref.dtype)


def _mla_project(x, w_in, w_uq, w_uk, w_uv, gq, gkv, tab_c, tab_s, scale):
    S, D = x.shape
    tm = min(PROJ_TM, S)
    full = lambda a: pl.BlockSpec(a.shape, lambda i: (0,) * a.ndim)
    row = lambda w: pl.BlockSpec((tm, w), lambda i: (i, 0))
    hw = MLA_HEADS * LANES
    return pl.pallas_call(
        functools.partial(_mla_proj_kernel, scale=scale),
        grid=(S // tm,),
        in_specs=[row(D), full(w_in), full(w_uq), full(w_uk), full(w_uv), full(gq), full(gkv), row(LANES), row(LANES)],
        out_specs=[row(hw), row(hw), row(MLA_HEADS * MLA_V)],
        out_shape=[jax.ShapeDtypeStruct((S, hw), BF16), jax.ShapeDtypeStruct((S, hw), BF16),
                   jax.ShapeDtypeStruct((S, MLA_HEADS * MLA_V), BF16)],
        compiler_params=_cparams(("parallel",)),
        name="mla_project",
    )(x, w_in, w_uq, w_uk, w_uv, gq, gkv, tab_c, tab_s)


def _gelu_tanh(x):
    return 0.5 * x * (1.0 + jnp.tanh(math.sqrt(2.0 / math.pi) * (x + 0.044715 * (x * x * x))))


def _compress_kernel(x_ref, pa_ref, pb_ref, w1a_ref, w1b_ref, w2_ref, c_ref, s_ref, o_ref, *, nc, rope):
    x = x_ref[0]
    ha = jnp.dot((x + pa_ref[...]).astype(BF16), w1a_ref[...], preferred_element_type=F32)
    hb = jnp.dot((x + pb_ref[...]).astype(BF16), w1b_ref[...], preferred_element_type=F32)
    hid = ha + pltpu.roll(hb, nc - 1, 0)
    out = jnp.dot(_gelu_tanh(hid).astype(BF16), w2_ref[...], preferred_element_type=F32)
    if rope:
        out = _rope128(out, c_ref[...], s_ref[...], NSA_HEAD_DIM // 2)
    o_ref[0] = out.astype(o_ref.dtype)


def _compress(xc, pos, w1, w2, tab_c, tab_s, rope, name):
    G, nc, cw = xc.shape
    half = NSA_CMP_LEN // 2
    pa = pos[:half].reshape(1, cw)
    pb = pos[half:].reshape(1, cw)
    w1a = w1[:cw].astype(BF16)
    w1b = w1[cw:].astype(BF16)
    w2p = jnp.pad(w2, ((0, 0), (0, LANES - w2.shape[1]))).astype(BF16)
    full = lambda a: pl.BlockSpec(a.shape, lambda g: (0,) * a.ndim)
    return pl.pallas_call(
        functools.partial(_compress_kernel, nc=nc, rope=rope),
        grid=(G,),
        in_specs=[pl.BlockSpec((1, nc, cw), lambda g: (g, 0, 0)), full(pa), full(pb), full(w1a), full(w1b),
                  full(w2p), full(tab_c), full(tab_s)],
        out_specs=pl.BlockSpec((1, nc, LANES), lambda g: (g, 0, 0)),
        out_shape=jax.ShapeDtypeStruct((G, nc, LANES), BF16),
        compiler_params=_cparams(("parallel",)),
        name=name,
    )(xc, pa, pb, w1a, w1b, w2p, tab_c, tab_s)


def _split3(x):
    hi = x.astype(BF16)
    r = x - hi.astype(F32)
    mid = r.astype(BF16)
    lo = (r - mid.astype(F32)).astype(BF16)
    return hi, mid, lo


def _merge_heads(heads):
    pairs = [heads[r] + pltpu.roll(heads[r + 1], NSA_HEAD_DIM, 1) for r in (0, 2)]
    return jnp.concatenate(pairs, axis=1)


def _nsa_local_kernel(q_ref, kc_ref, vc_ref, kw0, kw1, kw2, kw3, kw4, vw0, vw1, vw2, vw3, vw4,
                      gc_ref, gw_ref, amat_ref, o_ref, sel_ref, *, tq, nc, n_slc, n_top):
    i = pl.program_id(1)
    R = NSA_HEADS // NSA_GROUPS
    q = jnp.concatenate([q_ref[:, r * LANES:(r + 1) * LANES] for r in range(R)], axis=0)
    rows = R * tq
    qpos = i * tq + lax.broadcasted_iota(I32, (rows, 1), 0) % tq

    s = _qk(q, kc_ref[0])
    cmp_end = lax.broadcasted_iota(I32, (rows, nc), 1) * NSA_CMP_STRIDE + (NSA_CMP_LEN - 1)
    m_c = cmp_end <= qpos
    s = jnp.where(m_c, s, NEG)
    e = jnp.exp(s - jnp.max(s, axis=1, keepdims=True))
    p_c = e / jnp.sum(e, axis=1, keepdims=True)
    p_c = p_c * (qpos >= NSA_CMP_LEN - 1).astype(F32)
    o_c = jnp.dot(p_c.astype(BF16), vc_ref[0], preferred_element_type=F32)

    p_g = p_c[0:tq]
    for r in range(1, R):
        p_g = p_g + p_c[r * tq:(r + 1) * tq]
    amat = amat_ref[...]
    imp = None
    for part in _split3(p_g):
        t = jnp.dot(part, amat, preferred_element_type=F32)
        imp = t if imp is None else imp + t
    nslp = imp.shape[1]
    jb = lax.broadcasted_iota(I32, (tq, nslp), 1)
    qp = i * tq + lax.broadcasted_iota(I32, (tq, 1), 0)
    cur = qp // NSA_SLC_LEN
    valid = jb <= cur
    forced = valid & ((jb == 0) | (jb >= cur - 1))
    score = jnp.where(forced, NSA_FORCE, jnp.where(valid, imp, -1.0))
    score = jnp.where(jb < n_slc, score, -2.0)
    sel = jnp.zeros((tq, nslp), F32)
    for _ in range(n_top):
        mx = jnp.max(score, axis=1, keepdims=True)
        idx = jnp.min(jnp.where(score == mx, jb, nslp), axis=1, keepdims=True)
        hit = jb == idx
        sel = jnp.where(hit, 1.0, sel)
        score = jnp.where(hit, -jnp.inf, score)
    sel_ref[0] = sel.astype(sel_ref.dtype)

    nwin = NSA_WINDOW // tq + 1
    k_w = jnp.concatenate([r[...] for r in (kw0, kw1, kw2, kw3, kw4)], axis=0)
    v_w = jnp.concatenate([r[...] for r in (vw0, vw1, vw2, vw3, vw4)], axis=0)
    s = _qk(q, k_w)
    kpos = (i - (nwin - 1)) * tq + lax.broadcasted_iota(I32, (rows, nwin * tq), 1)
    m_w = (kpos <= qpos) & (kpos > qpos - NSA_WINDOW) & (kpos >= 0)
    s = jnp.where(m_w, s, NEG)
    e = jnp.exp(s - jnp.max(s, axis=1, keepdims=True))
    p_w = e / jnp.sum(e, axis=1, keepdims=True)
    o_w = jnp.dot(p_w.astype(BF16), v_w, preferred_element_type=F32)

    oc = _merge_heads([o_c[r * tq:(r + 1) * tq] for r in range(R)])
    ow = _merge_heads([o_w[r * tq:(r + 1) * tq] for r in range(R)])
    o_ref[...] = gc_ref[...] * oc + gw_ref[...] * ow


def _nsa_local(q, k_cmp, v_cmp, kw, vw, gc, gw, amat, n_slc, n_top):
    S = q.shape[0]
    tq = NSA_TQ
    G = NSA_GROUPS
    nc = k_cmp.shape[1]
    nslp = amat.shape[1]
    gw_lanes = (NSA_HEADS // G) * NSA_HEAD_DIM
    nwin = NSA_WINDOW // tq + 1
    assert nwin == 5

    def win_spec(t):
        return pl.BlockSpec((tq, LANES), lambda g, i: (jnp.maximum(i - (nwin - 1) + t, 0), g))

    in_specs = ([pl.BlockSpec((tq, (NSA_HEADS // G) * LANES), lambda g, i: (i, g)),
                 pl.BlockSpec((1, nc, LANES), lambda g, i: (g, 0, 0)),
                 pl.BlockSpec((1, nc, LANES), lambda g, i: (g, 0, 0))]
                + [win_spec(t) for t in range(nwin)] * 2
                + [pl.BlockSpec((tq, gw_lanes), lambda g, i: (i, g)),
                   pl.BlockSpec((tq, gw_lanes), lambda g, i: (i, g)),
                   pl.BlockSpec(amat.shape, lambda g, i: (0, 0))])
    return pl.pallas_call(
        functools.partial(_nsa_local_kernel, tq=tq, nc=nc, n_slc=n_slc, n_top=n_top),
        grid=(G, S // tq),
        in_specs=in_specs,
        out_specs=[pl.BlockSpec((tq, gw_lanes), lambda g, i: (i, g)),
                   pl.BlockSpec((1, tq, nslp), lambda g, i: (g, i, 0))],
        out_shape=[jax.ShapeDtypeStruct((S, NSA_HEADS * NSA_HEAD_DIM), F32),
                   jax.ShapeDtypeStruct((G, S, nslp), BF16)],
        compiler_params=_cparams(("parallel", "parallel")),
        name="nsa_local",
    )(q, k_cmp, v_cmp, *([kw] * nwin), *([vw] * nwin), gc, gw, amat)


def _nsa_select_kernel(q_ref, k_ref, v_ref, sel_ref, emat_ref, ocw_ref, gs_ref, o_ref, m_s, l_s, a_s,
                       *, tq, tk, nk, tiles_per_selblock):
    i = pl.program_id(1)
    j = pl.program_id(2)
    R = NSA_HEADS // NSA_GROUPS
    rows = R * tq

    @pl.when(j == 0)
    def _init():
        m_s[...] = jnp.full(m_s.shape, NEG, F32)
        l_s[...] = jnp.zeros(l_s.shape, F32)
        a_s[...] = jnp.zeros(a_s.shape, F32)

    last_j = (i * tq + tq - 1) // tk

    @pl.when(j <= last_j)
    def _step():
        q = jnp.concatenate([q_ref[:, r * LANES:(r + 1) * LANES] for r in range(R)], axis=0)
        s = _qk(q, k_ref[...])
        chosen = jnp.dot(sel_ref[0], emat_ref[j % tiles_per_selblock], preferred_element_type=F32)
        qpos = i * tq + lax.broadcasted_iota(I32, (tq, tk), 0)
        kpos = j * tk + lax.broadcasted_iota(I32, (tq, tk), 1)
        ok = (chosen > 0.5) & (kpos <= qpos)
        ok = jnp.concatenate([ok] * R, axis=0)
        s = jnp.where(ok, s, NEG)
        _softmax_step(s, v_ref[...], m_s, l_s, a_s)

    @pl.when(j == nk - 1)
    def _fin():
        o = a_s[...] / l_s[...]
        o_s = _merge_heads([o[r * tq:(r + 1) * tq] for r in range(R)])
        o_ref[...] = (ocw_ref[...] + gs_ref[...] * o_s).astype(o_ref.dtype)


def _nsa_select(q, ks, vs, sel, emat, ocw, gs):
    S = q.shape[0]
    tq, tk = NSA_TQ, min(NSA_TK, S)
    G = NSA_GROUPS
    R = NSA_HEADS // G
    nk = S // tk
    gw_lanes = R * NSA_HEAD_DIM
    tiles_per_selblock = emat.shape[0]

    def kv_map(g, i, j):
        return (jnp.minimum(j, (i * tq + tq - 1) // tk), g)

    def sel_map(g, i, j):
        return (g, i, jnp.minimum(j, (i * tq + tq - 1) // tk) // tiles_per_selblock)

    return pl.pallas_call(
        functools.partial(_nsa_select_kernel, tq=tq, tk=tk, nk=nk, tiles_per_selblock=tiles_per_selblock),
        grid=(G, S // tq, nk),
        in_specs=[pl.BlockSpec((tq, R * LANES), lambda g, i, j: (i, g)),
                  pl.BlockSpec((tk, LANES), kv_map),
                  pl.BlockSpec((tk, LANES), kv_map),
                  pl.BlockSpec((1, tq, LANES), sel_map),
                  pl.BlockSpec(emat.shape, lambda g, i, j: (0, 0, 0)),
                  pl.BlockSpec((tq, gw_lanes), lambda g, i, j: (i, g)),
                  pl.BlockSpec((tq, gw_lanes), lambda g, i, j: (i, g))],
        out_specs=pl.BlockSpec((tq, gw_lanes), lambda g, i, j: (i, g)),
        out_shape=jax.ShapeDtypeStruct((S, NSA_HEADS * NSA_HEAD_DIM), BF16),
        scratch_shapes=[pltpu.VMEM((R * tq, 1), F32), pltpu.VMEM((R * tq, 1), F32), pltpu.VMEM((R * tq, LANES), F32)],
        compiler_params=_cparams(("parallel", "parallel", "arbitrary")),
        name="nsa_select",
    )(q, ks, vs, sel, emat, ocw, gs)


def _router_kernel(h_ref, w_ref, b_ref, e_ref, g_ref):
    x = h_ref[...]
    w = w_ref[...]
    x_hi = x.astype(BF16)
    x_lo = (x - x_hi.astype(F32)).astype(BF16)
    w_hi = w.astype(BF16)
    w_lo = (w - w_hi.astype(F32)).astype(BF16)
    logits = (jnp.dot(x_hi, w_hi, preferred_element_type=F32) + jnp.dot(x_hi, w_lo, preferred_element_type=F32)
              + jnp.dot(x_lo, w_hi, preferred_element_type=F32)) + b_ref[...]
    lane = lax.broadcasted_iota(I32, logits.shape, 1)
    work = jnp.where(lane < N_EXPERTS, logits, -jnp.inf)
    vals, idxs = [], []
    for _ in range(TOP_K):
        mx = jnp.max(work, axis=1, keepdims=True)
        idx = jnp.min(jnp.where(work == mx, lane, LANES), axis=1, keepdims=True)
        vals.append(mx)
        idxs.append(idx)
        work = jnp.where(lane == idx, -jnp.inf, work)
    exps = [jnp.exp(v - vals[0]) for v in vals]
    den = exps[0]
    for t in exps[1:]:
        den = den + t
    e_out = jnp.zeros(logits.shape, I32)
    g_out = jnp.zeros(logits.shape, F32)
    for kk in range(TOP_K):
        e_out = jnp.where(lane == kk, idxs[kk], e_out)
        g_out = jnp.where(lane == kk, exps[kk] / den, g_out)
    e_ref[...] = e_out
    g_ref[...] = g_out


def _router(h, w_router, b_router):
    S, D = h.shape
    tm = min(LN_TM, S)
    w = jnp.pad(w_router, ((0, 0), (0, LANES - N_EXPERTS)))
    b = jnp.pad(b_router, (0, LANES - N_EXPERTS)).reshape(1, LANES)
    return pl.pallas_call(
        _router_kernel,
        grid=(S // tm,),
        in_specs=[pl.BlockSpec((tm, D), lambda i: (i, 0)),
                  pl.BlockSpec((D, LANES), lambda i: (0, 0)),
                  pl.BlockSpec((1, LANES), lambda i: (0, 0))],
        out_specs=[pl.BlockSpec((tm, LANES), lambda i: (i, 0))] * 2,
        out_shape=[jax.ShapeDtypeStruct((S, LANES), I32), jax.ShapeDtypeStruct((S, LANES), F32)],
        compiler_params=_cparams(("parallel",)),
        name="moe_router",
    )(h, w, b)


def _gather_kernel(idx_ref, src_ref, out_ref, sem, *, n, chunk):
    def issue(c):
        def body(r, carry):
            row = c * chunk + r
            pltpu.make_async_copy(src_ref.at[pl.ds(idx_ref[row], 1)], out_ref.at[pl.ds(row, 1)], sem).start()
            return carry
        lax.fori_loop(0, chunk, body, 0)

    def wait_chunk():
        pltpu.make_async_copy(src_ref.at[pl.ds(0, chunk)], out_ref.at[pl.ds(0, chunk)], sem).wait()

    issue(0)

    def outer(c, carry):
        issue(c)
        wait_chunk()
        return carry

    lax.fori_loop(1, n // chunk, outer, 0)
    wait_chunk()


def _gather_rows(src, idx, name):
    n = idx.shape[0]
    chunk = min(GATHER_CHUNK, n)
    assert n % chunk == 0
    return pl.pallas_call(
        functools.partial(_gather_kernel, n=n, chunk=chunk),
        grid_spec=pltpu.PrefetchScalarGridSpec(
            num_scalar_prefetch=1,
            grid=(1,),
            in_specs=[pl.BlockSpec(memory_space=pl.ANY)],
            out_specs=pl.BlockSpec(memory_space=pl.ANY),
            scratch_shapes=[pltpu.SemaphoreType.DMA(())],
        ),
        out_shape=jax.ShapeDtypeStruct((n, src.shape[1]), src.dtype),
        compiler_params=_cparams(("arbitrary",)),
        name=name,
    )(idx, src)


def _expert_kernel(te_ref, nt_ref, x_ref, wg_ref, wl_ref, bg_ref, bl_ref, wd_ref, bd_ref, o_ref):
    @pl.when(pl.program_id(0) < nt_ref[0])
    def _():
        xb = x_ref[...].astype(BF16)
        hg = jnp.dot(xb, wg_ref[0], preferred_element_type=F32) + bg_ref[0]
        hl = jnp.dot(xb, wl_ref[0], preferred_element_type=F32) + bl_ref[0]
        hg = jnp.minimum(hg, SWIGLU_LIMIT)
        hl = jnp.clip(hl, -SWIGLU_LIMIT, SWIGLU_LIMIT)
        act = hg * jax.nn.sigmoid(SWIGLU_ALPHA * hg) * (hl + 1.0)
        o_ref[...] = jnp.dot(act.astype(BF16), wd_ref[0], preferred_element_type=F32) + bd_ref[0]


def _experts(xs, tile_e, n_tiles_used, wg, wl, bg, bl, wd, bd):
    n_rows, D = xs.shape
    tm = MOE_TM
    F = wg.shape[2]
    row_map = lambda t, te, nt: (jnp.minimum(t, nt[0] - 1), 0)
    w_map = lambda t, te, nt: (te[t], 0, 0)
    return pl.pallas_call(
        _expert_kernel,
        grid_spec=pltpu.PrefetchScalarGridSpec(
            num_scalar_prefetch=2,
            grid=(n_rows // tm,),
            in_specs=[pl.BlockSpec((tm, D), row_map),
                      pl.BlockSpec((1, D, F), w_map), pl.BlockSpec((1, D, F), w_map),
                      pl.BlockSpec((1, 1, F), w_map), pl.BlockSpec((1, 1, F), w_map),
                      pl.BlockSpec((1, F, D), w_map), pl.BlockSpec((1, 1, D), w_map)],
            out_specs=pl.BlockSpec((tm, D), row_map),
        ),
        out_shape=jax.ShapeDtypeStruct((n_rows, D), F32),
        compiler_params=_cparams(("arbitrary",)),
        name="moe_experts",
    )(tile_e, n_tiles_used, xs, wg, wl, bg, bl, wd, bd)


def _combine_ln_kernel(y_ref, gate_ref, h_ref, g_ref, b_ref, out_ref):
    D = h_ref.shape[1]
    gate = gate_ref[...]
    y = gate[:, 0:1] * y_ref[:, 0:D]
    for kk in range(1, TOP_K):
        y = y + gate[:, kk:kk + 1] * y_ref[:, kk * D:(kk + 1) * D]
    out_ref[...] = _layer_norm(ALPHA_DN * h_ref[...] + y, g_ref[...], b_ref[...])


def _combine_ln(yg, gate, h, g, b):
    S, D = h.shape
    tm = min(LN_TM, S)
    return pl.pallas_call(
        _combine_ln_kernel,
        grid=(S // tm,),
        in_specs=[pl.BlockSpec((tm, TOP_K * D), lambda i: (i, 0)),
                  pl.BlockSpec((tm, LANES), lambda i: (i, 0)),
                  pl.BlockSpec((tm, D), lambda i: (i, 0)),
                  pl.BlockSpec((1, D), lambda i: (0, 0)),
                  pl.BlockSpec((1, D), lambda i: (0, 0))],
        out_specs=pl.BlockSpec((tm, D), lambda i: (i, 0)),
        out_shape=jax.ShapeDtypeStruct((S, D), F32),
        compiler_params=_cparams(("parallel",)),
        name="moe_combine_ln",
    )(yg, gate, h, g.reshape(1, D), b.reshape(1, D))


def _moe_block(h, w_router, b_router, w_gu, b_gu, w_down, b_down, ln_g, ln_b):
    S, D = h.shape
    E, tm = N_EXPERTS, MOE_TM
    e_full, g_full = _router(h, w_router, b_router)
    top_e = e_full[:, :TOP_K]

    onehot = (top_e[:, :, None] == jnp.arange(E, dtype=I32)[None, None, :]).astype(I32).sum(axis=1)
    csum = jnp.cumsum(onehot, axis=0)
    counts = csum[-1]
    before = csum - onehot
    padded = (counts + tm - 1) // tm * tm
    pends = jnp.cumsum(padded)
    pstarts = pends - padded
    dest = pstarts[top_e] + jnp.take_along_axis(before, top_e, axis=1)
    A = S * TOP_K
    n_tiles = A // tm + E
    n_rows = n_tiles * tm
    tok = jnp.broadcast_to(jnp.arange(S, dtype=I32)[:, None], (S, TOP_K))
    row_tok = jnp.zeros((n_rows,), I32).at[dest.reshape(A)].set(tok.reshape(A))
    n_used = (pends[-1] // tm).astype(I32)
    tile_start = jnp.arange(n_tiles, dtype=I32) * tm
    tile_e = jnp.minimum(jnp.searchsorted(pends, tile_start, side="right"), E - 1).astype(I32)
    tile_e = jnp.where(jnp.arange(n_tiles) < n_used, tile_e, tile_e[jnp.maximum(n_used - 1, 0)])

    xs = _gather_rows(h, row_tok, "moe_dispatch_gather")
    wg = w_gu[:, :, 0::2].astype(BF16)
    wl = w_gu[:, :, 1::2].astype(BF16)
    bg = b_gu[:, None, 0::2]
    bl = b_gu[:, None, 1::2]
    ys = _experts(xs, tile_e, n_used.reshape(1), wg, wl, bg, bl, w_down.astype(BF16), b_down[:, None, :])
    yg = _gather_rows(ys, dest.reshape(A), "moe_combine_gather").reshape(S, TOP_K * D)
    return _combine_ln(yg, g_full, h, ln_g, ln_b)


def _diff_attention(h, w_in, lq1, lk1, lq2, lk2, subln_g, w_out, ln_g, ln_b, tabs, layer_idx):
    dv = DA_HEADS * 2 * DA_HEAD_DIM
    scale = DA_HEAD_DIM ** -0.5
    segs = (Seg(0, dv, "rope", scale, BF16), Seg(dv, dv, "rope", 1.0, BF16), Seg(2 * dv, dv, "plain", 1.0, BF16))
    q, k, v = _project(h, w_in.astype(BF16), tabs["c64"], tabs["s64"], segs, DA_HEAD_DIM // 2, "da_project")
    lam_init = 0.8 - 0.6 * math.exp(-0.3 * layer_idx)
    lam4 = jnp.stack([lq1, lk1, lq2, lk2]).astype(F32)
    o = _flash("da", q, k, v, (lam4, subln_g.reshape(1, -1).astype(F32)), lam_init, "da_attention")
    return _out_ln(o, w_out.astype(BF16), h, ln_g, ln_b, "da_out_ln")


def _mla(h, w_in, q_norm_g, kv_norm_g, w_uq, w_ukv, w_out, ln_g, ln_b, tabs):
    H = MLA_HEADS
    D = h.shape[1]
    qk_dim = MLA_NOPE + MLA_ROPE
    pad_rope = ((0, 0), (MLA_NOPE, LANES - MLA_NOPE - MLA_ROPE))
    w_in_p = jnp.concatenate([w_in[:, :MLA_Q_RANK + MLA_KV_RANK],
                              jnp.pad(w_in[:, MLA_Q_RANK + MLA_KV_RANK:], pad_rope)], axis=1).astype(BF16)
    w_uq_p = jnp.pad(w_uq.reshape(MLA_Q_RANK, H, qk_dim), ((0, 0), (0, 0), (0, LANES - qk_dim)))
    w_uq_p = w_uq_p.reshape(MLA_Q_RANK, H * LANES).astype(BF16)
    w_ukv3 = w_ukv.reshape(MLA_KV_RANK, H, MLA_NOPE + MLA_V)
    w_uk_p = jnp.pad(w_ukv3[:, :, :MLA_NOPE], ((0, 0), (0, 0), (0, LANES - MLA_NOPE)))
    w_uk_p = w_uk_p.reshape(MLA_KV_RANK, H * LANES).astype(BF16)
    w_uv = w_ukv3[:, :, MLA_NOPE:].reshape(MLA_KV_RANK, H * MLA_V).astype(BF16)
    q, k, v = _mla_project(h, w_in_p, w_uq_p, w_uk_p, w_uv, q_norm_g.reshape(1, -1), kv_norm_g.reshape(1, -1),
                           tabs["c32"], tabs["s32"], qk_dim ** -0.5)
    o = _flash("mla", q, k, v, (), 0.0, "mla_attention")
    return _out_ln(o, w_out.astype(BF16), h, ln_g, ln_b, "mla_out_ln")


def _nsa(h, w_in, pos_k, pos_v, ck_w1, ck_w2, cv_w1, cv_w2, w_out, ln_g, ln_b, tabs):
    S, D = h.shape
    H, G, d = NSA_HEADS, NSA_GROUPS, NSA_HEAD_DIM
    R = H // G
    st, Ls = NSA_CMP_STRIDE, NSA_SLC_LEN
    n_cmp = (S - NSA_CMP_LEN) // st + 1
    nc = S // st
    n_slc = S // Ls
    n_top = min(NSA_SLC_TOPK, n_slc)
    nslp = -(-n_slc // LANES) * LANES
    gd = G * d

    def pad_heads(w, n):
        return jnp.pad(w.reshape(D, n, d), ((0, 0), (0, 0), (0, LANES - d))).reshape(D, n * LANES)

    off = np.cumsum([0, H * d] + [gd] * 6)
    wq, wkc, wvc, wks, wvs, wkw, wvw = [w_in[:, off[t]:off[t + 1]] for t in range(7)]
    wgl = w_in[:, off[7]:].reshape(D, H, 3)
    w_a = jnp.concatenate([pad_heads(wq, H), pad_heads(wks, G), pad_heads(wkw, G), pad_heads(wvs, G),
                           pad_heads(wvw, G), wkc, wvc], axis=1).astype(BF16)
    hq, hg = H * LANES, G * LANES
    segs_a = (Seg(0, hq, "rope", d ** -0.5, BF16), Seg(hq, hg, "rope", 1.0, BF16), Seg(hq + hg, hg, "rope", 1.0, BF16),
              Seg(hq + 2 * hg, hg, "plain", 1.0, BF16), Seg(hq + 3 * hg, hg, "plain", 1.0, BF16),
              Seg(hq + 4 * hg, gd, "plain", 1.0, F32), Seg(hq + 4 * hg + gd, gd, "plain", 1.0, F32))
    q, ks, kw, vs, vw, kc, vc = _project(h, w_a, tabs["c64p"], tabs["s64p"], segs_a, d // 2, "nsa_project")
    w_g = jnp.concatenate([jnp.repeat(wgl[:, :, b], d, axis=1) for b in range(3)], axis=1).astype(BF16)
    segs_g = tuple(Seg(b * H * d, H * d, "sigmoid", 1.0, F32) for b in range(3))
    gc, gs, gw = _project(h, w_g, tabs["c64p"], tabs["s64p"], segs_g, d // 2, "nsa_gates")

    def chunks(t):
        return t.reshape(nc, st, G, d).transpose(2, 0, 1, 3).reshape(G, nc, st * d)

    k_cmp = _compress(chunks(kc), pos_k, ck_w1, ck_w2, tabs["ccmp"], tabs["scmp"], True, "nsa_compress_k")
    v_cmp = _compress(chunks(vc), pos_v, cv_w1, cv_w2, tabs["ccmp"], tabs["scmp"], False, "nsa_compress_v")

    ratio = Ls // st
    amat = np.zeros((nc, nslp), np.float32)
    for jj in range(n_slc):
        for mm in range(ratio):
            for nn in range(NSA_CMP_LEN // st):
                c = ratio * jj + mm - nn
                if 0 <= c < n_cmp:
                    amat[c, jj] += 1.0
    ocw, sel = _nsa_local(q, k_cmp, v_cmp, kw, vw, gc, gw, jnp.asarray(amat, BF16), n_slc, n_top)

    tk = min(NSA_TK, S)
    per_tile = tk // Ls
    tiles_per_selblock = LANES // per_tile
    emat = np.zeros((tiles_per_selblock, LANES, tk), np.float32)
    for u in range(tiles_per_selblock):
        for t in range(tk):
            emat[u, per_tile * u + t // Ls, t] = 1.0
    o = _nsa_select(q, ks, vs, sel, jnp.asarray(emat, BF16), ocw, gs)
    return _out_ln(o, w_out.astype(BF16), h, ln_g, ln_b, "nsa_out_ln")


def _rope_tables(S):
    pos = jnp.arange(S)
    c64, s64 = _rope_angles(pos, DA_HEAD_DIM)
    c32, s32 = _rope_angles(pos, MLA_ROPE)
    nc = S // NSA_CMP_STRIDE
    cc, sc = _rope_angles(jnp.arange(nc) * NSA_CMP_STRIDE + NSA_CMP_LEN - 1, NSA_HEAD_DIM)
    one = lambda n, w: jnp.ones((n, w), F32)
    zero = lambda n, w: jnp.zeros((n, w), F32)
    cat = lambda *a: jnp.concatenate(a, axis=1)
    return {
        "c64": cat(c64, c64, c64, c64), "s64": cat(-s64, s64, -s64, s64),
        "c64p": cat(c64, c64, one(S, 64)), "s64p": cat(-s64, s64, zero(S, 64)),
        "c32": cat(one(S, 64), c32, c32, one(S, 32)), "s32": cat(zero(S, 64), -s32, s32, zero(S, 32)),
        "ccmp": cat(cc, cc, one(nc, 64)), "scmp": cat(-sc, sc, zero(nc, 64)),
    }


def kernel(x, da_w_in, da_lambda_q1, da_lambda_k1, da_lambda_q2, da_lambda_k2, da_subln, da_w_out, mla_w_in, mla_q_norm, mla_kv_norm, mla_w_uq, mla_w_ukv, mla_w_out, nsa_w_in, nsa_cmp_pos_k, nsa_cmp_pos_v, nsa_cmp_k_w1, nsa_cmp_k_w2, nsa_cmp_v_w1, nsa_cmp_v_w2, nsa_w_out, ln1_g, ln1_b, ln2_g, ln2_b, moe_w_router, moe_b_router, moe_w_gu, moe_b_gu, moe_w_down, moe_b_down):
    B, S, D = x.shape
    assert B == 1 and D == D_MODEL
    tabs = _rope_tables(S)
    h = x.reshape(S, D)
    for i in range(DEPTH):
        m, j = i % N_MIXERS, i // N_MIXERS
        if m == 0:
            h = _diff_attention(h, da_w_in[j], da_lambda_q1[j], da_lambda_k1[j], da_lambda_q2[j], da_lambda_k2[j],
                                da_subln[j], da_w_out[j], ln1_g[i], ln1_b[i], tabs, i)
        elif m == 1:
            h = _mla(h, mla_w_in[j], mla_q_norm[j], mla_kv_norm[j], mla_w_uq[j], mla_w_ukv[j], mla_w_out[j],
                     ln1_g[i], ln1_b[i], tabs)
        else:
            h = _nsa(h, nsa_w_in[j], nsa_cmp_pos_k[j], nsa_cmp_pos_v[j], nsa_cmp_k_w1[j], nsa_cmp_k_w2[j],
                     nsa_cmp_v_w1[j], nsa_cmp_v_w2[j], nsa_w_out[j], ln1_g[i], ln1_b[i], tabs)
        h = _moe_block(h, moe_w_router[i], moe_b_router[i], moe_w_gu[i], moe_b_gu[i], moe_w_down[i], moe_b_down[i],
                       ln2_g[i], ln2_b[i])
    return h.reshape(B, S, D)
```

```python
import collections
import functools
import math

import numpy as np
import jax
import jax.numpy as jnp
from jax import lax
from jax.experimental import pallas as pl
from jax.experimental.pallas import tpu as pltpu

F32 = jnp.float32
BF16 = jnp.bfloat16
I32 = jnp.int32

D_MODEL = 1024
DEPTH = 4
N_MIXERS = 3
ROPE_THETA = 10000.0
LN_EPS = 1e-5
RMS_EPS = 1e-6
NEG = -1e30
ALPHA_DN = (2 * DEPTH) ** 0.25
LOG2E = math.log2(math.e)

DA_HEADS = 8
DA_HEAD_DIM = 64
MLA_HEADS = 16
MLA_NOPE = 64
MLA_ROPE = 32
MLA_V = 64
MLA_Q_RANK = 256
MLA_KV_RANK = 128
NSA_HEADS = 16
NSA_GROUPS = 4
NSA_HEAD_DIM = 64
NSA_CMP_LEN = 32
NSA_CMP_STRIDE = 16
NSA_CMP_HIDDEN = 256
NSA_SLC_LEN = 64
NSA_SLC_TOPK = 16
NSA_WINDOW = 512
NSA_FORCE = 1e9
N_EXPERTS = 32
TOP_K = 4
D_EXPERT = 1024
SWIGLU_LIMIT = 7.0
SWIGLU_ALPHA = 1.702

LANES = 128
SUB = 8
VMEM_LIMIT = 48 * 1024 * 1024

PROJ_TM = 256
ATT_TQ = 512
ATT_TK = 512
KV_UNROLL = 2
NSA_TQ = 128
NSA_TK = 512
MOE_TM = 512
LN_TM = 512
COMBINE_TM = 256


def _cparams(sem):
    return pltpu.CompilerParams(dimension_semantics=sem, vmem_limit_bytes=VMEM_LIMIT)


def _rope_angles(pos, dim):
    inv = ROPE_THETA ** (-jnp.arange(0, dim, 2, dtype=F32) / dim)
    ang = pos.astype(F32)[:, None] * inv[None, :]
    return jnp.cos(ang), jnp.sin(ang)


def _rope128(t, c, sg, half):
    lane = lax.broadcasted_iota(I32, t.shape, 1)
    first = (lane % (2 * half)) < half
    partner = jnp.where(first, pltpu.roll(t, LANES - half, 1), pltpu.roll(t, half, 1))
    return t * c + partner * sg


Seg = collections.namedtuple("Seg", "start width mode scale dtype")


def _proj_kernel(*refs, segs, half):
    x_ref, w_ref, c_ref, s_ref = refs[:4]
    out_refs = refs[4:]
    xb = x_ref[...].astype(BF16)
    for seg, o_ref in zip(segs, out_refs):
        acc = jnp.dot(xb, w_ref[:, seg.start:seg.start + seg.width], preferred_element_type=F32)
        if seg.mode == "rope":
            c = c_ref[...]
            sg = s_ref[...]
            for g in range(seg.width // LANES):
                t = _rope128(acc[:, g * LANES:(g + 1) * LANES], c, sg, half)
                o_ref[:, g * LANES:(g + 1) * LANES] = (t * seg.scale).astype(o_ref.dtype)
        elif seg.mode == "sigmoid":
            o_ref[...] = jax.nn.sigmoid(acc).astype(o_ref.dtype)
        else:
            o_ref[...] = (acc * seg.scale).astype(o_ref.dtype) if seg.scale != 1.0 else acc.astype(o_ref.dtype)


def _project(x, w, tab_c, tab_s, segs, half, name):
    S, K = x.shape
    tm = min(PROJ_TM, S)
    return pl.pallas_call(
        functools.partial(_proj_kernel, segs=segs, half=half),
        grid=(S // tm,),
        in_specs=[
            pl.BlockSpec((tm, K), lambda i: (i, 0)),
            pl.BlockSpec(w.shape, lambda i: (0, 0)),
            pl.BlockSpec((tm, LANES), lambda i: (i, 0)),
            pl.BlockSpec((tm, LANES), lambda i: (i, 0)),
        ],
        out_specs=[pl.BlockSpec((tm, s.width), lambda i: (i, 0)) for s in segs],
        out_shape=[jax.ShapeDtypeStruct((S, s.width), s.dtype) for s in segs],
        compiler_params=_cparams(("parallel",)),
        name=name,
    )(x, w, tab_c, tab_s)


def _qk(q, k):
    return lax.dot_general(q, k, (((1,), (1,)), ((), ())), preferred_element_type=F32)


ONES_ROWS = 16


def _flash_update(s_t, vt_ext, m_ref, acc_ref, st):
    m_old = m_ref[st]
    m_new = jnp.maximum(m_old, jnp.max(s_t, axis=0, keepdims=True))
    alpha = jnp.exp2(m_old - m_new)
    p_t = jnp.exp2(s_t - m_new).astype(BF16)
    acc_ref[st] = alpha * acc_ref[st] + jnp.dot(vt_ext, p_t, preferred_element_type=F32)
    m_ref[st] = m_new


def _flash_kernel(*refs, variant, tq, tk, lam_init):
    if variant == "da":
        lam_ref, g_ref, q_ref, k_ref, vt_ref, o_ref, qa, qb, m_ref, acc_ref = refs
    else:
        q_ref, k_ref, vt_ref, o_ref, m_ref, acc_ref = refs
    i = pl.program_id(1)
    dv = vt_ref.shape[0]

    if variant == "da":
        q = q_ref[...]
        lane = lax.broadcasted_iota(I32, q.shape, 1)
        qa[...] = jnp.where(lane < DA_HEAD_DIM, q, jnp.zeros_like(q))
        qb[...] = jnp.where(lane >= DA_HEAD_DIM, q, jnp.zeros_like(q))
    m_ref[...] = jnp.full(m_ref.shape, NEG, F32)
    acc_ref[...] = jnp.zeros(acc_ref.shape, F32)
    ones = jnp.ones((ONES_ROWS, tk), BF16)

    def tiles(js, masked):
        work = []
        for j in js:
            start = pl.multiple_of(j * tk, tk)
            k = k_ref[pl.ds(start, tk), :]
            if variant == "da":
                ops = ((k, qa[...]), (k, qb[...]))
            else:
                ops = ((k[:, :LANES], q_ref[:, :LANES]), (k[:, LANES:], q_ref[:, LANES:]))
            work.append((j, start, [_qk(kk, qq) for kk, qq in ops]))
        for j, start, scores in work:
            vt_ext = jnp.concatenate([vt_ref[:, pl.ds(start, tk)], ones], axis=0)
            for st, s_t in enumerate(scores):
                if masked:
                    kpos = j * tk + lax.broadcasted_iota(I32, s_t.shape, 0)
                    qpos = i * tq + lax.broadcasted_iota(I32, s_t.shape, 1)
                    s_t = jnp.where(kpos <= qpos, s_t, NEG)
                _flash_update(s_t, vt_ext, m_ref, acc_ref, st)

    n_full = (i * tq + 1) // tk

    def body(p, carry):
        tiles([p * KV_UNROLL + u for u in range(KV_UNROLL)], False)
        return carry

    lax.fori_loop(0, n_full // KV_UNROLL, body, 0)
    for u in range(KV_UNROLL - 1):
        @pl.when(u < n_full % KV_UNROLL)
        def _():
            tiles([n_full - n_full % KV_UNROLL + u], False)
    tiles([n_full + d for d in range(max(1, tq // tk))], True)

    outs = []
    for st in range(2):
        acc = acc_ref[st]
        outs.append((acc[:dv] / acc[dv:dv + 1]).T)
    o1, o2 = outs
    if variant == "da":
        lam4 = lam_ref[...]
        lam = (jnp.exp(jnp.sum(lam4[0:1] * lam4[1:2], axis=1, keepdims=True))
               - jnp.exp(jnp.sum(lam4[2:3] * lam4[3:4], axis=1, keepdims=True)) + lam_init)
        o = o1 - lam * o2
        ms = jnp.mean(o * o, axis=-1, keepdims=True)
        o = o * lax.rsqrt(ms + LN_EPS) * g_ref[...] * (1.0 - lam_init)
    else:
        lane = lax.broadcasted_iota(I32, o1.shape, 1)
        o = jnp.where(lane < MLA_V, o1, o2)
    o_ref[...] = o.astype(o_ref.dtype)


def _flash(variant, q, k, vt, extra, lam_init, name):
    S = q.shape[0]
    tq, tk = min(ATT_TQ, S), min(ATT_TK, S)
    assert tk % tq == 0 or tq % tk == 0
    qw = LANES if variant == "da" else 2 * LANES
    n_groups = q.shape[1] // qw
    in_specs = [
        pl.BlockSpec((tq, qw), lambda h, i: (i, h)),
        pl.BlockSpec((S, qw), lambda h, i: (0, h)),
        pl.BlockSpec((LANES, S), lambda h, i: (h, 0)),
    ]
    scratch = [pltpu.VMEM((2, 1, tq), F32), pltpu.VMEM((2, LANES + ONES_ROWS, tq), F32)]
    if variant == "da":
        in_specs = [pl.BlockSpec(extra[0].shape, lambda h, i: (0, 0)),
                    pl.BlockSpec(extra[1].shape, lambda h, i: (0, 0))] + in_specs
        scratch = [pltpu.VMEM((tq, LANES), BF16)] * 2 + scratch
    return pl.pallas_call(
        functools.partial(_flash_kernel, variant=variant, tq=tq, tk=tk, lam_init=lam_init),
        grid=(n_groups, S // tq),
        in_specs=in_specs,
        out_specs=pl.BlockSpec((tq, LANES), lambda h, i: (i, h)),
        out_shape=jax.ShapeDtypeStruct((S, n_groups * LANES), BF16),
        scratch_shapes=scratch,
        compiler_params=_cparams(("parallel", "parallel")),
        name=name,
    )(*extra, q, k, vt)


def _layer_norm(x, g, b):
    mu = jnp.mean(x, axis=-1, keepdims=True)
    xc = x - mu
    var = jnp.mean(xc * xc, axis=-1, keepdims=True)
    return xc * lax.rsqrt(var + LN_EPS) * g + b


def _out_ln_kernel(o_ref, w_ref, h_ref, g_ref, b_ref, out_ref):
    y = jnp.dot(o_ref[...], w_ref[...], preferred_element_type=F32)
    out_ref[...] = _layer_norm(ALPHA_DN * h_ref[...] + y, g_ref[...], b_ref[...])


def _out_ln(o, w, h, g, b, name):
    S, K = o.shape
    D = h.shape[1]
    tm = min(LN_TM, S)
    return pl.pallas_call(
        _out_ln_kernel,
        grid=(S // tm,),
        in_specs=[
            pl.BlockSpec((tm, K), lambda i: (i, 0)),
            pl.BlockSpec((K, D), lambda i: (0, 0)),
            pl.BlockSpec((tm, D), lambda i: (i, 0)),
            pl.BlockSpec((1, D), lambda i: (0, 0)),
            pl.BlockSpec((1, D), lambda i: (0, 0)),
        ],
        out_specs=pl.BlockSpec((tm, D), lambda i: (i, 0)),
        out_shape=jax.ShapeDtypeStruct((S, D), F32),
        compiler_params=_cparams(("parallel",)),
        name=name,
    )(o, w, h, g.reshape(1, D), b.reshape(1, D))


def _rms(x, g, eps):
    return x * lax.rsqrt(jnp.mean(x * x, axis=-1, keepdims=True) + eps) * g


def _mla_proj_kernel(x_ref, win_ref, wuq_ref, wuk_ref, wuv_ref, gq_ref, gkv_ref, c_ref, s_ref,
                     q_ref, k_ref, v_ref, *, scale):
    xb = x_ref[...].astype(BF16)
    p1 = jnp.dot(xb, win_ref[...], preferred_element_type=F32)
    c_q = _rms(p1[:, :MLA_Q_RANK], gq_ref[...], RMS_EPS).astype(BF16)
    c_kv = _rms(p1[:, MLA_Q_RANK:MLA_Q_RANK + MLA_KV_RANK], gkv_ref[...], RMS_EPS).astype(BF16)
    c = c_ref[...]
    sg = s_ref[...]
    half = MLA_ROPE // 2
    k_rope = _rope128(p1[:, MLA_Q_RANK + MLA_KV_RANK:], c, sg, half)
    q = jnp.dot(c_q, wuq_ref[...], preferred_element_type=F32)
    k = jnp.dot(c_kv, wuk_ref[...], preferred_element_type=F32)
    for h in range(MLA_HEADS):
        sl = slice(h * LANES, (h + 1) * LANES)
        q_ref[:, sl] = (_rope128(q[:, sl], c, sg, half) * scale).astype(q_ref.dtype)
        k_ref[:, sl] = (k[:, sl] + k_rope).astype(k_ref.dtype)
    v_ref[...] = jnp.dot(c_kv, wuv_ref[...], preferred_element_type=F32).astype(v_ref.dtype)


def _mla_project(x, w_in, w_uq, w_uk, w_uv, gq, gkv, tab_c, tab_s, scale):
    S, D = x.shape
    tm = min(PROJ_TM, S)
    full = lambda a: pl.BlockSpec(a.shape, lambda i: (0,) * a.ndim)
    row = lambda w: pl.BlockSpec((tm, w), lambda i: (i, 0))
    hw = MLA_HEADS * LANES
    return pl.pallas_call(
        functools.partial(_mla_proj_kernel, scale=scale),
        grid=(S // tm,),
        in_specs=[row(D), full(w_in), full(w_uq), full(w_uk), full(w_uv), full(gq), full(gkv), row(LANES), row(LANES)],
        out_specs=[row(hw), row(hw), row(MLA_HEADS * MLA_V)],
        out_shape=[jax.ShapeDtypeStruct((S, hw), BF16), jax.ShapeDtypeStruct((S, hw), BF16),
                   jax.ShapeDtypeStruct((S, MLA_HEADS * MLA_V), BF16)],
        compiler_params=_cparams(("parallel",)),
        name="mla_project",
    )(x, w_in, w_uq, w_uk, w_uv, gq, gkv, tab_c, tab_s)


def _gelu_tanh(x):
    return 0.5 * x * (1.0 + jnp.tanh(math.sqrt(2.0 / math.pi) * (x + 0.044715 * (x * x * x))))


def _compress_kernel(x_ref, pa_ref, pb_ref, w1a_ref, w1b_ref, w2_ref, c_ref, s_ref, o_ref, *, nc, rope):
    x = x_ref[0]
    ha = jnp.dot((x + pa_ref[...]).astype(BF16), w1a_ref[...], preferred_element_type=F32)
    hb = jnp.dot((x + pb_ref[...]).astype(BF16), w1b_ref[...], preferred_element_type=F32)
    hid = ha + pltpu.roll(hb, nc - 1, 0)
    out = jnp.dot(_gelu_tanh(hid).astype(BF16), w2_ref[...], preferred_element_type=F32)
    if rope:
        out = _rope128(out, c_ref[...], s_ref[...], NSA_HEAD_DIM // 2)
    o_ref[0] = out.astype(o_ref.dtype)


def _compress(xc, pos, w1, w2, tab_c, tab_s, rope, name):
    G, nc, cw = xc.shape
    half = NSA_CMP_LEN // 2
    pa = pos[:half].reshape(1, cw)
    pb = pos[half:].reshape(1, cw)
    w1a = w1[:cw].astype(BF16)
    w1b = w1[cw:].astype(BF16)
    w2p = jnp.pad(w2, ((0, 0), (0, LANES - w2.shape[1]))).astype(BF16)
    full = lambda a: pl.BlockSpec(a.shape, lambda g: (0,) * a.ndim)
    return pl.pallas_call(
        functools.partial(_compress_kernel, nc=nc, rope=rope),
        grid=(G,),
        in_specs=[pl.BlockSpec((1, nc, cw), lambda g: (g, 0, 0)), full(pa), full(pb), full(w1a), full(w1b),
                  full(w2p), full(tab_c), full(tab_s)],
        out_specs=pl.BlockSpec((1, nc, LANES), lambda g: (g, 0, 0)),
        out_shape=jax.ShapeDtypeStruct((G, nc, LANES), BF16),
        compiler_params=_cparams(("parallel",)),
        name=name,
    )(xc, pa, pb, w1a, w1b, w2p, tab_c, tab_s)


def _split3(x):
    hi = x.astype(BF16)
    r = x - hi.astype(F32)
    mid = r.astype(BF16)
    lo = (r - mid.astype(F32)).astype(BF16)
    return hi, mid, lo


def _merge_heads(heads):
    pairs = [heads[r] + pltpu.roll(heads[r + 1], NSA_HEAD_DIM, 1) for r in (0, 2)]
    return jnp.concatenate(pairs, axis=1)


def _nsa_local_kernel(q_ref, kc_ref, vc_ref, kw0, kw1, kw2, kw3, kw4, vw0, vw1, vw2, vw3, vw4,
                      gc_ref, gw_ref, amat_ref, o_ref, sel_ref, *, tq, nc, n_slc, n_top):
    i = pl.program_id(1)
    R = NSA_HEADS // NSA_GROUPS
    q = jnp.concatenate([q_ref[:, r * LANES:(r + 1) * LANES] for r in range(R)], axis=0)
    rows = R * tq
    qpos = i * tq + lax.broadcasted_iota(I32, (rows, 1), 0) % tq

    s = _qk(q, kc_ref[0])
    cmp_end = lax.broadcasted_iota(I32, (rows, nc), 1) * NSA_CMP_STRIDE + (NSA_CMP_LEN - 1)
    m_c = cmp_end <= qpos
    s = jnp.where(m_c, s, NEG)
    e = jnp.exp2(s - jnp.max(s, axis=1, keepdims=True))
    p_c = e / jnp.sum(e, axis=1, keepdims=True)
    p_c = p_c * (qpos >= NSA_CMP_LEN - 1).astype(F32)
    o_c = jnp.dot(p_c.astype(BF16), vc_ref[0], preferred_element_type=F32)

    p_g = p_c[0:tq]
    for r in range(1, R):
        p_g = p_g + p_c[r * tq:(r + 1) * tq]
    amat = amat_ref[...]
    imp = None
    for part in _split3(p_g):
        t = jnp.dot(part, amat, preferred_element_type=F32)
        imp = t if imp is None else imp + t
    nslp = imp.shape[1]
    jb = lax.broadcasted_iota(I32, (tq, nslp), 1)
    qp = i * tq + lax.broadcasted_iota(I32, (tq, 1), 0)
    cur = qp // NSA_SLC_LEN
    valid = jb <= cur
    forced = valid & ((jb == 0) | (jb >= cur - 1))
    score = jnp.where(forced, NSA_FORCE, jnp.where(valid, imp, -1.0))
    score = jnp.where(jb < n_slc, score, -2.0)
    sel = jnp.zeros((tq, nslp), F32)
    for _ in range(n_top):
        mx = jnp.max(score, axis=1, keepdims=True)
        idx = jnp.min(jnp.where(score == mx, jb, nslp), axis=1, keepdims=True)
        hit = jb == idx
        sel = jnp.where(hit, 1.0, sel)
        score = jnp.where(hit, -jnp.inf, score)
    sel_ref[0] = sel.astype(sel_ref.dtype)

    nwin = NSA_WINDOW // tq + 1
    k_w = jnp.concatenate([r[...] for r in (kw0, kw1, kw2, kw3, kw4)], axis=0)
    v_w = jnp.concatenate([r[...] for r in (vw0, vw1, vw2, vw3, vw4)], axis=0)
    s = _qk(q, k_w)
    kpos = (i - (nwin - 1)) * tq + lax.broadcasted_iota(I32, (rows, nwin * tq), 1)
    m_w = (kpos <= qpos) & (kpos > qpos - NSA_WINDOW) & (kpos >= 0)
    s = jnp.where(m_w, s, NEG)
    e = jnp.exp2(s - jnp.max(s, axis=1, keepdims=True))
    p_w = e / jnp.sum(e, axis=1, keepdims=True)
    o_w = jnp.dot(p_w.astype(BF16), v_w, preferred_element_type=F32)

    oc = _merge_heads([o_c[r * tq:(r + 1) * tq] for r in range(R)])
    ow = _merge_heads([o_w[r * tq:(r + 1) * tq] for r in range(R)])
    o_ref[...] = gc_ref[...] * oc + gw_ref[...] * ow


def _nsa_local(q, k_cmp, v_cmp, kw, vw, gc, gw, amat, n_slc, n_top):
    S = q.shape[0]
    tq = NSA_TQ
    G = NSA_GROUPS
    nc = k_cmp.shape[1]
    nslp = amat.shape[1]
    gw_lanes = (NSA_HEADS // G) * NSA_HEAD_DIM
    nwin = NSA_WINDOW // tq + 1
    assert nwin == 5

    def win_spec(t):
        return pl.BlockSpec((tq, LANES), lambda g, i: (jnp.maximum(i - (nwin - 1) + t, 0), g))

    in_specs = ([pl.BlockSpec((tq, (NSA_HEADS // G) * LANES), lambda g, i: (i, g)),
                 pl.BlockSpec((1, nc, LANES), lambda g, i: (g, 0, 0)),
                 pl.BlockSpec((1, nc, LANES), lambda g, i: (g, 0, 0))]
                + [win_spec(t) for t in range(nwin)] * 2
                + [pl.BlockSpec((tq, gw_lanes), lambda g, i: (i, g)),
                   pl.BlockSpec((tq, gw_lanes), lambda g, i: (i, g)),
                   pl.BlockSpec(amat.shape, lambda g, i: (0, 0))])
    return pl.pallas_call(
        functools.partial(_nsa_local_kernel, tq=tq, nc=nc, n_slc=n_slc, n_top=n_top),
        grid=(G, S // tq),
        in_specs=in_specs,
        out_specs=[pl.BlockSpec((tq, gw_lanes), lambda g, i: (i, g)),
                   pl.BlockSpec((1, tq, nslp), lambda g, i: (g, i, 0))],
        out_shape=[jax.ShapeDtypeStruct((S, NSA_HEADS * NSA_HEAD_DIM), F32),
                   jax.ShapeDtypeStruct((G, S, nslp), BF16)],
        compiler_params=_cparams(("parallel", "parallel")),
        name="nsa_local",
    )(q, k_cmp, v_cmp, *([kw] * nwin), *([vw] * nwin), gc, gw, amat)


def _nsa_select_kernel(q_ref, k_ref, vt_ref, sel_ref, emat_ref, ocw_ref, gs_ref, o_ref, m_ref, acc_ref,
                       *, tq, tk, tiles_per_selblock):
    i = pl.program_id(1)
    R = NSA_HEADS // NSA_GROUPS
    dv = vt_ref.shape[0]
    q = jnp.concatenate([q_ref[:, r * LANES:(r + 1) * LANES] for r in range(R)], axis=0)
    m_ref[...] = jnp.full(m_ref.shape, NEG, F32)
    acc_ref[...] = jnp.zeros(acc_ref.shape, F32)
    ones = jnp.ones((ONES_ROWS, tk), BF16)

    def tiles(js, masked):
        work = []
        for j in js:
            start = pl.multiple_of(j * tk, tk)
            s_t = _qk(k_ref[pl.ds(start, tk), :], q)
            sel_blk = sel_ref[0, :, pl.ds(pl.multiple_of((j // tiles_per_selblock) * LANES, LANES), LANES)]
            chosen_t = _qk(emat_ref[j % tiles_per_selblock], sel_blk)
            work.append((j, start, s_t, chosen_t))
        for j, start, s_t, chosen_t in work:
            vt_ext = jnp.concatenate([vt_ref[:, pl.ds(start, tk)], ones], axis=0)
            ok = chosen_t > 0.5
            if masked:
                kpos = j * tk + lax.broadcasted_iota(I32, chosen_t.shape, 0)
                qpos = i * tq + lax.broadcasted_iota(I32, chosen_t.shape, 1)
                ok = ok & (kpos <= qpos)
            bias = jnp.where(ok, 0.0, NEG)
            s_t = s_t + jnp.concatenate([bias] * R, axis=1)
            _flash_update(s_t, vt_ext, m_ref, acc_ref, 0)

    n_full = (i * tq + 1) // tk

    def body(p, carry):
        tiles([p * KV_UNROLL + u for u in range(KV_UNROLL)], False)
        return carry

    lax.fori_loop(0, n_full // KV_UNROLL, body, 0)
    for u in range(KV_UNROLL - 1):
        @pl.when(u < n_full % KV_UNROLL)
        def _():
            tiles([n_full - n_full % KV_UNROLL + u], False)
    tiles([n_full], True)

    acc = acc_ref[0]
    o_t = acc[:dv] / acc[dv:dv + 1]
    o_s = jnp.concatenate([o_t[:NSA_HEAD_DIM, r * tq:(r + 1) * tq] for r in range(R)], axis=0).T
    o_ref[...] = (ocw_ref[...] + gs_ref[...] * o_s).astype(o_ref.dtype)


def _nsa_select(q, ks, vst, sel, emat, ocw, gs):
    S = q.shape[0]
    tq, tk = NSA_TQ, min(NSA_TK, S)
    G = NSA_GROUPS
    R = NSA_HEADS // G
    gw_lanes = R * NSA_HEAD_DIM
    nslp = sel.shape[2]
    tiles_per_selblock = emat.shape[0]
    return pl.pallas_call(
        functools.partial(_nsa_select_kernel, tq=tq, tk=tk, tiles_per_selblock=tiles_per_selblock),
        grid=(G, S // tq),
        in_specs=[pl.BlockSpec((tq, R * LANES), lambda g, i: (i, g)),
                  pl.BlockSpec((S, LANES), lambda g, i: (0, g)),
                  pl.BlockSpec((LANES, S), lambda g, i: (g, 0)),
                  pl.BlockSpec((1, tq, nslp), lambda g, i: (g, i, 0)),
                  pl.BlockSpec(emat.shape, lambda g, i: (0, 0, 0)),
                  pl.BlockSpec((tq, gw_lanes), lambda g, i: (i, g)),
                  pl.BlockSpec((tq, gw_lanes), lambda g, i: (i, g))],
        out_specs=pl.BlockSpec((tq, gw_lanes), lambda g, i: (i, g)),
        out_shape=jax.ShapeDtypeStruct((S, NSA_HEADS * NSA_HEAD_DIM), BF16),
        scratch_shapes=[pltpu.VMEM((1, 1, R * tq), F32), pltpu.VMEM((1, LANES + ONES_ROWS, R * tq), F32)],
        compiler_params=_cparams(("parallel", "parallel")),
        name="nsa_select",
    )(q, ks, vst, sel, emat, ocw, gs)


def _router_kernel(h_ref, w_ref, b_ref, e_ref, g_ref):
    x = h_ref[...]
    w = w_ref[...]
    x_hi = x.astype(BF16)
    x_lo = (x - x_hi.astype(F32)).astype(BF16)
    w_hi = w.astype(BF16)
    w_lo = (w - w_hi.astype(F32)).astype(BF16)
    logits = (jnp.dot(x_hi, w_hi, preferred_element_type=F32) + jnp.dot(x_hi, w_lo, preferred_element_type=F32)
              + jnp.dot(x_lo, w_hi, preferred_element_type=F32)) + b_ref[...]
    lane = lax.broadcasted_iota(I32, logits.shape, 1)
    work = jnp.where(lane < N_EXPERTS, logits, -jnp.inf)
    vals, idxs = [], []
    for _ in range(TOP_K):
        mx = jnp.max(work, axis=1, keepdims=True)
        idx = jnp.min(jnp.where(work == mx, lane, LANES), axis=1, keepdims=True)
        vals.append(mx)
        idxs.append(idx)
        work = jnp.where(lane == idx, -jnp.inf, work)
    exps = [jnp.exp(v - vals[0]) for v in vals]
    den = exps[0]
    for t in exps[1:]:
        den = den + t
    e_out = jnp.zeros(logits.shape, I32)
    g_out = jnp.zeros(logits.shape, F32)
    for kk in range(TOP_K):
        e_out = jnp.where(lane == kk, idxs[kk], e_out)
        g_out = jnp.where(lane == kk, exps[kk] / den, g_out)
    e_ref[...] = e_out
    g_ref[...] = g_out


def _router(h, w_router, b_router):
    S, D = h.shape
    tm = min(LN_TM, S)
    w = jnp.pad(w_router, ((0, 0), (0, LANES - N_EXPERTS)))
    b = jnp.pad(b_router, (0, LANES - N_EXPERTS)).reshape(1, LANES)
    return pl.pallas_call(
        _router_kernel,
        grid=(S // tm,),
        in_specs=[pl.BlockSpec((tm, D), lambda i: (i, 0)),
                  pl.BlockSpec((D, LANES), lambda i: (0, 0)),
                  pl.BlockSpec((1, LANES), lambda i: (0, 0))],
        out_specs=[pl.BlockSpec((tm, LANES), lambda i: (i, 0))] * 2,
        out_shape=[jax.ShapeDtypeStruct((S, LANES), I32), jax.ShapeDtypeStruct((S, LANES), F32)],
        compiler_params=_cparams(("parallel",)),
        name="moe_router",
    )(h, w, b)


def _start_row_copies(idx_ref, idx_base, n, src_ref, dst_ref, dst_base, sem):
    def body(r, carry):
        tok = idx_ref[idx_base + r]
        pltpu.make_async_copy(src_ref.at[pl.ds(pl.multiple_of(tok * SUB, SUB), SUB)],
                              dst_ref.at[pl.ds(pl.multiple_of((dst_base + r) * SUB, SUB), SUB)], sem).start()
        return carry
    lax.fori_loop(0, n, body, 0, unroll=8)


def _wait_row_copies(n, src_ref, dst_ref, dst_base, sem):
    pltpu.make_async_copy(src_ref.at[pl.ds(0, n * SUB)], dst_ref.at[pl.ds(dst_base * SUB, n * SUB)], sem).wait()


def _rows_to_matrix(ref, base, n):
    return jnp.concatenate([ref[pl.ds(base * SUB + s, n, stride=SUB), :] for s in range(SUB)], axis=1)


def _expert_kernel(te_ref, nt_ref, tok_ref, h_ref, wgu_ref, bgu_ref, wd_ref, bd_ref, pmat_ref, o_ref,
                   xbuf, sems, wperm, *, tm):
    t = pl.program_id(0)
    nt = nt_ref[0]
    slot = t % 2

    @pl.when(t == 0)
    def _():
        _start_row_copies(tok_ref, 0, tm, h_ref, xbuf, 0, sems.at[0])

    @pl.when(t + 1 < nt)
    def _():
        _start_row_copies(tok_ref, (t + 1) * tm, tm, h_ref, xbuf, (1 - slot) * tm, sems.at[1 - slot])

    changed = jnp.logical_or(t == 0, te_ref[t] != te_ref[jnp.maximum(t - 1, 0)])

    @pl.when(jnp.logical_and(t < nt, changed))
    def _():
        blk = pmat_ref.shape[0]
        for c in range(wperm.shape[1] // blk):
            sl = slice(c * blk, (c + 1) * blk)
            wperm[:, sl] = jnp.dot(wgu_ref[0, :, sl], pmat_ref[...], preferred_element_type=F32).astype(BF16)

    @pl.when(t < nt)
    def _():
        _wait_row_copies(tm, h_ref, xbuf, slot * tm, sems.at[slot])
        xb = _rows_to_matrix(xbuf, slot * tm, tm).astype(BF16)
        hgu = jnp.dot(xb, wperm[...], preferred_element_type=F32) + bgu_ref[0]
        acts = []
        for c in range(hgu.shape[1] // (2 * LANES)):
            hg = jnp.minimum(hgu[:, 2 * c * LANES:(2 * c + 1) * LANES], SWIGLU_LIMIT)
            hl = jnp.clip(hgu[:, (2 * c + 1) * LANES:(2 * c + 2) * LANES], -SWIGLU_LIMIT, SWIGLU_LIMIT)
            acts.append((hg * jax.nn.sigmoid(SWIGLU_ALPHA * hg) * (hl + 1.0)).astype(BF16))
        act = jnp.concatenate(acts, axis=1)
        y = jnp.dot(act, wd_ref[0], preferred_element_type=F32) + bd_ref[0]
        for s in range(SUB):
            o_ref[pl.ds(s, tm, stride=SUB), :] = y[:, s * LANES:(s + 1) * LANES]

    @pl.when(t >= nt)
    def _():
        o_ref[...] = jnp.zeros(o_ref.shape, o_ref.dtype)


def _experts(h_rows, row_tok, tile_e, n_tiles_used, wgu, bgu, wd, bd, pmat):
    n_rows = row_tok.shape[0]
    tm = MOE_TM
    D, F2 = wgu.shape[1], wgu.shape[2]
    row_map = lambda t, te, nt, tok: (t, 0)
    w_map = lambda t, te, nt, tok: (te[t], 0, 0)
    return pl.pallas_call(
        functools.partial(_expert_kernel, tm=tm),
        grid_spec=pltpu.PrefetchScalarGridSpec(
            num_scalar_prefetch=3,
            grid=(n_rows // tm,),
            in_specs=[pl.BlockSpec(memory_space=pl.ANY),
                      pl.BlockSpec((1, D, F2), w_map), pl.BlockSpec((1, 1, F2), w_map),
                      pl.BlockSpec((1, F2 // 2, D), w_map), pl.BlockSpec((1, 1, D), w_map),
                      pl.BlockSpec(pmat.shape, lambda t, te, nt, tok: (0, 0))],
            out_specs=pl.BlockSpec((tm * SUB, LANES), row_map),
            scratch_shapes=[pltpu.VMEM((2 * tm * SUB, LANES), F32), pltpu.SemaphoreType.DMA((2,)),
                            pltpu.VMEM((D, F2), BF16)],
        ),
        out_shape=jax.ShapeDtypeStruct((n_rows * SUB, LANES), F32),
        compiler_params=_cparams(("arbitrary",)),
        name="moe_experts",
    )(tile_e, n_tiles_used, row_tok, h_rows, wgu, bgu, wd, bd, pmat)


def _combine_ln_kernel(dest_ref, y_ref, gate_ref, h_ref, g_ref, b_ref, out_ref, ybuf, sems, *, tm):
    i = pl.program_id(0)
    n = pl.num_programs(0)
    slot = i % 2
    per = TOP_K * tm

    @pl.when(i == 0)
    def _():
        _start_row_copies(dest_ref, 0, per, y_ref, ybuf, 0, sems.at[0])

    @pl.when(i + 1 < n)
    def _():
        _start_row_copies(dest_ref, (i + 1) * per, per, y_ref, ybuf, (1 - slot) * per, sems.at[1 - slot])

    _wait_row_copies(per, y_ref, ybuf, slot * per, sems.at[slot])
    gate = gate_ref[...]
    pieces = []
    for s in range(SUB):
        acc = None
        for kk in range(TOP_K):
            rows = ybuf[pl.ds((slot * per + kk) * SUB + s, tm, stride=TOP_K * SUB), :]
            term = gate[:, kk:kk + 1] * rows
            acc = term if acc is None else acc + term
        pieces.append(acc)
    y = jnp.concatenate(pieces, axis=1)
    out_ref[...] = _layer_norm(ALPHA_DN * h_ref[...] + y, g_ref[...], b_ref[...])


def _combine_ln(ys_rows, dest, gate, h, g, b):
    S, D = h.shape
    tm = min(COMBINE_TM, S)
    return pl.pallas_call(
        functools.partial(_combine_ln_kernel, tm=tm),
        grid_spec=pltpu.PrefetchScalarGridSpec(
            num_scalar_prefetch=1,
            grid=(S // tm,),
            in_specs=[pl.BlockSpec(memory_space=pl.ANY),
                      pl.BlockSpec((tm, LANES), lambda i, d: (i, 0)),
                      pl.BlockSpec((tm, D), lambda i, d: (i, 0)),
                      pl.BlockSpec((1, D), lambda i, d: (0, 0)),
                      pl.BlockSpec((1, D), lambda i, d: (0, 0))],
            out_specs=pl.BlockSpec((tm, D), lambda i, d: (i, 0)),
            scratch_shapes=[pltpu.VMEM((2 * TOP_K * tm * SUB, LANES), F32), pltpu.SemaphoreType.DMA((2,))],
        ),
        out_shape=jax.ShapeDtypeStruct((S, D), F32),
        compiler_params=_cparams(("arbitrary",)),
        name="moe_combine_ln",
    )(dest, ys_rows, gate, h, g.reshape(1, D), b.reshape(1, D))


def _moe_block(h, w_router, b_router, w_gu, b_gu, w_down, b_down, ln_g, ln_b):
    S, D = h.shape
    E, tm = N_EXPERTS, MOE_TM
    e_full, g_full = _router(h, w_router, b_router)
    top_e = e_full[:, :TOP_K]

    onehot = (top_e[:, :, None] == jnp.arange(E, dtype=I32)[None, None, :]).astype(I32).sum(axis=1)
    csum = jnp.cumsum(onehot, axis=0)
    counts = csum[-1]
    before = csum - onehot
    padded = (counts + tm - 1) // tm * tm
    pends = jnp.cumsum(padded)
    pstarts = pends - padded
    dest = pstarts[top_e] + jnp.take_along_axis(before, top_e, axis=1)
    A = S * TOP_K
    n_tiles = A // tm + E
    n_rows = n_tiles * tm
    tok = jnp.broadcast_to(jnp.arange(S, dtype=I32)[:, None], (S, TOP_K))
    row_tok = jnp.zeros((n_rows,), I32).at[dest.reshape(A)].set(tok.reshape(A))
    n_used = (pends[-1] // tm).astype(I32)
    tile_start = jnp.arange(n_tiles, dtype=I32) * tm
    tile_e = jnp.minimum(jnp.searchsorted(pends, tile_start, side="right"), E - 1).astype(I32)
    tile_e = jnp.where(jnp.arange(n_tiles) < n_used, tile_e, tile_e[jnp.maximum(n_used - 1, 0)])

    blk = 2 * LANES
    pm = np.zeros((blk, blk), np.float32)
    for qq in range(LANES):
        pm[2 * qq, qq] = 1.0
        pm[2 * qq + 1, LANES + qq] = 1.0
    F2 = w_gu.shape[2]
    bgu = b_gu.reshape(E, F2 // blk, LANES, 2).transpose(0, 1, 3, 2).reshape(E, 1, F2)
    h_rows = h.reshape(S * SUB, LANES)
    ys = _experts(h_rows, row_tok, tile_e, n_used.reshape(1), w_gu.astype(BF16), bgu, w_down.astype(BF16),
                  b_down[:, None, :], jnp.asarray(pm, BF16))
    return _combine_ln(ys, dest.reshape(A), g_full, h, ln_g, ln_b)


def _diff_attention(h, w_in, lq1, lk1, lq2, lk2, subln_g, w_out, ln_g, ln_b, tabs, layer_idx):
    dv = DA_HEADS * 2 * DA_HEAD_DIM
    scale = DA_HEAD_DIM ** -0.5 * LOG2E
    segs = (Seg(0, dv, "rope", scale, BF16), Seg(dv, dv, "rope", 1.0, BF16), Seg(2 * dv, dv, "plain", 1.0, BF16))
    q, k, v = _project(h, w_in.astype(BF16), tabs["c64"], tabs["s64"], segs, DA_HEAD_DIM // 2, "da_project")
    lam_init = 0.8 - 0.6 * math.exp(-0.3 * layer_idx)
    lam4 = jnp.stack([lq1, lk1, lq2, lk2]).astype(F32)
    o = _flash("da", q, k, v.T, (lam4, subln_g.reshape(1, -1).astype(F32)), lam_init, "da_attention")
    return _out_ln(o, w_out.astype(BF16), h, ln_g, ln_b, "da_out_ln")


def _mla(h, w_in, q_norm_g, kv_norm_g, w_uq, w_ukv, w_out, ln_g, ln_b, tabs):
    H = MLA_HEADS
    D = h.shape[1]
    qk_dim = MLA_NOPE + MLA_ROPE
    pad_rope = ((0, 0), (MLA_NOPE, LANES - MLA_NOPE - MLA_ROPE))
    w_in_p = jnp.concatenate([w_in[:, :MLA_Q_RANK + MLA_KV_RANK],
                              jnp.pad(w_in[:, MLA_Q_RANK + MLA_KV_RANK:], pad_rope)], axis=1).astype(BF16)
    w_uq_p = jnp.pad(w_uq.reshape(MLA_Q_RANK, H, qk_dim), ((0, 0), (0, 0), (0, LANES - qk_dim)))
    w_uq_p = w_uq_p.reshape(MLA_Q_RANK, H * LANES).astype(BF16)
    w_ukv3 = w_ukv.reshape(MLA_KV_RANK, H, MLA_NOPE + MLA_V)
    w_uk_p = jnp.pad(w_ukv3[:, :, :MLA_NOPE], ((0, 0), (0, 0), (0, LANES - MLA_NOPE)))
    w_uk_p = w_uk_p.reshape(MLA_KV_RANK, H * LANES).astype(BF16)
    w_uv = w_ukv3[:, :, MLA_NOPE:].reshape(MLA_KV_RANK, H * MLA_V).astype(BF16)
    q, k, v = _mla_project(h, w_in_p, w_uq_p, w_uk_p, w_uv, q_norm_g.reshape(1, -1), kv_norm_g.reshape(1, -1),
                           tabs["c32"], tabs["s32"], qk_dim ** -0.5 * LOG2E)
    o = _flash("mla", q, k, v.T, (), 0.0, "mla_attention")
    return _out_ln(o, w_out.astype(BF16), h, ln_g, ln_b, "mla_out_ln")


def _nsa(h, w_in, pos_k, pos_v, ck_w1, ck_w2, cv_w1, cv_w2, w_out, ln_g, ln_b, tabs):
    S, D = h.shape
    H, G, d = NSA_HEADS, NSA_GROUPS, NSA_HEAD_DIM
    R = H // G
    st, Ls = NSA_CMP_STRIDE, NSA_SLC_LEN
    n_cmp = (S - NSA_CMP_LEN) // st + 1
    nc = S // st
    n_slc = S // Ls
    n_top = min(NSA_SLC_TOPK, n_slc)
    nslp = -(-n_slc // LANES) * LANES
    gd = G * d

    def pad_heads(w, n):
        return jnp.pad(w.reshape(D, n, d), ((0, 0), (0, 0), (0, LANES - d))).reshape(D, n * LANES)

    off = np.cumsum([0, H * d] + [gd] * 6)
    wq, wkc, wvc, wks, wvs, wkw, wvw = [w_in[:, off[t]:off[t + 1]] for t in range(7)]
    wgl = w_in[:, off[7]:].reshape(D, H, 3)
    w_a = jnp.concatenate([pad_heads(wq, H), pad_heads(wks, G), pad_heads(wkw, G), pad_heads(wvs, G),
                           pad_heads(wvw, G), wkc, wvc], axis=1).astype(BF16)
    hq, hg = H * LANES, G * LANES
    segs_a = (Seg(0, hq, "rope", d ** -0.5 * LOG2E, BF16), Seg(hq, hg, "rope", 1.0, BF16), Seg(hq + hg, hg, "rope", 1.0, BF16),
              Seg(hq + 2 * hg, hg, "plain", 1.0, BF16), Seg(hq + 3 * hg, hg, "plain", 1.0, BF16),
              Seg(hq + 4 * hg, gd, "plain", 1.0, F32), Seg(hq + 4 * hg + gd, gd, "plain", 1.0, F32))
    q, ks, kw, vs, vw, kc, vc = _project(h, w_a, tabs["c64p"], tabs["s64p"], segs_a, d // 2, "nsa_project")
    w_g = jnp.concatenate([jnp.repeat(wgl[:, :, b], d, axis=1) for b in range(3)], axis=1).astype(BF16)
    segs_g = tuple(Seg(b * H * d, H * d, "sigmoid", 1.0, F32) for b in range(3))
    gc, gs, gw = _project(h, w_g, tabs["c64p"], tabs["s64p"], segs_g, d // 2, "nsa_gates")

    def chunks(t):
        return t.reshape(nc, st, G, d).transpose(2, 0, 1, 3).reshape(G, nc, st * d)

    k_cmp = _compress(chunks(kc), pos_k, ck_w1, ck_w2, tabs["ccmp"], tabs["scmp"], True, "nsa_compress_k")
    v_cmp = _compress(chunks(vc), pos_v, cv_w1, cv_w2, tabs["ccmp"], tabs["scmp"], False, "nsa_compress_v")

    ratio = Ls // st
    amat = np.zeros((nc, nslp), np.float32)
    for jj in range(n_slc):
        for mm in range(ratio):
            for nn in range(NSA_CMP_LEN // st):
                c = ratio * jj + mm - nn
                if 0 <= c < n_cmp:
                    amat[c, jj] += 1.0
    ocw, sel = _nsa_local(q, k_cmp, v_cmp, kw, vw, gc, gw, jnp.asarray(amat, BF16), n_slc, n_top)

    tk = min(NSA_TK, S)
    per_tile = tk // Ls
    tiles_per_selblock = LANES // per_tile
    emat = np.zeros((tiles_per_selblock, tk, LANES), np.float32)
    for u in range(tiles_per_selblock):
        for t in range(tk):
            emat[u, t, per_tile * u + t // Ls] = 1.0
    o = _nsa_select(q, ks, vs.T, sel, jnp.asarray(emat, BF16), ocw, gs)
    return _out_ln(o, w_out.astype(BF16), h, ln_g, ln_b, "nsa_out_ln")


def _rope_tables(S):
    pos = jnp.arange(S)
    c64, s64 = _rope_angles(pos, DA_HEAD_DIM)
    c32, s32 = _rope_angles(pos, MLA_ROPE)
    nc = S // NSA_CMP_STRIDE
    cc, sc = _rope_angles(jnp.arange(nc) * NSA_CMP_STRIDE + NSA_CMP_LEN - 1, NSA_HEAD_DIM)
    one = lambda n, w: jnp.ones((n, w), F32)
    zero = lambda n, w: jnp.zeros((n, w), F32)
    cat = lambda *a: jnp.concatenate(a, axis=1)
    return {
        "c64": cat(c64, c64, c64, c64), "s64": cat(-s64, s64, -s64, s64),
        "c64p": cat(c64, c64, one(S, 64)), "s64p": cat(-s64, s64, zero(S, 64)),
        "c32": cat(one(S, 64), c32, c32, one(S, 32)), "s32": cat(zero(S, 64), -s32, s32, zero(S, 32)),
        "ccmp": cat(cc, cc, one(nc, 64)), "scmp": cat(-sc, sc, zero(nc, 64)),
    }


def kernel(x, da_w_in, da_lambda_q1, da_lambda_k1, da_lambda_q2, da_lambda_k2, da_subln, da_w_out, mla_w_in, mla_q_norm, mla_kv_norm, mla_w_uq, mla_w_ukv, mla_w_out, nsa_w_in, nsa_cmp_pos_k, nsa_cmp_pos_v, nsa_cmp_k_w1, nsa_cmp_k_w2, nsa_cmp_v_w1, nsa_cmp_v_w2, nsa_w_out, ln1_g, ln1_b, ln2_g, ln2_b, moe_w_router, moe_b_router, moe_w_gu, moe_b_gu, moe_w_down, moe_b_down):
    B, S, D = x.shape
    assert B == 1 and D == D_MODEL
    tabs = _rope_tables(S)
    h = x.reshape(S, D)
    for i in range(DEPTH):
        m, j = i % N_MIXERS, i // N_MIXERS
        if m == 0:
            h = _diff_attention(h, da_w_in[j], da_lambda_q1[j], da_lambda_k1[j], da_lambda_q2[j], da_lambda_k2[j],
                                da_subln[j], da_w_out[j], ln1_g[i], ln1_b[i], tabs, i)
        elif m == 1:
            h = _mla(h, mla_w_in[j], mla_q_norm[j], mla_kv_norm[j], mla_w_uq[j], mla_w_ukv[j], mla_w_out[j],
                     ln1_g[i], ln1_b[i], tabs)
        else:
            h = _nsa(h, nsa_w_in[j], nsa_cmp_pos_k[j], nsa_cmp_pos_v[j], nsa_cmp_k_w1[j], nsa_cmp_k_w2[j],
                     nsa_cmp_v_w1[j], nsa_cmp_v_w2[j], nsa_w_out[j], ln1_g[i], ln1_b[i], tabs)
        h = _moe_block(h, moe_w_router[i], moe_b_router[i], moe_w_gu[i], moe_b_gu[i], moe_w_down[i], moe_b_down[i],
                       ln2_g[i], ln2_b[i])
    return h.reshape(B, S, D)
```

```python
import collections
import functools
import math

import numpy as np
import jax
import jax.numpy as jnp
from jax import lax
from jax.experimental import pallas as pl
from jax.experimental.pallas import tpu as pltpu

F32 = jnp.float32
BF16 = jnp.bfloat16
I32 = jnp.int32

D_MODEL = 1024
DEPTH = 4
N_MIXERS = 3
ROPE_THETA = 10000.0
LN_EPS = 1e-5
RMS_EPS = 1e-6
NEG = -1e30
ALPHA_DN = (2 * DEPTH) ** 0.25
LOG2E = math.log2(math.e)

DA_HEADS = 8
DA_HEAD_DIM = 64
MLA_HEADS = 16
MLA_NOPE = 64
MLA_ROPE = 32
MLA_V = 64
MLA_Q_RANK = 256
MLA_KV_RANK = 128
NSA_HEADS = 16
NSA_GROUPS = 4
NSA_HEAD_DIM = 64
NSA_CMP_LEN = 32
NSA_CMP_STRIDE = 16
NSA_CMP_HIDDEN = 256
NSA_SLC_LEN = 64
NSA_SLC_TOPK = 16
NSA_WINDOW = 512
NSA_FORCE = 1e9
N_EXPERTS = 32
TOP_K = 4
D_EXPERT = 1024
SWIGLU_LIMIT = 7.0
SWIGLU_ALPHA = 1.702

LANES = 128
SUB = 8
VMEM_LIMIT = 48 * 1024 * 1024

PROJ_TM = 256
ATT_TQ = 512
ATT_TK = 512
KV_UNROLL = 2
NSA_TQ = 128
NSA_TK = 512
NSA_KV_UNROLL = 4
MOE_TM = 512
LN_TM = 512
COMBINE_TM = 256


def _cparams(sem):
    return pltpu.CompilerParams(dimension_semantics=sem, vmem_limit_bytes=VMEM_LIMIT)


def _rope_angles(pos, dim):
    inv = ROPE_THETA ** (-jnp.arange(0, dim, 2, dtype=F32) / dim)
    ang = pos.astype(F32)[:, None] * inv[None, :]
    return jnp.cos(ang), jnp.sin(ang)


def _rope128(t, c, sg, half):
    lane = lax.broadcasted_iota(I32, t.shape, 1)
    first = (lane % (2 * half)) < half
    partner = jnp.where(first, pltpu.roll(t, LANES - half, 1), pltpu.roll(t, half, 1))
    return t * c + partner * sg


Seg = collections.namedtuple("Seg", "start width mode scale dtype")


def _proj_kernel(*refs, segs, half):
    x_ref, w_ref, c_ref, s_ref = refs[:4]
    out_refs = refs[4:]
    xb = x_ref[...].astype(BF16)
    for seg, o_ref in zip(segs, out_refs):
        acc = jnp.dot(xb, w_ref[:, seg.start:seg.start + seg.width], preferred_element_type=F32)
        if seg.mode == "rope":
            c = c_ref[...]
            sg = s_ref[...]
            for g in range(seg.width // LANES):
                t = _rope128(acc[:, g * LANES:(g + 1) * LANES], c, sg, half)
                o_ref[:, g * LANES:(g + 1) * LANES] = (t * seg.scale).astype(o_ref.dtype)
        elif seg.mode == "sigmoid":
            o_ref[...] = jax.nn.sigmoid(acc).astype(o_ref.dtype)
        else:
            o_ref[...] = (acc * seg.scale).astype(o_ref.dtype) if seg.scale != 1.0 else acc.astype(o_ref.dtype)


def _project(x, w, tab_c, tab_s, segs, half, name):
    S, K = x.shape
    tm = min(PROJ_TM, S)
    return pl.pallas_call(
        functools.partial(_proj_kernel, segs=segs, half=half),
        grid=(S // tm,),
        in_specs=[
            pl.BlockSpec((tm, K), lambda i: (i, 0)),
            pl.BlockSpec(w.shape, lambda i: (0, 0)),
            pl.BlockSpec((tm, LANES), lambda i: (i, 0)),
            pl.BlockSpec((tm, LANES), lambda i: (i, 0)),
        ],
        out_specs=[pl.BlockSpec((tm, s.width), lambda i: (i, 0)) for s in segs],
        out_shape=[jax.ShapeDtypeStruct((S, s.width), s.dtype) for s in segs],
        compiler_params=_cparams(("parallel",)),
        name=name,
    )(x, w, tab_c, tab_s)


def _qk(q, k):
    return lax.dot_general(q, k, (((1,), (1,)), ((), ())), preferred_element_type=F32)


ONES_ROWS = 16


def _flash_update(s_t, vt_ext, m_ref, acc_ref, st):
    m_old = m_ref[st]
    m_new = jnp.maximum(m_old, jnp.max(s_t, axis=0, keepdims=True))
    alpha = jnp.exp2(m_old - m_new)
    p_t = jnp.exp2(s_t - m_new).astype(BF16)
    acc_ref[st] = alpha * acc_ref[st] + jnp.dot(vt_ext, p_t, preferred_element_type=F32)
    m_ref[st] = m_new


def _flash_kernel(*refs, variant, tq, tk, lam_init):
    if variant == "da":
        lam_ref, g_ref, q_ref, k_ref, vt_ref, o_ref, qa, qb, m_ref, acc_ref = refs
    else:
        q_ref, k_ref, vt_ref, o_ref, m_ref, acc_ref = refs
    i = pl.program_id(1)
    dv = vt_ref.shape[0]

    if variant == "da":
        q = q_ref[...]
        lane = lax.broadcasted_iota(I32, q.shape, 1)
        qa[...] = jnp.where(lane < DA_HEAD_DIM, q, jnp.zeros_like(q))
        qb[...] = jnp.where(lane >= DA_HEAD_DIM, q, jnp.zeros_like(q))
    m_ref[...] = jnp.full(m_ref.shape, NEG, F32)
    acc_ref[...] = jnp.zeros(acc_ref.shape, F32)
    ones = jnp.ones((ONES_ROWS, tk), BF16)

    def tiles(js, masked):
        work = []
        for j in js:
            start = pl.multiple_of(j * tk, tk)
            k = k_ref[pl.ds(start, tk), :]
            if variant == "da":
                ops = ((k, qa[...]), (k, qb[...]))
            else:
                ops = ((k[:, :LANES], q_ref[:, :LANES]), (k[:, LANES:], q_ref[:, LANES:]))
            work.append((j, start, [_qk(kk, qq) for kk, qq in ops]))
        for j, start, scores in work:
            vt_ext = jnp.concatenate([vt_ref[:, pl.ds(start, tk)], ones], axis=0)
            for st, s_t in enumerate(scores):
                if masked:
                    kpos = j * tk + lax.broadcasted_iota(I32, s_t.shape, 0)
                    qpos = i * tq + lax.broadcasted_iota(I32, s_t.shape, 1)
                    s_t = jnp.where(kpos <= qpos, s_t, NEG)
                _flash_update(s_t, vt_ext, m_ref, acc_ref, st)

    n_full = (i * tq + 1) // tk

    def body(p, carry):
        tiles([p * KV_UNROLL + u for u in range(KV_UNROLL)], False)
        return carry

    lax.fori_loop(0, n_full // KV_UNROLL, body, 0)
    for u in range(KV_UNROLL - 1):
        @pl.when(u < n_full % KV_UNROLL)
        def _():
            tiles([n_full - n_full % KV_UNROLL + u], False)
    tiles([n_full + d for d in range(max(1, tq // tk))], True)

    outs = []
    for st in range(2):
        acc = acc_ref[st]
        outs.append((acc[:dv] / acc[dv:dv + 1]).T)
    o1, o2 = outs
    if variant == "da":
        lam4 = lam_ref[...]
        lam = (jnp.exp(jnp.sum(lam4[0:1] * lam4[1:2], axis=1, keepdims=True))
               - jnp.exp(jnp.sum(lam4[2:3] * lam4[3:4], axis=1, keepdims=True)) + lam_init)
        o = o1 - lam * o2
        ms = jnp.mean(o * o, axis=-1, keepdims=True)
        o = o * lax.rsqrt(ms + LN_EPS) * g_ref[...] * (1.0 - lam_init)
    else:
        lane = lax.broadcasted_iota(I32, o1.shape, 1)
        o = jnp.where(lane < MLA_V, o1, o2)
    o_ref[...] = o.astype(o_ref.dtype)


def _flash(variant, q, k, vt, extra, lam_init, name):
    S = q.shape[0]
    tq, tk = min(ATT_TQ, S), min(ATT_TK, S)
    assert tk % tq == 0 or tq % tk == 0
    qw = LANES if variant == "da" else 2 * LANES
    n_groups = q.shape[1] // qw
    in_specs = [
        pl.BlockSpec((tq, qw), lambda h, i: (i, h)),
        pl.BlockSpec((S, qw), lambda h, i: (0, h)),
        pl.BlockSpec((LANES, S), lambda h, i: (h, 0)),
    ]
    scratch = [pltpu.VMEM((2, 1, tq), F32), pltpu.VMEM((2, LANES + ONES_ROWS, tq), F32)]
    if variant == "da":
        in_specs = [pl.BlockSpec(extra[0].shape, lambda h, i: (0, 0)),
                    pl.BlockSpec(extra[1].shape, lambda h, i: (0, 0))] + in_specs
        scratch = [pltpu.VMEM((tq, LANES), BF16)] * 2 + scratch
    return pl.pallas_call(
        functools.partial(_flash_kernel, variant=variant, tq=tq, tk=tk, lam_init=lam_init),
        grid=(n_groups, S // tq),
        in_specs=in_specs,
        out_specs=pl.BlockSpec((tq, LANES), lambda h, i: (i, h)),
        out_shape=jax.ShapeDtypeStruct((S, n_groups * LANES), BF16),
        scratch_shapes=scratch,
        compiler_params=_cparams(("parallel", "parallel")),
        name=name,
    )(*extra, q, k, vt)


def _layer_norm(x, g, b):
    mu = jnp.mean(x, axis=-1, keepdims=True)
    xc = x - mu
    var = jnp.mean(xc * xc, axis=-1, keepdims=True)
    return xc * lax.rsqrt(var + LN_EPS) * g + b


def _out_ln_kernel(o_ref, w_ref, h_ref, g_ref, b_ref, out_ref):
    y = jnp.dot(o_ref[...], w_ref[...], preferred_element_type=F32)
    out_ref[...] = _layer_norm(ALPHA_DN * h_ref[...] + y, g_ref[...], b_ref[...])


def _out_ln(o, w, h, g, b, name):
    S, K = o.shape
    D = h.shape[1]
    tm = min(LN_TM, S)
    return pl.pallas_call(
        _out_ln_kernel,
        grid=(S // tm,),
        in_specs=[
            pl.BlockSpec((tm, K), lambda i: (i, 0)),
            pl.BlockSpec((K, D), lambda i: (0, 0)),
            pl.BlockSpec((tm, D), lambda i: (i, 0)),
            pl.BlockSpec((1, D), lambda i: (0, 0)),
            pl.BlockSpec((1, D), lambda i: (0, 0)),
        ],
        out_specs=pl.BlockSpec((tm, D), lambda i: (i, 0)),
        out_shape=jax.ShapeDtypeStruct((S, D), F32),
        compiler_params=_cparams(("parallel",)),
        name=name,
    )(o, w, h, g.reshape(1, D), b.reshape(1, D))


def _rms(x, g, eps):
    return x * lax.rsqrt(jnp.mean(x * x, axis=-1, keepdims=True) + eps) * g


def _mla_proj_kernel(x_ref, win_ref, wuq_ref, wuk_ref, wuv_ref, gq_ref, gkv_ref, c_ref, s_ref,
                     q_ref, k_ref, v_ref, *, scale):
    xb = x_ref[...].astype(BF16)
    p1 = jnp.dot(xb, win_ref[...], preferred_element_type=F32)
    c_q = _rms(p1[:, :MLA_Q_RANK], gq_ref[...], RMS_EPS).astype(BF16)
    c_kv = _rms(p1[:, MLA_Q_RANK:MLA_Q_RANK + MLA_KV_RANK], gkv_ref[...], RMS_EPS).astype(BF16)
    c = c_ref[...]
    sg = s_ref[...]
    half = MLA_ROPE // 2
    k_rope = _rope128(p1[:, MLA_Q_RANK + MLA_KV_RANK:], c, sg, half)
    q = jnp.dot(c_q, wuq_ref[...], preferred_element_type=F32)
    k = jnp.dot(c_kv, wuk_ref[...], preferred_element_type=F32)
    for h in range(MLA_HEADS):
        sl = slice(h * LANES, (h + 1) * LANES)
        q_ref[:, sl] = (_rope128(q[:, sl], c, sg, half) * scale).astype(q_ref.dtype)
        k_ref[:, sl] = (k[:, sl] + k_rope).astype(k_ref.dtype)
    v_ref[...] = jnp.dot(c_kv, wuv_ref[...], preferred_element_type=F32).astype(v_ref.dtype)


def _mla_project(x, w_in, w_uq, w_uk, w_uv, gq, gkv, tab_c, tab_s, scale):
    S, D = x.shape
    tm = min(PROJ_TM, S)
    full = lambda a: pl.BlockSpec(a.shape, lambda i: (0,) * a.ndim)
    row = lambda w: pl.BlockSpec((tm, w), lambda i: (i, 0))
    hw = MLA_HEADS * LANES
    return pl.pallas_call(
        functools.partial(_mla_proj_kernel, scale=scale),
        grid=(S // tm,),
        in_specs=[row(D), full(w_in), full(w_uq), full(w_uk), full(w_uv), full(gq), full(gkv), row(LANES), row(LANES)],
        out_specs=[row(hw), row(hw), row(MLA_HEADS * MLA_V)],
        out_shape=[jax.ShapeDtypeStruct((S, hw), BF16), jax.ShapeDtypeStruct((S, hw), BF16),
                   jax.ShapeDtypeStruct((S, MLA_HEADS * MLA_V), BF16)],
        compiler_params=_cparams(("parallel",)),
        name="mla_project",
    )(x, w_in, w_uq, w_uk, w_uv, gq, gkv, tab_c, tab_s)


def _gelu_tanh(x):
    return 0.5 * x * (1.0 + jnp.tanh(math.sqrt(2.0 / math.pi) * (x + 0.044715 * (x * x * x))))


def _compress_kernel(x_ref, pa_ref, pb_ref, w1a_ref, w1b_ref, w2_ref, c_ref, s_ref, o_ref, *, nc, rope):
    x = x_ref[0]
    ha = jnp.dot((x + pa_ref[...]).astype(BF16), w1a_ref[...], preferred_element_type=F32)
    hb = jnp.dot((x + pb_ref[...]).astype(BF16), w1b_ref[...], preferred_element_type=F32)
    hid = ha + pltpu.roll(hb, nc - 1, 0)
    out = jnp.dot(_gelu_tanh(hid).astype(BF16), w2_ref[...], preferred_element_type=F32)
    if rope:
        out = _rope128(out, c_ref[...], s_ref[...], NSA_HEAD_DIM // 2)
    o_ref[0] = out.astype(o_ref.dtype)


def _compress(xc, pos, w1, w2, tab_c, tab_s, rope, name):
    G, nc, cw = xc.shape
    half = NSA_CMP_LEN // 2
    pa = pos[:half].reshape(1, cw)
    pb = pos[half:].reshape(1, cw)
    w1a = w1[:cw].astype(BF16)
    w1b = w1[cw:].astype(BF16)
    w2p = jnp.pad(w2, ((0, 0), (0, LANES - w2.shape[1]))).astype(BF16)
    full = lambda a: pl.BlockSpec(a.shape, lambda g: (0,) * a.ndim)
    return pl.pallas_call(
        functools.partial(_compress_kernel, nc=nc, rope=rope),
        grid=(G,),
        in_specs=[pl.BlockSpec((1, nc, cw), lambda g: (g, 0, 0)), full(pa), full(pb), full(w1a), full(w1b),
                  full(w2p), full(tab_c), full(tab_s)],
        out_specs=pl.BlockSpec((1, nc, LANES), lambda g: (g, 0, 0)),
        out_shape=jax.ShapeDtypeStruct((G, nc, LANES), BF16),
        compiler_params=_cparams(("parallel",)),
        name=name,
    )(xc, pa, pb, w1a, w1b, w2p, tab_c, tab_s)


def _split3(x):
    hi = x.astype(BF16)
    r = x - hi.astype(F32)
    mid = r.astype(BF16)
    lo = (r - mid.astype(F32)).astype(BF16)
    return hi, mid, lo


def _heads_to_rows(o_t, tq):
    R = NSA_HEADS // NSA_GROUPS
    return jnp.concatenate([o_t[:NSA_HEAD_DIM, r * tq:(r + 1) * tq] for r in range(R)], axis=0).T


def _nsa_local_kernel(q_ref, kc_ref, vct_ref, kw0, kw1, kw2, kw3, kw4, vw0, vw1, vw2, vw3, vw4,
                      gc_ref, gw_ref, amat_ref, o_ref, sel_ref, *, tq, nc, n_slc, n_top):
    i = pl.program_id(1)
    R = NSA_HEADS // NSA_GROUPS
    q = jnp.concatenate([q_ref[:, r * LANES:(r + 1) * LANES] for r in range(R)], axis=0)
    cols = R * tq
    qpos = i * tq + lax.broadcasted_iota(I32, (1, cols), 1) % tq

    s_t = _qk(kc_ref[0], q)
    cmp_end = lax.broadcasted_iota(I32, (nc, cols), 0) * NSA_CMP_STRIDE + (NSA_CMP_LEN - 1)
    s_t = jnp.where(cmp_end <= qpos, s_t, NEG)
    e = jnp.exp2(s_t - jnp.max(s_t, axis=0, keepdims=True))
    den = jnp.sum(e, axis=0, keepdims=True)
    p_t = e * jnp.where(qpos >= NSA_CMP_LEN - 1, 1.0 / den, 0.0)
    o_ct = jnp.dot(vct_ref[0], p_t.astype(BF16), preferred_element_type=F32)

    p_g = p_t[:, 0:tq]
    for r in range(1, R):
        p_g = p_g + p_t[:, r * tq:(r + 1) * tq]
    amat = amat_ref[...]
    imp = None
    for part in _split3(p_g):
        t = jnp.dot(amat, part, preferred_element_type=F32)
        imp = t if imp is None else imp + t
    nslp = imp.shape[0]
    jb = lax.broadcasted_iota(I32, (nslp, tq), 0)
    cur = (i * tq + lax.broadcasted_iota(I32, (1, tq), 1)) // NSA_SLC_LEN
    valid = jb <= cur
    forced = valid & ((jb == 0) | (jb >= cur - 1))
    score = jnp.where(forced, NSA_FORCE, jnp.where(valid, imp, -1.0))
    score = jnp.where(jb < n_slc, score, -2.0)
    sel = jnp.zeros((nslp, tq), F32)
    for _ in range(n_top):
        mx = jnp.max(score, axis=0, keepdims=True)
        idx = jnp.min(jnp.where(score == mx, jb, nslp), axis=0, keepdims=True)
        hit = jb == idx
        sel = jnp.where(hit, 1.0, sel)
        score = jnp.where(hit, -jnp.inf, score)
    sel_ref[0] = sel.T.astype(sel_ref.dtype)

    nwin = NSA_WINDOW // tq + 1
    k_w = jnp.concatenate([r[...] for r in (kw0, kw1, kw2, kw3, kw4)], axis=0)
    vt_w = jnp.concatenate([r[...] for r in (vw0, vw1, vw2, vw3, vw4)], axis=1)
    s_t = _qk(k_w, q)
    kpos = (i - (nwin - 1)) * tq + lax.broadcasted_iota(I32, (nwin * tq, cols), 0)
    m_w = (kpos <= qpos) & (kpos > qpos - NSA_WINDOW) & (kpos >= 0)
    s_t = jnp.where(m_w, s_t, NEG)
    e = jnp.exp2(s_t - jnp.max(s_t, axis=0, keepdims=True))
    p_w = e * (1.0 / jnp.sum(e, axis=0, keepdims=True))
    o_wt = jnp.dot(vt_w, p_w.astype(BF16), preferred_element_type=F32)

    o_ref[...] = gc_ref[...] * _heads_to_rows(o_ct, tq) + gw_ref[...] * _heads_to_rows(o_wt, tq)


def _nsa_local(q, k_cmp, vt_cmp, kw, vwt, gc, gw, amat_t, n_slc, n_top):
    S = q.shape[0]
    tq = NSA_TQ
    G = NSA_GROUPS
    nc = k_cmp.shape[1]
    nslp = amat_t.shape[0]
    gw_lanes = (NSA_HEADS // G) * NSA_HEAD_DIM
    nwin = NSA_WINDOW // tq + 1
    assert nwin == 5

    def kwin(t):
        return pl.BlockSpec((tq, LANES), lambda g, i: (jnp.maximum(i - (nwin - 1) + t, 0), g))

    def vwin(t):
        return pl.BlockSpec((LANES, tq), lambda g, i: (g, jnp.maximum(i - (nwin - 1) + t, 0)))

    in_specs = ([pl.BlockSpec((tq, (NSA_HEADS // G) * LANES), lambda g, i: (i, g)),
                 pl.BlockSpec((1, nc, LANES), lambda g, i: (g, 0, 0)),
                 pl.BlockSpec((1, LANES, nc), lambda g, i: (g, 0, 0))]
                + [kwin(t) for t in range(nwin)] + [vwin(t) for t in range(nwin)]
                + [pl.BlockSpec((tq, gw_lanes), lambda g, i: (i, g)),
                   pl.BlockSpec((tq, gw_lanes), lambda g, i: (i, g)),
                   pl.BlockSpec(amat_t.shape, lambda g, i: (0, 0))])
    return pl.pallas_call(
        functools.partial(_nsa_local_kernel, tq=tq, nc=nc, n_slc=n_slc, n_top=n_top),
        grid=(G, S // tq),
        in_specs=in_specs,
        out_specs=[pl.BlockSpec((tq, gw_lanes), lambda g, i: (i, g)),
                   pl.BlockSpec((1, tq, nslp), lambda g, i: (g, i, 0))],
        out_shape=[jax.ShapeDtypeStruct((S, NSA_HEADS * NSA_HEAD_DIM), F32),
                   jax.ShapeDtypeStruct((G, S, nslp), BF16)],
        compiler_params=_cparams(("parallel", "parallel")),
        name="nsa_local",
    )(q, k_cmp, vt_cmp, *([kw] * nwin), *([vwt] * nwin), gc, gw, amat_t)


MASK_BIG = 2.0 ** 100


def _nsa_select_kernel(q_ref, k_ref, vt_ref, sel_ref, kpat_ref, pm_ref, ocw_ref, gs_ref, o_ref, m_ref, acc_ref,
                       *, tq, tk, tiles_per_selblock):
    i = pl.program_id(1)
    R = NSA_HEADS // NSA_GROUPS
    dv = vt_ref.shape[0]
    m_ref[...] = jnp.full(m_ref.shape, NEG, F32)
    acc_ref[...] = jnp.zeros(acc_ref.shape, F32)
    ones = jnp.ones((ONES_ROWS, tk), BF16)

    def tiles(js, masked):
        exts = []
        for j in js:
            sel_blk = sel_ref[0, :, pl.ds(pl.multiple_of((j // tiles_per_selblock) * LANES, LANES), LANES)]
            exts.append(jnp.dot(sel_blk - 1.0, pm_ref[j % tiles_per_selblock], preferred_element_type=F32).astype(BF16))
        work = []
        for j, ext in zip(js, exts):
            start = pl.multiple_of(j * tk, tk)
            k_ext = k_ref[pl.ds(start, tk), :] + kpat_ref[...]
            q_ext = jnp.concatenate([q_ref[:, r * LANES:(r + 1) * LANES] + ext for r in range(R)], axis=0)
            work.append((j, start, _qk(k_ext, q_ext)))
        for j, start, s_t in work:
            vt_ext = jnp.concatenate([vt_ref[:, pl.ds(start, tk)], ones], axis=0)
            if masked:
                kpos = j * tk + lax.broadcasted_iota(I32, s_t.shape, 0)
                qpos = i * tq + lax.broadcasted_iota(I32, s_t.shape, 1) % tq
                s_t = jnp.where(kpos <= qpos, s_t, NEG)
            _flash_update(s_t, vt_ext, m_ref, acc_ref, 0)

    n_full = (i * tq + 1) // tk

    def body(p, carry):
        tiles([p * NSA_KV_UNROLL + u for u in range(NSA_KV_UNROLL)], False)
        return carry

    lax.fori_loop(0, n_full // NSA_KV_UNROLL, body, 0)
    for u in range(NSA_KV_UNROLL - 1):
        @pl.when(u < n_full % NSA_KV_UNROLL)
        def _():
            tiles([n_full - n_full % NSA_KV_UNROLL + u], False)
    tiles([n_full], True)

    acc = acc_ref[0]
    o_t = acc[:dv] / acc[dv:dv + 1]
    o_ref[...] = (ocw_ref[...] + gs_ref[...] * _heads_to_rows(o_t, tq)).astype(o_ref.dtype)


def _nsa_select(q, ks, vst, sel, kpat, pm, ocw, gs):
    S = q.shape[0]
    tq, tk = NSA_TQ, kpat.shape[0]
    G = NSA_GROUPS
    R = NSA_HEADS // G
    gw_lanes = R * NSA_HEAD_DIM
    nslp = sel.shape[2]
    tiles_per_selblock = pm.shape[0]
    return pl.pallas_call(
        functools.partial(_nsa_select_kernel, tq=tq, tk=tk, tiles_per_selblock=tiles_per_selblock),
        grid=(G, S // tq),
        in_specs=[pl.BlockSpec((tq, R * LANES), lambda g, i: (i, g)),
                  pl.BlockSpec((S, LANES), lambda g, i: (0, g)),
                  pl.BlockSpec((LANES, S), lambda g, i: (g, 0)),
                  pl.BlockSpec((1, tq, nslp), lambda g, i: (g, i, 0)),
                  pl.BlockSpec(kpat.shape, lambda g, i: (0, 0)),
                  pl.BlockSpec(pm.shape, lambda g, i: (0, 0, 0)),
                  pl.BlockSpec((tq, gw_lanes), lambda g, i: (i, g)),
                  pl.BlockSpec((tq, gw_lanes), lambda g, i: (i, g))],
        out_specs=pl.BlockSpec((tq, gw_lanes), lambda g, i: (i, g)),
        out_shape=jax.ShapeDtypeStruct((S, NSA_HEADS * NSA_HEAD_DIM), BF16),
        scratch_shapes=[pltpu.VMEM((1, 1, R * tq), F32), pltpu.VMEM((1, LANES + ONES_ROWS, R * tq), F32)],
        compiler_params=_cparams(("parallel", "parallel")),
        name="nsa_select",
    )(q, ks, vst, sel, kpat, pm, ocw, gs)


def _router_kernel(h_ref, w_ref, b_ref, e_ref, g_ref, rank_ref, cnt_ref, carry):
    @pl.when(pl.program_id(0) == 0)
    def _():
        carry[...] = jnp.zeros(carry.shape, F32)

    x = h_ref[...]
    w = w_ref[...]
    x_hi = x.astype(BF16)
    x_lo = (x - x_hi.astype(F32)).astype(BF16)
    w_hi = w.astype(BF16)
    w_lo = (w - w_hi.astype(F32)).astype(BF16)
    logits = (jnp.dot(x_hi, w_hi, preferred_element_type=F32) + jnp.dot(x_hi, w_lo, preferred_element_type=F32)
              + jnp.dot(x_lo, w_hi, preferred_element_type=F32)) + b_ref[...]
    tm = logits.shape[0]
    lane = lax.broadcasted_iota(I32, logits.shape, 1)
    work = jnp.where(lane < N_EXPERTS, logits, -jnp.inf)
    vals, idxs = [], []
    for _ in range(TOP_K):
        mx = jnp.max(work, axis=1, keepdims=True)
        idx = jnp.min(jnp.where(work == mx, lane, LANES), axis=1, keepdims=True)
        vals.append(mx)
        idxs.append(idx)
        work = jnp.where(lane == idx, -jnp.inf, work)
    exps = [jnp.exp(v - vals[0]) for v in vals]
    den = exps[0]
    for t in exps[1:]:
        den = den + t

    chosen = jnp.zeros(logits.shape, F32)
    for kk in range(TOP_K):
        chosen = jnp.where(lane == idxs[kk], 1.0, chosen)
    r_i = lax.broadcasted_iota(I32, (tm, tm), 0)
    c_i = lax.broadcasted_iota(I32, (tm, tm), 1)
    tri = jnp.where(c_i < r_i, 1.0, 0.0).astype(BF16)
    before = carry[...] + jnp.dot(tri, chosen.astype(BF16), preferred_element_type=F32)
    carry[...] = carry[...] + jnp.sum(chosen, axis=0, keepdims=True)
    cnt_ref[...] = jnp.broadcast_to(carry[...], cnt_ref.shape)

    e_out = jnp.zeros(logits.shape, I32)
    g_out = jnp.zeros(logits.shape, F32)
    r_out = jnp.zeros(logits.shape, F32)
    for kk in range(TOP_K):
        e_out = jnp.where(lane == kk, idxs[kk], e_out)
        g_out = jnp.where(lane == kk, exps[kk] / den, g_out)
        rank = jnp.sum(jnp.where(lane == idxs[kk], before, 0.0), axis=1, keepdims=True)
        r_out = jnp.where(lane == kk, rank, r_out)
    e_ref[...] = e_out
    g_ref[...] = g_out
    rank_ref[...] = r_out.astype(I32)


def _router(h, w_router, b_router):
    S, D = h.shape
    tm = min(LN_TM, S)
    w = jnp.pad(w_router, ((0, 0), (0, LANES - N_EXPERTS)))
    b = jnp.pad(b_router, (0, LANES - N_EXPERTS)).reshape(1, LANES)
    return pl.pallas_call(
        _router_kernel,
        grid=(S // tm,),
        in_specs=[pl.BlockSpec((tm, D), lambda i: (i, 0)),
                  pl.BlockSpec((D, LANES), lambda i: (0, 0)),
                  pl.BlockSpec((1, LANES), lambda i: (0, 0))],
        out_specs=[pl.BlockSpec((tm, LANES), lambda i: (i, 0))] * 3 + [pl.BlockSpec((SUB, LANES), lambda i: (0, 0))],
        out_shape=[jax.ShapeDtypeStruct((S, LANES), I32), jax.ShapeDtypeStruct((S, LANES), F32),
                   jax.ShapeDtypeStruct((S, LANES), I32), jax.ShapeDtypeStruct((SUB, LANES), F32)],
        scratch_shapes=[pltpu.VMEM((1, LANES), F32)],
        compiler_params=_cparams(("arbitrary",)),
        name="moe_router",
    )(h, w, b)


def _start_row_copies(idx_ref, idx_base, n, src_ref, dst_ref, dst_base, sem):
    def body(r, carry):
        tok = idx_ref[idx_base + r]
        pltpu.make_async_copy(src_ref.at[pl.ds(pl.multiple_of(tok * SUB, SUB), SUB)],
                              dst_ref.at[pl.ds(pl.multiple_of((dst_base + r) * SUB, SUB), SUB)], sem).start()
        return carry
    lax.fori_loop(0, n, body, 0, unroll=8)


def _wait_row_copies(n, src_ref, dst_ref, dst_base, sem):
    pltpu.make_async_copy(src_ref.at[pl.ds(0, n * SUB)], dst_ref.at[pl.ds(dst_base * SUB, n * SUB)], sem).wait()


def _rows_to_matrix(ref, base, n):
    return jnp.concatenate([ref[pl.ds(base * SUB + s, n, stride=SUB), :] for s in range(SUB)], axis=1)


def _expert_kernel(te_ref, nt_ref, tok_ref, h_ref, wgu_ref, bgu_ref, wd_ref, bd_ref, pmat_ref, o_ref,
                   xbuf, sems, wperm, *, tm):
    t = pl.program_id(0)
    nt = nt_ref[0]
    slot = t % 2

    @pl.when(t == 0)
    def _():
        _start_row_copies(tok_ref, 0, tm, h_ref, xbuf, 0, sems.at[0])

    @pl.when(t + 1 < nt)
    def _():
        _start_row_copies(tok_ref, (t + 1) * tm, tm, h_ref, xbuf, (1 - slot) * tm, sems.at[1 - slot])

    changed = jnp.logical_or(t == 0, te_ref[t] != te_ref[jnp.maximum(t - 1, 0)])

    @pl.when(jnp.logical_and(t < nt, changed))
    def _():
        blk = pmat_ref.shape[0]
        for c in range(wperm.shape[1] // blk):
            sl = slice(c * blk, (c + 1) * blk)
            wperm[:, sl] = jnp.dot(wgu_ref[0, :, sl], pmat_ref[...], preferred_element_type=F32).astype(BF16)

    @pl.when(t < nt)
    def _():
        _wait_row_copies(tm, h_ref, xbuf, slot * tm, sems.at[slot])
        xb = _rows_to_matrix(xbuf, slot * tm, tm).astype(BF16)
        hgu = jnp.dot(xb, wperm[...], preferred_element_type=F32) + bgu_ref[0]
        acts = []
        for c in range(hgu.shape[1] // (2 * LANES)):
            hg = jnp.minimum(hgu[:, 2 * c * LANES:(2 * c + 1) * LANES], SWIGLU_LIMIT)
            hl = jnp.clip(hgu[:, (2 * c + 1) * LANES:(2 * c + 2) * LANES], -SWIGLU_LIMIT, SWIGLU_LIMIT)
            acts.append((hg * jax.nn.sigmoid(SWIGLU_ALPHA * hg) * (hl + 1.0)).astype(BF16))
        act = jnp.concatenate(acts, axis=1)
        y = jnp.dot(act, wd_ref[0], preferred_element_type=F32) + bd_ref[0]
        for s in range(SUB):
            o_ref[pl.ds(s, tm, stride=SUB), :] = y[:, s * LANES:(s + 1) * LANES]

    @pl.when(t >= nt)
    def _():
        o_ref[...] = jnp.zeros(o_ref.shape, o_ref.dtype)


def _experts(h_rows, row_tok, tile_e, n_tiles_used, wgu, bgu, wd, bd, pmat):
    n_rows = row_tok.shape[0]
    tm = MOE_TM
    D, F2 = wgu.shape[1], wgu.shape[2]
    row_map = lambda t, te, nt, tok: (t, 0)
    w_map = lambda t, te, nt, tok: (te[t], 0, 0)
    return pl.pallas_call(
        functools.partial(_expert_kernel, tm=tm),
        grid_spec=pltpu.PrefetchScalarGridSpec(
            num_scalar_prefetch=3,
            grid=(n_rows // tm,),
            in_specs=[pl.BlockSpec(memory_space=pl.ANY),
                      pl.BlockSpec((1, D, F2), w_map), pl.BlockSpec((1, 1, F2), w_map),
                      pl.BlockSpec((1, F2 // 2, D), w_map), pl.BlockSpec((1, 1, D), w_map),
                      pl.BlockSpec(pmat.shape, lambda t, te, nt, tok: (0, 0))],
            out_specs=pl.BlockSpec((tm * SUB, LANES), row_map),
            scratch_shapes=[pltpu.VMEM((2 * tm * SUB, LANES), F32), pltpu.SemaphoreType.DMA((2,)),
                            pltpu.VMEM((D, F2), BF16)],
        ),
        out_shape=jax.ShapeDtypeStruct((n_rows * SUB, LANES), F32),
        compiler_params=_cparams(("arbitrary",)),
        name="moe_experts",
    )(tile_e, n_tiles_used, row_tok, h_rows, wgu, bgu, wd, bd, pmat)


def _combine_ln_kernel(dest_ref, y_ref, gate_ref, h_ref, g_ref, b_ref, out_ref, ybuf, sems, *, tm):
    i = pl.program_id(0)
    n = pl.num_programs(0)
    slot = i % 2
    per = TOP_K * tm

    @pl.when(i == 0)
    def _():
        _start_row_copies(dest_ref, 0, per, y_ref, ybuf, 0, sems.at[0])

    @pl.when(i + 1 < n)
    def _():
        _start_row_copies(dest_ref, (i + 1) * per, per, y_ref, ybuf, (1 - slot) * per, sems.at[1 - slot])

    _wait_row_copies(per, y_ref, ybuf, slot * per, sems.at[slot])
    gate = gate_ref[...]
    pieces = []
    for s in range(SUB):
        acc = None
        for kk in range(TOP_K):
            rows = ybuf[pl.ds((slot * per + kk) * SUB + s, tm, stride=TOP_K * SUB), :]
            term = gate[:, kk:kk + 1] * rows
            acc = term if acc is None else acc + term
        pieces.append(acc)
    y = jnp.concatenate(pieces, axis=1)
    out_ref[...] = _layer_norm(ALPHA_DN * h_ref[...] + y, g_ref[...], b_ref[...])


def _combine_ln(ys_rows, dest, gate, h, g, b):
    S, D = h.shape
    tm = min(COMBINE_TM, S)
    return pl.pallas_call(
        functools.partial(_combine_ln_kernel, tm=tm),
        grid_spec=pltpu.PrefetchScalarGridSpec(
            num_scalar_prefetch=1,
            grid=(S // tm,),
            in_specs=[pl.BlockSpec(memory_space=pl.ANY),
                      pl.BlockSpec((tm, LANES), lambda i, d: (i, 0)),
                      pl.BlockSpec((tm, D), lambda i, d: (i, 0)),
                      pl.BlockSpec((1, D), lambda i, d: (0, 0)),
                      pl.BlockSpec((1, D), lambda i, d: (0, 0))],
            out_specs=pl.BlockSpec((tm, D), lambda i, d: (i, 0)),
            scratch_shapes=[pltpu.VMEM((2 * TOP_K * tm * SUB, LANES), F32), pltpu.SemaphoreType.DMA((2,))],
        ),
        out_shape=jax.ShapeDtypeStruct((S, D), F32),
        compiler_params=_cparams(("arbitrary",)),
        name="moe_combine_ln",
    )(dest, ys_rows, gate, h, g.reshape(1, D), b.reshape(1, D))


def _moe_block(h, w_router, b_router, w_gu, b_gu, w_down, b_down, ln_g, ln_b):
    S, D = h.shape
    E, tm = N_EXPERTS, MOE_TM
    e_full, g_full, rank_full, cnt = _router(h, w_router, b_router)
    top_e = e_full[:, :TOP_K]

    counts = cnt[0, :E].astype(I32)
    padded = (counts + tm - 1) // tm * tm
    pends = jnp.cumsum(padded)
    pstarts = pends - padded
    dest = pstarts[top_e] + rank_full[:, :TOP_K]
    A = S * TOP_K
    n_tiles = A // tm + E
    n_rows = n_tiles * tm
    tok = jnp.broadcast_to(jnp.arange(S, dtype=I32)[:, None], (S, TOP_K))
    row_tok = jnp.zeros((n_rows,), I32).at[dest.reshape(A)].set(tok.reshape(A))
    n_used = (pends[-1] // tm).astype(I32)
    tile_start = jnp.arange(n_tiles, dtype=I32) * tm
    tile_e = jnp.minimum((pends[None, :] <= tile_start[:, None]).astype(I32).sum(axis=1), E - 1)
    tile_e = jnp.where(jnp.arange(n_tiles) < n_used, tile_e, tile_e[jnp.maximum(n_used - 1, 0)])

    blk = 2 * LANES
    pm = np.zeros((blk, blk), np.float32)
    for qq in range(LANES):
        pm[2 * qq, qq] = 1.0
        pm[2 * qq + 1, LANES + qq] = 1.0
    F2 = w_gu.shape[2]
    bgu = b_gu.reshape(E, F2 // blk, LANES, 2).transpose(0, 1, 3, 2).reshape(E, 1, F2)
    h_rows = h.reshape(S * SUB, LANES)
    ys = _experts(h_rows, row_tok, tile_e, n_used.reshape(1), w_gu.astype(BF16), bgu, w_down.astype(BF16),
                  b_down[:, None, :], jnp.asarray(pm, BF16))
    return _combine_ln(ys, dest.reshape(A), g_full, h, ln_g, ln_b)


def _diff_attention(h, w_in, lq1, lk1, lq2, lk2, subln_g, w_out, ln_g, ln_b, tabs, layer_idx):
    dv = DA_HEADS * 2 * DA_HEAD_DIM
    scale = DA_HEAD_DIM ** -0.5 * LOG2E
    segs = (Seg(0, dv, "rope", scale, BF16), Seg(dv, dv, "rope", 1.0, BF16), Seg(2 * dv, dv, "plain", 1.0, BF16))
    q, k, v = _project(h, w_in.astype(BF16), tabs["c64"], tabs["s64"], segs, DA_HEAD_DIM // 2, "da_project")
    lam_init = 0.8 - 0.6 * math.exp(-0.3 * layer_idx)
    lam4 = jnp.stack([lq1, lk1, lq2, lk2]).astype(F32)
    o = _flash("da", q, k, v.T, (lam4, subln_g.reshape(1, -1).astype(F32)), lam_init, "da_attention")
    return _out_ln(o, w_out.astype(BF16), h, ln_g, ln_b, "da_out_ln")


def _mla(h, w_in, q_norm_g, kv_norm_g, w_uq, w_ukv, w_out, ln_g, ln_b, tabs):
    H = MLA_HEADS
    D = h.shape[1]
    qk_dim = MLA_NOPE + MLA_ROPE
    pad_rope = ((0, 0), (MLA_NOPE, LANES - MLA_NOPE - MLA_ROPE))
    w_in_p = jnp.concatenate([w_in[:, :MLA_Q_RANK + MLA_KV_RANK],
                              jnp.pad(w_in[:, MLA_Q_RANK + MLA_KV_RANK:], pad_rope)], axis=1).astype(BF16)
    w_uq_p = jnp.pad(w_uq.reshape(MLA_Q_RANK, H, qk_dim), ((0, 0), (0, 0), (0, LANES - qk_dim)))
    w_uq_p = w_uq_p.reshape(MLA_Q_RANK, H * LANES).astype(BF16)
    w_ukv3 = w_ukv.reshape(MLA_KV_RANK, H, MLA_NOPE + MLA_V)
    w_uk_p = jnp.pad(w_ukv3[:, :, :MLA_NOPE], ((0, 0), (0, 0), (0, LANES - MLA_NOPE)))
    w_uk_p = w_uk_p.reshape(MLA_KV_RANK, H * LANES).astype(BF16)
    w_uv = w_ukv3[:, :, MLA_NOPE:].reshape(MLA_KV_RANK, H * MLA_V).astype(BF16)
    q, k, v = _mla_project(h, w_in_p, w_uq_p, w_uk_p, w_uv, q_norm_g.reshape(1, -1), kv_norm_g.reshape(1, -1),
                           tabs["c32"], tabs["s32"], qk_dim ** -0.5 * LOG2E)
    o = _flash("mla", q, k, v.T, (), 0.0, "mla_attention")
    return _out_ln(o, w_out.astype(BF16), h, ln_g, ln_b, "mla_out_ln")


def _nsa(h, w_in, pos_k, pos_v, ck_w1, ck_w2, cv_w1, cv_w2, w_out, ln_g, ln_b, tabs):
    S, D = h.shape
    H, G, d = NSA_HEADS, NSA_GROUPS, NSA_HEAD_DIM
    R = H // G
    st, Ls = NSA_CMP_STRIDE, NSA_SLC_LEN
    n_cmp = (S - NSA_CMP_LEN) // st + 1
    nc = S // st
    n_slc = S // Ls
    n_top = min(NSA_SLC_TOPK, n_slc)
    nslp = -(-n_slc // LANES) * LANES
    gd = G * d

    def pad_heads(w, n):
        return jnp.pad(w.reshape(D, n, d), ((0, 0), (0, 0), (0, LANES - d))).reshape(D, n * LANES)

    off = np.cumsum([0, H * d] + [gd] * 6)
    wq, wkc, wvc, wks, wvs, wkw, wvw = [w_in[:, off[t]:off[t + 1]] for t in range(7)]
    wgl = w_in[:, off[7]:].reshape(D, H, 3)
    w_a = jnp.concatenate([pad_heads(wq, H), pad_heads(wks, G), pad_heads(wkw, G), pad_heads(wvs, G),
                           pad_heads(wvw, G), wkc, wvc], axis=1).astype(BF16)
    hq, hg = H * LANES, G * LANES
    segs_a = (Seg(0, hq, "rope", d ** -0.5 * LOG2E, BF16), Seg(hq, hg, "rope", 1.0, BF16), Seg(hq + hg, hg, "rope", 1.0, BF16),
              Seg(hq + 2 * hg, hg, "plain", 1.0, BF16), Seg(hq + 3 * hg, hg, "plain", 1.0, BF16),
              Seg(hq + 4 * hg, gd, "plain", 1.0, F32), Seg(hq + 4 * hg + gd, gd, "plain", 1.0, F32))
    q, ks, kw, vs, vw, kc, vc = _project(h, w_a, tabs["c64p"], tabs["s64p"], segs_a, d // 2, "nsa_project")
    w_g = jnp.concatenate([jnp.repeat(wgl[:, :, b], d, axis=1) for b in range(3)], axis=1).astype(BF16)
    segs_g = tuple(Seg(b * H * d, H * d, "sigmoid", 1.0, F32) for b in range(3))
    gc, gs, gw = _project(h, w_g, tabs["c64p"], tabs["s64p"], segs_g, d // 2, "nsa_gates")

    def chunks(t):
        return t.reshape(nc, st, G, d).transpose(2, 0, 1, 3).reshape(G, nc, st * d)

    k_cmp = _compress(chunks(kc), pos_k, ck_w1, ck_w2, tabs["ccmp"], tabs["scmp"], True, "nsa_compress_k")
    v_cmp = _compress(chunks(vc), pos_v, cv_w1, cv_w2, tabs["ccmp"], tabs["scmp"], False, "nsa_compress_v")

    ratio = Ls // st
    amat_t = np.zeros((nslp, nc), np.float32)
    for jj in range(n_slc):
        for mm in range(ratio):
            for nn in range(NSA_CMP_LEN // st):
                c = ratio * jj + mm - nn
                if 0 <= c < n_cmp:
                    amat_t[jj, c] += 1.0
    ocw, sel = _nsa_local(q, k_cmp, v_cmp.transpose(0, 2, 1), kw, vw.T, gc, gw, jnp.asarray(amat_t, BF16), n_slc, n_top)

    tk = min(NSA_TK, S)
    per_tile = tk // Ls
    tiles_per_selblock = LANES // per_tile
    assert per_tile <= LANES - d
    kpat = np.zeros((tk, LANES), np.float32)
    for t in range(tk):
        kpat[t, d + t // Ls] = 1.0
    pm = np.zeros((tiles_per_selblock, LANES, LANES), np.float32)
    for u in range(tiles_per_selblock):
        for bb in range(per_tile):
            pm[u, per_tile * u + bb, d + bb] = MASK_BIG
    o = _nsa_select(q, ks, vs.T, sel, jnp.asarray(kpat, BF16), jnp.asarray(pm, BF16), ocw, gs)
    return _out_ln(o, w_out.astype(BF16), h, ln_g, ln_b, "nsa_out_ln")


def _rope_tables(S):
    pos = jnp.arange(S)
    c64, s64 = _rope_angles(pos, DA_HEAD_DIM)
    c32, s32 = _rope_angles(pos, MLA_ROPE)
    nc = S // NSA_CMP_STRIDE
    cc, sc = _rope_angles(jnp.arange(nc) * NSA_CMP_STRIDE + NSA_CMP_LEN - 1, NSA_HEAD_DIM)
    one = lambda n, w: jnp.ones((n, w), F32)
    zero = lambda n, w: jnp.zeros((n, w), F32)
    cat = lambda *a: jnp.concatenate(a, axis=1)
    return {
        "c64": cat(c64, c64, c64, c64), "s64": cat(-s64, s64, -s64, s64),
        "c64p": cat(c64, c64, one(S, 64)), "s64p": cat(-s64, s64, zero(S, 64)),
        "c32": cat(one(S, 64), c32, c32, one(S, 32)), "s32": cat(zero(S, 64), -s32, s32, zero(S, 32)),
        "ccmp": cat(cc, cc, one(nc, 64)), "scmp": cat(-sc, sc, zero(nc, 64)),
    }


def kernel(x, da_w_in, da_lambda_q1, da_lambda_k1, da_lambda_q2, da_lambda_k2, da_subln, da_w_out, mla_w_in, mla_q_norm, mla_kv_norm, mla_w_uq, mla_w_ukv, mla_w_out, nsa_w_in, nsa_cmp_pos_k, nsa_cmp_pos_v, nsa_cmp_k_w1, nsa_cmp_k_w2, nsa_cmp_v_w1, nsa_cmp_v_w2, nsa_w_out, ln1_g, ln1_b, ln2_g, ln2_b, moe_w_router, moe_b_router, moe_w_gu, moe_b_gu, moe_w_down, moe_b_down):
    B, S, D = x.shape
    assert B == 1 and D == D_MODEL
    tabs = _rope_tables(S)
    h = x.reshape(S, D)
    for i in range(DEPTH):
        m, j = i % N_MIXERS, i // N_MIXERS
        if m == 0:
            h = _diff_attention(h, da_w_in[j], da_lambda_q1[j], da_lambda_k1[j], da_lambda_q2[j], da_lambda_k2[j],
                                da_subln[j], da_w_out[j], ln1_g[i], ln1_b[i], tabs, i)
        elif m == 1:
            h = _mla(h, mla_w_in[j], mla_q_norm[j], mla_kv_norm[j], mla_w_uq[j], mla_w_ukv[j], mla_w_out[j],
                     ln1_g[i], ln1_b[i], tabs)
        else:
            h = _nsa(h, nsa_w_in[j], nsa_cmp_pos_k[j], nsa_cmp_pos_v[j], nsa_cmp_k_w1[j], nsa_cmp_k_w2[j],
                     nsa_cmp_v_w1[j], nsa_cmp_v_w2[j], nsa_w_out[j], ln1_g[i], ln1_b[i], tabs)
        h = _moe_block(h, moe_w_router[i], moe_b_router[i], moe_w_gu[i], moe_b_gu[i], moe_w_down[i], moe_b_down[i],
                       ln2_g[i], ln2_b[i])
    return h.reshape(B, S, D)
```

```python
import collections
import functools
import math

import numpy as np
import jax
import jax.numpy as jnp
from jax import lax
from jax.experimental import pallas as pl
from jax.experimental.pallas import tpu as pltpu

F32 = jnp.float32
BF16 = jnp.bfloat16
I32 = jnp.int32

D_MODEL = 1024
DEPTH = 4
N_MIXERS = 3
ROPE_THETA = 10000.0
LN_EPS = 1e-5
RMS_EPS = 1e-6
NEG = -1e30
ALPHA_DN = (2 * DEPTH) ** 0.25
LOG2E = math.log2(math.e)

DA_HEADS = 8
DA_HEAD_DIM = 64
MLA_HEADS = 16
MLA_NOPE = 64
MLA_ROPE = 32
MLA_V = 64
MLA_Q_RANK = 256
MLA_KV_RANK = 128
NSA_HEADS = 16
NSA_GROUPS = 4
NSA_HEAD_DIM = 64
NSA_CMP_LEN = 32
NSA_CMP_STRIDE = 16
NSA_CMP_HIDDEN = 256
NSA_SLC_LEN = 64
NSA_SLC_TOPK = 16
NSA_WINDOW = 512
NSA_FORCE = 1e9
N_EXPERTS = 32
TOP_K = 4
D_EXPERT = 1024
SWIGLU_LIMIT = 7.0
SWIGLU_ALPHA = 1.702

LANES = 128
SUB = 8
VMEM_LIMIT = 48 * 1024 * 1024

PROJ_TM = 256
ATT_TQ = 512
ATT_TK = 512
KV_UNROLL = 2
NSA_TQ = 128
NSA_TK = 512
NSA_KV_UNROLL = 4
MOE_TM = 512
LN_TM = 512
COMBINE_TM = 256
GATHER_SLOTS = 3


def _cparams(sem):
    return pltpu.CompilerParams(dimension_semantics=sem, vmem_limit_bytes=VMEM_LIMIT)


def _rope_angles(pos, dim):
    inv = ROPE_THETA ** (-jnp.arange(0, dim, 2, dtype=F32) / dim)
    ang = pos.astype(F32)[:, None] * inv[None, :]
    return jnp.cos(ang), jnp.sin(ang)


def _rope128(t, c, sg, half):
    lane = lax.broadcasted_iota(I32, t.shape, 1)
    first = (lane % (2 * half)) < half
    partner = jnp.where(first, pltpu.roll(t, LANES - half, 1), pltpu.roll(t, half, 1))
    return t * c + partner * sg


Seg = collections.namedtuple("Seg", "start width mode scale dtype")


def _proj_kernel(*refs, segs, half):
    x_ref, w_ref, c_ref, s_ref = refs[:4]
    out_refs = refs[4:]
    xb = x_ref[...].astype(BF16)
    for seg, o_ref in zip(segs, out_refs):
        acc = jnp.dot(xb, w_ref[:, seg.start:seg.start + seg.width], preferred_element_type=F32)
        if seg.mode == "rope":
            c = c_ref[...]
            sg = s_ref[...]
            for g in range(seg.width // LANES):
                t = _rope128(acc[:, g * LANES:(g + 1) * LANES], c, sg, half)
                o_ref[:, g * LANES:(g + 1) * LANES] = (t * seg.scale).astype(o_ref.dtype)
        elif seg.mode == "sigmoid":
            o_ref[...] = jax.nn.sigmoid(acc).astype(o_ref.dtype)
        else:
            o_ref[...] = (acc * seg.scale).astype(o_ref.dtype) if seg.scale != 1.0 else acc.astype(o_ref.dtype)


def _project(x, w, tab_c, tab_s, segs, half, name):
    S, K = x.shape
    tm = min(PROJ_TM, S)
    return pl.pallas_call(
        functools.partial(_proj_kernel, segs=segs, half=half),
        grid=(S // tm,),
        in_specs=[
            pl.BlockSpec((tm, K), lambda i: (i, 0)),
            pl.BlockSpec(w.shape, lambda i: (0, 0)),
            pl.BlockSpec((tm, LANES), lambda i: (i, 0)),
            pl.BlockSpec((tm, LANES), lambda i: (i, 0)),
        ],
        out_specs=[pl.BlockSpec((tm, s.width), lambda i: (i, 0)) for s in segs],
        out_shape=[jax.ShapeDtypeStruct((S, s.width), s.dtype) for s in segs],
        compiler_params=_cparams(("parallel",)),
        name=name,
    )(x, w, tab_c, tab_s)


def _qk(q, k):
    return lax.dot_general(q, k, (((1,), (1,)), ((), ())), preferred_element_type=F32)


ONES_ROWS = 16


def _flash_update(s_t, vt_ext, m_ref, acc_ref, st):
    m_old = m_ref[st]
    m_new = jnp.maximum(m_old, jnp.max(s_t, axis=0, keepdims=True))
    alpha = jnp.exp2(m_old - m_new)
    p_t = jnp.exp2(s_t - m_new).astype(BF16)
    acc_ref[st] = alpha * acc_ref[st] + jnp.dot(vt_ext, p_t, preferred_element_type=F32)
    m_ref[st] = m_new


def _flash_kernel(*refs, variant, tq, tk, lam_init):
    if variant == "da":
        lam_ref, g_ref, q_ref, k_ref, vt_ref, o_ref, qa, qb, m_ref, acc_ref = refs
    else:
        q_ref, k_ref, vt_ref, o_ref, m_ref, acc_ref = refs
    i = pl.program_id(1)
    dv = vt_ref.shape[0]

    if variant == "da":
        q = q_ref[...]
        lane = lax.broadcasted_iota(I32, q.shape, 1)
        qa[...] = jnp.where(lane < DA_HEAD_DIM, q, jnp.zeros_like(q))
        qb[...] = jnp.where(lane >= DA_HEAD_DIM, q, jnp.zeros_like(q))
    m_ref[...] = jnp.full(m_ref.shape, NEG, F32)
    acc_ref[...] = jnp.zeros(acc_ref.shape, F32)
    ones = jnp.ones((ONES_ROWS, tk), BF16)

    def tiles(js, masked):
        work = []
        for j in js:
            start = pl.multiple_of(j * tk, tk)
            if variant == "da":
                k = k_ref[pl.ds(start, tk), :]
                ops = ((k, qa[...]), (k, qb[...]))
            else:
                ops = ((k_ref[pl.ds(start, tk), :LANES], q_ref[:, :LANES]),
                       (k_ref[pl.ds(start, tk), LANES:], q_ref[:, LANES:]))
            work.append((j, start, [_qk(kk, qq) for kk, qq in ops]))
        for j, start, scores in work:
            vt_ext = jnp.concatenate([vt_ref[:, pl.ds(start, tk)], ones], axis=0)
            for st, s_t in enumerate(scores):
                if masked:
                    kpos = j * tk + lax.broadcasted_iota(I32, s_t.shape, 0)
                    qpos = i * tq + lax.broadcasted_iota(I32, s_t.shape, 1)
                    s_t = jnp.where(kpos <= qpos, s_t, NEG)
                _flash_update(s_t, vt_ext, m_ref, acc_ref, st)

    n_full = (i * tq + 1) // tk

    def body(p, carry):
        tiles([p * KV_UNROLL + u for u in range(KV_UNROLL)], False)
        return carry

    lax.fori_loop(0, n_full // KV_UNROLL, body, 0)
    for u in range(KV_UNROLL - 1):
        @pl.when(u < n_full % KV_UNROLL)
        def _():
            tiles([n_full - n_full % KV_UNROLL + u], False)
    tiles([n_full + d for d in range(max(1, tq // tk))], True)

    outs = []
    for st in range(2):
        acc = acc_ref[st]
        outs.append((acc[:dv] / acc[dv:dv + 1]).T)
    o1, o2 = outs
    if variant == "da":
        lam4 = lam_ref[...]
        lam = (jnp.exp(jnp.sum(lam4[0:1] * lam4[1:2], axis=1, keepdims=True))
               - jnp.exp(jnp.sum(lam4[2:3] * lam4[3:4], axis=1, keepdims=True)) + lam_init)
        o = o1 - lam * o2
        ms = jnp.mean(o * o, axis=-1, keepdims=True)
        o = o * lax.rsqrt(ms + LN_EPS) * g_ref[...] * (1.0 - lam_init)
    else:
        lane = lax.broadcasted_iota(I32, o1.shape, 1)
        o = jnp.where(lane < MLA_V, o1, o2)
    o_ref[...] = o.astype(o_ref.dtype)


def _flash(variant, q, k, vt, extra, lam_init, name):
    S = q.shape[0]
    tq, tk = min(ATT_TQ, S), min(ATT_TK, S)
    assert tk % tq == 0 or tq % tk == 0
    qw = LANES if variant == "da" else 2 * LANES
    n_groups = q.shape[1] // qw
    in_specs = [
        pl.BlockSpec((tq, qw), lambda h, i: (i, h)),
        pl.BlockSpec((S, qw), lambda h, i: (0, h)),
        pl.BlockSpec((LANES, S), lambda h, i: (h, 0)),
    ]
    scratch = [pltpu.VMEM((2, 1, tq), F32), pltpu.VMEM((2, LANES + ONES_ROWS, tq), F32)]
    if variant == "da":
        in_specs = [pl.BlockSpec(extra[0].shape, lambda h, i: (0, 0)),
                    pl.BlockSpec(extra[1].shape, lambda h, i: (0, 0))] + in_specs
        scratch = [pltpu.VMEM((tq, LANES), BF16)] * 2 + scratch
    return pl.pallas_call(
        functools.partial(_flash_kernel, variant=variant, tq=tq, tk=tk, lam_init=lam_init),
        grid=(n_groups, S // tq),
        in_specs=in_specs,
        out_specs=pl.BlockSpec((tq, LANES), lambda h, i: (i, h)),
        out_shape=jax.ShapeDtypeStruct((S, n_groups * LANES), BF16),
        scratch_shapes=scratch,
        compiler_params=_cparams(("parallel", "parallel")),
        name=name,
    )(*extra, q, k, vt)


def _layer_norm(x, g, b):
    mu = jnp.mean(x, axis=-1, keepdims=True)
    xc = x - mu
    var = jnp.mean(xc * xc, axis=-1, keepdims=True)
    return xc * lax.rsqrt(var + LN_EPS) * g + b


def _out_ln_kernel(o_ref, w_ref, h_ref, g_ref, b_ref, out_ref):
    y = jnp.dot(o_ref[...], w_ref[...], preferred_element_type=F32)
    out_ref[...] = _layer_norm(ALPHA_DN * h_ref[...] + y, g_ref[...], b_ref[...])


def _out_ln(o, w, h, g, b, name):
    S, K = o.shape
    D = h.shape[1]
    tm = min(LN_TM, S)
    return pl.pallas_call(
        _out_ln_kernel,
        grid=(S // tm,),
        in_specs=[
            pl.BlockSpec((tm, K), lambda i: (i, 0)),
            pl.BlockSpec((K, D), lambda i: (0, 0)),
            pl.BlockSpec((tm, D), lambda i: (i, 0)),
            pl.BlockSpec((1, D), lambda i: (0, 0)),
            pl.BlockSpec((1, D), lambda i: (0, 0)),
        ],
        out_specs=pl.BlockSpec((tm, D), lambda i: (i, 0)),
        out_shape=jax.ShapeDtypeStruct((S, D), F32),
        compiler_params=_cparams(("parallel",)),
        name=name,
    )(o, w, h, g.reshape(1, D), b.reshape(1, D))


def _rms(x, g, eps):
    return x * lax.rsqrt(jnp.mean(x * x, axis=-1, keepdims=True) + eps) * g


def _mla_proj_kernel(x_ref, win_ref, wuq_ref, wuk_ref, wuv_ref, gq_ref, gkv_ref, c_ref, s_ref,
                     q_ref, k_ref, v_ref, *, scale):
    xb = x_ref[...].astype(BF16)
    p1 = jnp.dot(xb, win_ref[...], preferred_element_type=F32)
    c_q = _rms(p1[:, :MLA_Q_RANK], gq_ref[...], RMS_EPS).astype(BF16)
    c_kv = _rms(p1[:, MLA_Q_RANK:MLA_Q_RANK + MLA_KV_RANK], gkv_ref[...], RMS_EPS).astype(BF16)
    c = c_ref[...]
    sg = s_ref[...]
    half = MLA_ROPE // 2
    k_rope = _rope128(p1[:, MLA_Q_RANK + MLA_KV_RANK:], c, sg, half)
    q = jnp.dot(c_q, wuq_ref[...], preferred_element_type=F32)
    k = jnp.dot(c_kv, wuk_ref[...], preferred_element_type=F32)
    for h in range(MLA_HEADS):
        sl = slice(h * LANES, (h + 1) * LANES)
        q_ref[:, sl] = (_rope128(q[:, sl], c, sg, half) * scale).astype(q_ref.dtype)
        k_ref[:, sl] = (k[:, sl] + k_rope).astype(k_ref.dtype)
    v_ref[...] = jnp.dot(c_kv, wuv_ref[...], preferred_element_type=F32).astype(v_ref.dtype)


def _mla_project(x, w_in, w_uq, w_uk, w_uv, gq, gkv, tab_c, tab_s, scale):
    S, D = x.shape
    tm = min(PROJ_TM, S)
    full = lambda a: pl.BlockSpec(a.shape, lambda i: (0,) * a.ndim)
    row = lambda w: pl.BlockSpec((tm, w), lambda i: (i, 0))
    hw = MLA_HEADS * LANES
    return pl.pallas_call(
        functools.partial(_mla_proj_kernel, scale=scale),
        grid=(S // tm,),
        in_specs=[row(D), full(w_in), full(w_uq), full(w_uk), full(w_uv), full(gq), full(gkv), row(LANES), row(LANES)],
        out_specs=[row(hw), row(hw), row(MLA_HEADS * MLA_V)],
        out_shape=[jax.ShapeDtypeStruct((S, hw), BF16), jax.ShapeDtypeStruct((S, hw), BF16),
                   jax.ShapeDtypeStruct((S, MLA_HEADS * MLA_V), BF16)],
        compiler_params=_cparams(("parallel",)),
        name="mla_project",
    )(x, w_in, w_uq, w_uk, w_uv, gq, gkv, tab_c, tab_s)


def _gelu_tanh(x):
    return 0.5 * x * (1.0 + jnp.tanh(math.sqrt(2.0 / math.pi) * (x + 0.044715 * (x * x * x))))


def _compress_kernel(x_ref, pa_ref, pb_ref, w1a_ref, w1b_ref, w2_ref, c_ref, s_ref, o_ref, *, nc, rope):
    x = x_ref[0]
    ha = jnp.dot((x + pa_ref[...]).astype(BF16), w1a_ref[...], preferred_element_type=F32)
    hb = jnp.dot((x + pb_ref[...]).astype(BF16), w1b_ref[...], preferred_element_type=F32)
    hid = ha + pltpu.roll(hb, nc - 1, 0)
    out = jnp.dot(_gelu_tanh(hid).astype(BF16), w2_ref[...], preferred_element_type=F32)
    if rope:
        out = _rope128(out, c_ref[...], s_ref[...], NSA_HEAD_DIM // 2)
    o_ref[0] = out.astype(o_ref.dtype)


def _compress(xc, pos, w1, w2, tab_c, tab_s, rope, name):
    G, nc, cw = xc.shape
    half = NSA_CMP_LEN // 2
    pa = pos[:half].reshape(1, cw)
    pb = pos[half:].reshape(1, cw)
    w1a = w1[:cw].astype(BF16)
    w1b = w1[cw:].astype(BF16)
    w2p = jnp.pad(w2, ((0, 0), (0, LANES - w2.shape[1]))).astype(BF16)
    full = lambda a: pl.BlockSpec(a.shape, lambda g: (0,) * a.ndim)
    return pl.pallas_call(
        functools.partial(_compress_kernel, nc=nc, rope=rope),
        grid=(G,),
        in_specs=[pl.BlockSpec((1, nc, cw), lambda g: (g, 0, 0)), full(pa), full(pb), full(w1a), full(w1b),
                  full(w2p), full(tab_c), full(tab_s)],
        out_specs=pl.BlockSpec((1, nc, LANES), lambda g: (g, 0, 0)),
        out_shape=jax.ShapeDtypeStruct((G, nc, LANES), BF16),
        compiler_params=_cparams(("parallel",)),
        name=name,
    )(xc, pa, pb, w1a, w1b, w2p, tab_c, tab_s)


def _split3(x):
    hi = x.astype(BF16)
    r = x - hi.astype(F32)
    mid = r.astype(BF16)
    lo = (r - mid.astype(F32)).astype(BF16)
    return hi, mid, lo


def _heads_to_rows(o_t, tq):
    R = NSA_HEADS // NSA_GROUPS
    return jnp.concatenate([o_t[:NSA_HEAD_DIM, r * tq:(r + 1) * tq] for r in range(R)], axis=0).T


def _nsa_local_kernel(q_ref, kc_ref, vct_ref, kw0, kw1, kw2, kw3, kw4, vw0, vw1, vw2, vw3, vw4,
                      gc_ref, gw_ref, amat_ref, o_ref, sel_ref, *, tq, nc, n_slc, n_top):
    i = pl.program_id(1)
    R = NSA_HEADS // NSA_GROUPS
    q = jnp.concatenate([q_ref[:, r * LANES:(r + 1) * LANES] for r in range(R)], axis=0)
    cols = R * tq
    qpos = i * tq + lax.broadcasted_iota(I32, (1, cols), 1) % tq

    s_t = _qk(kc_ref[0], q)
    cmp_end = lax.broadcasted_iota(I32, (nc, cols), 0) * NSA_CMP_STRIDE + (NSA_CMP_LEN - 1)
    s_t = jnp.where(cmp_end <= qpos, s_t, NEG)
    e = jnp.exp2(s_t - jnp.max(s_t, axis=0, keepdims=True))
    den = jnp.sum(e, axis=0, keepdims=True)
    p_t = e * jnp.where(qpos >= NSA_CMP_LEN - 1, 1.0 / den, 0.0)
    o_ct = jnp.dot(vct_ref[0], p_t.astype(BF16), preferred_element_type=F32)

    p_g = p_t[:, 0:tq]
    for r in range(1, R):
        p_g = p_g + p_t[:, r * tq:(r + 1) * tq]
    amat = amat_ref[...]
    imp = None
    for part in _split3(p_g):
        t = jnp.dot(amat, part, preferred_element_type=F32)
        imp = t if imp is None else imp + t
    nslp = imp.shape[0]
    jb = lax.broadcasted_iota(I32, (nslp, tq), 0)
    cur = (i * tq + lax.broadcasted_iota(I32, (1, tq), 1)) // NSA_SLC_LEN
    valid = jb <= cur
    forced = valid & ((jb == 0) | (jb >= cur - 1))
    score = jnp.where(forced, NSA_FORCE, jnp.where(valid, imp, -1.0))
    score = jnp.where(jb < n_slc, score, -2.0)
    sel = jnp.zeros((nslp, tq), F32)
    for _ in range(n_top):
        mx = jnp.max(score, axis=0, keepdims=True)
        idx = jnp.min(jnp.where(score == mx, jb, nslp), axis=0, keepdims=True)
        hit = jb == idx
        sel = jnp.where(hit, 1.0, sel)
        score = jnp.where(hit, -jnp.inf, score)
    sel_ref[0] = sel.T.astype(sel_ref.dtype)

    nwin = NSA_WINDOW // tq + 1
    k_w = jnp.concatenate([r[...] for r in (kw0, kw1, kw2, kw3, kw4)], axis=0)
    vt_w = jnp.concatenate([r[...] for r in (vw0, vw1, vw2, vw3, vw4)], axis=1)
    s_t = _qk(k_w, q)
    kpos = (i - (nwin - 1)) * tq + lax.broadcasted_iota(I32, (nwin * tq, cols), 0)
    m_w = (kpos <= qpos) & (kpos > qpos - NSA_WINDOW) & (kpos >= 0)
    s_t = jnp.where(m_w, s_t, NEG)
    e = jnp.exp2(s_t - jnp.max(s_t, axis=0, keepdims=True))
    p_w = e * (1.0 / jnp.sum(e, axis=0, keepdims=True))
    o_wt = jnp.dot(vt_w, p_w.astype(BF16), preferred_element_type=F32)

    o_ref[...] = gc_ref[...] * _heads_to_rows(o_ct, tq) + gw_ref[...] * _heads_to_rows(o_wt, tq)


def _nsa_local(q, k_cmp, vt_cmp, kw, vwt, gc, gw, amat_t, n_slc, n_top):
    S = q.shape[0]
    tq = NSA_TQ
    G = NSA_GROUPS
    nc = k_cmp.shape[1]
    nslp = amat_t.shape[0]
    gw_lanes = (NSA_HEADS // G) * NSA_HEAD_DIM
    nwin = NSA_WINDOW // tq + 1
    assert nwin == 5

    def kwin(t):
        return pl.BlockSpec((tq, LANES), lambda g, i: (jnp.maximum(i - (nwin - 1) + t, 0), g))

    def vwin(t):
        return pl.BlockSpec((LANES, tq), lambda g, i: (g, jnp.maximum(i - (nwin - 1) + t, 0)))

    in_specs = ([pl.BlockSpec((tq, (NSA_HEADS // G) * LANES), lambda g, i: (i, g)),
                 pl.BlockSpec((1, nc, LANES), lambda g, i: (g, 0, 0)),
                 pl.BlockSpec((1, LANES, nc), lambda g, i: (g, 0, 0))]
                + [kwin(t) for t in range(nwin)] + [vwin(t) for t in range(nwin)]
                + [pl.BlockSpec((tq, gw_lanes), lambda g, i: (i, g)),
                   pl.BlockSpec((tq, gw_lanes), lambda g, i: (i, g)),
                   pl.BlockSpec(amat_t.shape, lambda g, i: (0, 0))])
    return pl.pallas_call(
        functools.partial(_nsa_local_kernel, tq=tq, nc=nc, n_slc=n_slc, n_top=n_top),
        grid=(G, S // tq),
        in_specs=in_specs,
        out_specs=[pl.BlockSpec((tq, gw_lanes), lambda g, i: (i, g)),
                   pl.BlockSpec((1, tq, nslp), lambda g, i: (g, i, 0))],
        out_shape=[jax.ShapeDtypeStruct((S, NSA_HEADS * NSA_HEAD_DIM), F32),
                   jax.ShapeDtypeStruct((G, S, nslp), BF16)],
        compiler_params=_cparams(("parallel", "parallel")),
        name="nsa_local",
    )(q, k_cmp, vt_cmp, *([kw] * nwin), *([vwt] * nwin), gc, gw, amat_t)


MASK_BIG = 2.0 ** 100


def _nsa_select_kernel(q_ref, k_ref, vt_ref, sel_ref, kpat_ref, pm_ref, ocw_ref, gs_ref, o_ref, m_ref, acc_ref,
                       *, tq, tk, tiles_per_selblock):
    i = pl.program_id(1)
    R = NSA_HEADS // NSA_GROUPS
    dv = vt_ref.shape[0]
    m_ref[...] = jnp.full(m_ref.shape, NEG, F32)
    acc_ref[...] = jnp.zeros(acc_ref.shape, F32)
    ones = jnp.ones((ONES_ROWS, tk), BF16)

    def tiles(js, masked):
        exts = []
        for j in js:
            sel_blk = sel_ref[0, :, pl.ds(pl.multiple_of((j // tiles_per_selblock) * LANES, LANES), LANES)]
            exts.append(jnp.dot(sel_blk - 1.0, pm_ref[j % tiles_per_selblock], preferred_element_type=F32).astype(BF16))
        work = []
        for j, ext in zip(js, exts):
            start = pl.multiple_of(j * tk, tk)
            k_ext = k_ref[pl.ds(start, tk), :] + kpat_ref[...]
            q_ext = jnp.concatenate([q_ref[:, r * LANES:(r + 1) * LANES] + ext for r in range(R)], axis=0)
            work.append((j, start, _qk(k_ext, q_ext)))
        for j, start, s_t in work:
            vt_ext = jnp.concatenate([vt_ref[:, pl.ds(start, tk)], ones], axis=0)
            if masked:
                kpos = j * tk + lax.broadcasted_iota(I32, s_t.shape, 0)
                qpos = i * tq + lax.broadcasted_iota(I32, s_t.shape, 1) % tq
                s_t = jnp.where(kpos <= qpos, s_t, NEG)
            _flash_update(s_t, vt_ext, m_ref, acc_ref, 0)

    n_full = (i * tq + 1) // tk

    def body(p, carry):
        tiles([p * NSA_KV_UNROLL + u for u in range(NSA_KV_UNROLL)], False)
        return carry

    lax.fori_loop(0, n_full // NSA_KV_UNROLL, body, 0)
    rem = n_full % NSA_KV_UNROLL
    base = n_full - rem

    @pl.when(rem >= 2)
    def _():
        tiles([base, base + 1], False)

    @pl.when(rem % 2 == 1)
    def _():
        tiles([base + rem - 1], False)

    tiles([n_full], True)

    acc = acc_ref[0]
    o_t = acc[:dv] / acc[dv:dv + 1]
    o_ref[...] = (ocw_ref[...] + gs_ref[...] * _heads_to_rows(o_t, tq)).astype(o_ref.dtype)


def _nsa_select(q, ks, vst, sel, kpat, pm, ocw, gs):
    S = q.shape[0]
    tq, tk = NSA_TQ, kpat.shape[0]
    G = NSA_GROUPS
    R = NSA_HEADS // G
    gw_lanes = R * NSA_HEAD_DIM
    nslp = sel.shape[2]
    tiles_per_selblock = pm.shape[0]
    return pl.pallas_call(
        functools.partial(_nsa_select_kernel, tq=tq, tk=tk, tiles_per_selblock=tiles_per_selblock),
        grid=(G, S // tq),
        in_specs=[pl.BlockSpec((tq, R * LANES), lambda g, i: (i, g)),
                  pl.BlockSpec((S, LANES), lambda g, i: (0, g)),
                  pl.BlockSpec((LANES, S), lambda g, i: (g, 0)),
                  pl.BlockSpec((1, tq, nslp), lambda g, i: (g, i, 0)),
                  pl.BlockSpec(kpat.shape, lambda g, i: (0, 0)),
                  pl.BlockSpec(pm.shape, lambda g, i: (0, 0, 0)),
                  pl.BlockSpec((tq, gw_lanes), lambda g, i: (i, g)),
                  pl.BlockSpec((tq, gw_lanes), lambda g, i: (i, g))],
        out_specs=pl.BlockSpec((tq, gw_lanes), lambda g, i: (i, g)),
        out_shape=jax.ShapeDtypeStruct((S, NSA_HEADS * NSA_HEAD_DIM), BF16),
        scratch_shapes=[pltpu.VMEM((1, 1, R * tq), F32), pltpu.VMEM((1, LANES + ONES_ROWS, R * tq), F32)],
        compiler_params=_cparams(("parallel", "parallel")),
        name="nsa_select",
    )(q, ks, vst, sel, kpat, pm, ocw, gs)


def _router_kernel(h_ref, w_ref, b_ref, e_ref, g_ref, rank_ref, cnt_ref, carry):
    @pl.when(pl.program_id(0) == 0)
    def _():
        carry[...] = jnp.zeros(carry.shape, F32)

    x = h_ref[...]
    w = w_ref[...]
    x_hi = x.astype(BF16)
    x_lo = (x - x_hi.astype(F32)).astype(BF16)
    w_hi = w.astype(BF16)
    w_lo = (w - w_hi.astype(F32)).astype(BF16)
    logits = (jnp.dot(x_hi, w_hi, preferred_element_type=F32) + jnp.dot(x_hi, w_lo, preferred_element_type=F32)
              + jnp.dot(x_lo, w_hi, preferred_element_type=F32)) + b_ref[...]
    tm = logits.shape[0]
    lane = lax.broadcasted_iota(I32, logits.shape, 1)
    work = jnp.where(lane < N_EXPERTS, logits, -jnp.inf)
    vals, idxs = [], []
    for _ in range(TOP_K):
        mx = jnp.max(work, axis=1, keepdims=True)
        idx = jnp.min(jnp.where(work == mx, lane, LANES), axis=1, keepdims=True)
        vals.append(mx)
        idxs.append(idx)
        work = jnp.where(lane == idx, -jnp.inf, work)
    exps = [jnp.exp(v - vals[0]) for v in vals]
    den = exps[0]
    for t in exps[1:]:
        den = den + t

    chosen = jnp.zeros(logits.shape, F32)
    for kk in range(TOP_K):
        chosen = jnp.where(lane == idxs[kk], 1.0, chosen)
    r_i = lax.broadcasted_iota(I32, (tm, tm), 0)
    c_i = lax.broadcasted_iota(I32, (tm, tm), 1)
    tri = jnp.where(c_i < r_i, 1.0, 0.0).astype(BF16)
    before = carry[...] + jnp.dot(tri, chosen.astype(BF16), preferred_element_type=F32)
    carry[...] = carry[...] + jnp.sum(chosen, axis=0, keepdims=True)
    cnt_ref[...] = jnp.broadcast_to(carry[...], cnt_ref.shape)

    e_out = jnp.zeros(logits.shape, I32)
    g_out = jnp.zeros(logits.shape, F32)
    r_out = jnp.zeros(logits.shape, F32)
    for kk in range(TOP_K):
        e_out = jnp.where(lane == kk, idxs[kk], e_out)
        g_out = jnp.where(lane == kk, exps[kk] / den, g_out)
        rank = jnp.sum(jnp.where(lane == idxs[kk], before, 0.0), axis=1, keepdims=True)
        r_out = jnp.where(lane == kk, rank, r_out)
    e_ref[...] = e_out
    g_ref[...] = g_out
    rank_ref[...] = r_out.astype(I32)


def _router(h, w_router, b_router):
    S, D = h.shape
    tm = min(LN_TM, S)
    w = jnp.pad(w_router, ((0, 0), (0, LANES - N_EXPERTS)))
    b = jnp.pad(b_router, (0, LANES - N_EXPERTS)).reshape(1, LANES)
    return pl.pallas_call(
        _router_kernel,
        grid=(S // tm,),
        in_specs=[pl.BlockSpec((tm, D), lambda i: (i, 0)),
                  pl.BlockSpec((D, LANES), lambda i: (0, 0)),
                  pl.BlockSpec((1, LANES), lambda i: (0, 0))],
        out_specs=[pl.BlockSpec((tm, LANES), lambda i: (i, 0))] * 3 + [pl.BlockSpec((SUB, LANES), lambda i: (0, 0))],
        out_shape=[jax.ShapeDtypeStruct((S, LANES), I32), jax.ShapeDtypeStruct((S, LANES), F32),
                   jax.ShapeDtypeStruct((S, LANES), I32), jax.ShapeDtypeStruct((SUB, LANES), F32)],
        scratch_shapes=[pltpu.VMEM((1, LANES), F32)],
        compiler_params=_cparams(("arbitrary",)),
        name="moe_router",
    )(h, w, b)


def _start_row_copies(idx_ref, idx_base, n, src_ref, dst_ref, dst_base, sem):
    def body(r, carry):
        tok = idx_ref[idx_base + r]
        pltpu.make_async_copy(src_ref.at[pl.ds(pl.multiple_of(tok * SUB, SUB), SUB)],
                              dst_ref.at[pl.ds(pl.multiple_of((dst_base + r) * SUB, SUB), SUB)], sem).start()
        return carry
    lax.fori_loop(0, n, body, 0, unroll=8)


def _wait_row_copies(n, src_ref, dst_ref, dst_base, sem):
    pltpu.make_async_copy(src_ref.at[pl.ds(0, n * SUB)], dst_ref.at[pl.ds(dst_base * SUB, n * SUB)], sem).wait()


def _rows_to_matrix(ref, base, n):
    return jnp.concatenate([ref[pl.ds(base * SUB + s, n, stride=SUB), :] for s in range(SUB)], axis=1)


def _expert_kernel(te_ref, nt_ref, tok_ref, h_ref, wgu_ref, bgu_ref, wd_ref, bd_ref, pmat_ref, o_ref,
                   xbuf, sems, wperm, *, tm):
    t = pl.program_id(0)
    nt = nt_ref[0]
    slot = t % GATHER_SLOTS

    @pl.when(t == 0)
    def _():
        for a in range(GATHER_SLOTS - 1):
            @pl.when(a < nt)
            def _():
                _start_row_copies(tok_ref, a * tm, tm, h_ref, xbuf, a * tm, sems.at[a])

    ahead = t + GATHER_SLOTS - 1

    @pl.when(ahead < nt)
    def _():
        aslot = ahead % GATHER_SLOTS
        _start_row_copies(tok_ref, ahead * tm, tm, h_ref, xbuf, aslot * tm, sems.at[aslot])

    changed = jnp.logical_or(t == 0, te_ref[t] != te_ref[jnp.maximum(t - 1, 0)])

    @pl.when(jnp.logical_and(t < nt, changed))
    def _():
        blk = pmat_ref.shape[0]
        for c in range(wperm.shape[1] // blk):
            sl = slice(c * blk, (c + 1) * blk)
            wperm[:, sl] = jnp.dot(wgu_ref[0, 0, :, sl], pmat_ref[...], preferred_element_type=F32).astype(BF16)

    @pl.when(t < nt)
    def _():
        _wait_row_copies(tm, h_ref, xbuf, slot * tm, sems.at[slot])
        xb = _rows_to_matrix(xbuf, slot * tm, tm).astype(BF16)
        hgu = jnp.dot(xb, wperm[...], preferred_element_type=F32) + bgu_ref[0, 0]
        acts = []
        for c in range(hgu.shape[1] // (2 * LANES)):
            hg = jnp.minimum(hgu[:, 2 * c * LANES:(2 * c + 1) * LANES], SWIGLU_LIMIT)
            hl = jnp.clip(hgu[:, (2 * c + 1) * LANES:(2 * c + 2) * LANES], -SWIGLU_LIMIT, SWIGLU_LIMIT)
            acts.append((hg * jax.nn.sigmoid(SWIGLU_ALPHA * hg) * (hl + 1.0)).astype(BF16))
        act = jnp.concatenate(acts, axis=1)
        y = jnp.dot(act, wd_ref[0, 0], preferred_element_type=F32) + bd_ref[0, 0]
        for s in range(SUB):
            o_ref[pl.ds(s, tm, stride=SUB), :] = y[:, s * LANES:(s + 1) * LANES]

    @pl.when(t >= nt)
    def _():
        o_ref[...] = jnp.zeros(o_ref.shape, o_ref.dtype)


def _experts(h_rows, row_tok, tile_e, n_tiles_used, layer, wgu, bgu, wd, bd, pmat):
    n_rows = row_tok.shape[0]
    tm = MOE_TM
    D, F2 = wgu.shape[2], wgu.shape[3]
    row_map = lambda t, te, nt, tok: (t, 0)
    w_map = lambda t, te, nt, tok: (layer, te[t], 0, 0)
    return pl.pallas_call(
        functools.partial(_expert_kernel, tm=tm),
        grid_spec=pltpu.PrefetchScalarGridSpec(
            num_scalar_prefetch=3,
            grid=(n_rows // tm,),
            in_specs=[pl.BlockSpec(memory_space=pl.ANY),
                      pl.BlockSpec((1, 1, D, F2), w_map), pl.BlockSpec((1, 1, 1, F2), w_map),
                      pl.BlockSpec((1, 1, F2 // 2, D), w_map), pl.BlockSpec((1, 1, 1, D), w_map),
                      pl.BlockSpec(pmat.shape, lambda t, te, nt, tok: (0, 0))],
            out_specs=pl.BlockSpec((tm * SUB, LANES), row_map),
            scratch_shapes=[pltpu.VMEM((GATHER_SLOTS * tm * SUB, LANES), F32), pltpu.SemaphoreType.DMA((GATHER_SLOTS,)),
                            pltpu.VMEM((D, F2), BF16)],
        ),
        out_shape=jax.ShapeDtypeStruct((n_rows * SUB, LANES), F32),
        compiler_params=_cparams(("arbitrary",)),
        name="moe_experts",
    )(tile_e, n_tiles_used, row_tok, h_rows, wgu, bgu, wd, bd, pmat)


def _combine_ln_kernel(dest_ref, y_ref, gate_ref, h_ref, g_ref, b_ref, out_ref, ybuf, sems, *, tm):
    i = pl.program_id(0)
    n = pl.num_programs(0)
    slot = i % 2
    per = TOP_K * tm

    @pl.when(i == 0)
    def _():
        _start_row_copies(dest_ref, 0, per, y_ref, ybuf, 0, sems.at[0])

    @pl.when(i + 1 < n)
    def _():
        _start_row_copies(dest_ref, (i + 1) * per, per, y_ref, ybuf, (1 - slot) * per, sems.at[1 - slot])

    _wait_row_copies(per, y_ref, ybuf, slot * per, sems.at[slot])
    gate = gate_ref[...]
    pieces = []
    for s in range(SUB):
        acc = None
        for kk in range(TOP_K):
            rows = ybuf[pl.ds((slot * per + kk) * SUB + s, tm, stride=TOP_K * SUB), :]
            term = gate[:, kk:kk + 1] * rows
            acc = term if acc is None else acc + term
        pieces.append(acc)
    y = jnp.concatenate(pieces, axis=1)
    out_ref[...] = _layer_norm(ALPHA_DN * h_ref[...] + y, g_ref[...], b_ref[...])


def _combine_ln(ys_rows, dest, gate, h, g, b):
    S, D = h.shape
    tm = min(COMBINE_TM, S)
    return pl.pallas_call(
        functools.partial(_combine_ln_kernel, tm=tm),
        grid_spec=pltpu.PrefetchScalarGridSpec(
            num_scalar_prefetch=1,
            grid=(S // tm,),
            in_specs=[pl.BlockSpec(memory_space=pl.ANY),
                      pl.BlockSpec((tm, LANES), lambda i, d: (i, 0)),
                      pl.BlockSpec((tm, D), lambda i, d: (i, 0)),
                      pl.BlockSpec((1, D), lambda i, d: (0, 0)),
                      pl.BlockSpec((1, D), lambda i, d: (0, 0))],
            out_specs=pl.BlockSpec((tm, D), lambda i, d: (i, 0)),
            scratch_shapes=[pltpu.VMEM((2 * TOP_K * tm * SUB, LANES), F32), pltpu.SemaphoreType.DMA((2,))],
        ),
        out_shape=jax.ShapeDtypeStruct((S, D), F32),
        compiler_params=_cparams(("arbitrary",)),
        name="moe_combine_ln",
    )(dest, ys_rows, gate, h, g.reshape(1, D), b.reshape(1, D))


def _moe_block(h, w_router, b_router, layer, expert_params, ln_g, ln_b):
    S, D = h.shape
    E, tm = N_EXPERTS, MOE_TM
    e_full, g_full, rank_full, cnt = _router(h, w_router, b_router)
    top_e = e_full[:, :TOP_K]

    counts = cnt[0, :E].astype(I32)
    padded = (counts + tm - 1) // tm * tm
    pends = jnp.cumsum(padded)
    pstarts = pends - padded
    dest = pstarts[top_e] + rank_full[:, :TOP_K]
    A = S * TOP_K
    n_tiles = A // tm + E
    n_rows = n_tiles * tm
    tok = jnp.broadcast_to(jnp.arange(S, dtype=I32)[:, None], (S, TOP_K))
    row_tok = jnp.zeros((n_rows,), I32).at[dest.reshape(A)].set(tok.reshape(A))
    n_used = (pends[-1] // tm).astype(I32)
    tile_start = jnp.arange(n_tiles, dtype=I32) * tm
    tile_e = jnp.minimum((pends[None, :] <= tile_start[:, None]).astype(I32).sum(axis=1), E - 1)
    tile_e = jnp.where(jnp.arange(n_tiles) < n_used, tile_e, tile_e[jnp.maximum(n_used - 1, 0)])

    h_rows = h.reshape(S * SUB, LANES)
    ys = _experts(h_rows, row_tok, tile_e, n_used.reshape(1), layer, *expert_params)
    return _combine_ln(ys, dest.reshape(A), g_full, h, ln_g, ln_b)


def _expert_params(w_gu, b_gu, w_down, b_down):
    L, E, _, F2 = w_gu.shape
    blk = 2 * LANES
    pm = np.zeros((blk, blk), np.float32)
    for qq in range(LANES):
        pm[2 * qq, qq] = 1.0
        pm[2 * qq + 1, LANES + qq] = 1.0
    bgu = b_gu.reshape(L, E, F2 // blk, LANES, 2).transpose(0, 1, 2, 4, 3).reshape(L, E, 1, F2)
    return w_gu.astype(BF16), bgu, w_down.astype(BF16), b_down[:, :, None, :], jnp.asarray(pm, BF16)


def _diff_attention(h, w_in, lq1, lk1, lq2, lk2, subln_g, w_out, ln_g, ln_b, tabs, layer_idx):
    dv = DA_HEADS * 2 * DA_HEAD_DIM
    scale = DA_HEAD_DIM ** -0.5 * LOG2E
    segs = (Seg(0, dv, "rope", scale, BF16), Seg(dv, dv, "rope", 1.0, BF16), Seg(2 * dv, dv, "plain", 1.0, BF16))
    q, k, v = _project(h, w_in.astype(BF16), tabs["c64"], tabs["s64"], segs, DA_HEAD_DIM // 2, "da_project")
    lam_init = 0.8 - 0.6 * math.exp(-0.3 * layer_idx)
    lam4 = jnp.stack([lq1, lk1, lq2, lk2]).astype(F32)
    o = _flash("da", q, k, v.T, (lam4, subln_g.reshape(1, -1).astype(F32)), lam_init, "da_attention")
    return _out_ln(o, w_out.astype(BF16), h, ln_g, ln_b, "da_out_ln")


def _mla(h, w_in, q_norm_g, kv_norm_g, w_uq, w_ukv, w_out, ln_g, ln_b, tabs):
    H = MLA_HEADS
    D = h.shape[1]
    qk_dim = MLA_NOPE + MLA_ROPE
    pad_rope = ((0, 0), (MLA_NOPE, LANES - MLA_NOPE - MLA_ROPE))
    w_in_p = jnp.concatenate([w_in[:, :MLA_Q_RANK + MLA_KV_RANK],
                              jnp.pad(w_in[:, MLA_Q_RANK + MLA_KV_RANK:], pad_rope)], axis=1).astype(BF16)
    w_uq_p = jnp.pad(w_uq.reshape(MLA_Q_RANK, H, qk_dim), ((0, 0), (0, 0), (0, LANES - qk_dim)))
    w_uq_p = w_uq_p.reshape(MLA_Q_RANK, H * LANES).astype(BF16)
    w_ukv3 = w_ukv.reshape(MLA_KV_RANK, H, MLA_NOPE + MLA_V)
    w_uk_p = jnp.pad(w_ukv3[:, :, :MLA_NOPE], ((0, 0), (0, 0), (0, LANES - MLA_NOPE)))
    w_uk_p = w_uk_p.reshape(MLA_KV_RANK, H * LANES).astype(BF16)
    w_uv = w_ukv3[:, :, MLA_NOPE:].reshape(MLA_KV_RANK, H * MLA_V).astype(BF16)
    q, k, v = _mla_project(h, w_in_p, w_uq_p, w_uk_p, w_uv, q_norm_g.reshape(1, -1), kv_norm_g.reshape(1, -1),
                           tabs["c32"], tabs["s32"], qk_dim ** -0.5 * LOG2E)
    o = _flash("mla", q, k, v.T, (), 0.0, "mla_attention")
    return _out_ln(o, w_out.astype(BF16), h, ln_g, ln_b, "mla_out_ln")


def _nsa(h, w_in, pos_k, pos_v, ck_w1, ck_w2, cv_w1, cv_w2, w_out, ln_g, ln_b, tabs):
    S, D = h.shape
    H, G, d = NSA_HEADS, NSA_GROUPS, NSA_HEAD_DIM
    R = H // G
    st, Ls = NSA_CMP_STRIDE, NSA_SLC_LEN
    n_cmp = (S - NSA_CMP_LEN) // st + 1
    nc = S // st
    n_slc = S // Ls
    n_top = min(NSA_SLC_TOPK, n_slc)
    nslp = -(-n_slc // LANES) * LANES
    gd = G * d

    def pad_heads(w, n):
        return jnp.pad(w.reshape(D, n, d), ((0, 0), (0, 0), (0, LANES - d))).reshape(D, n * LANES)

    off = np.cumsum([0, H * d] + [gd] * 6)
    wq, wkc, wvc, wks, wvs, wkw, wvw = [w_in[:, off[t]:off[t + 1]] for t in range(7)]
    wgl = w_in[:, off[7]:].reshape(D, H, 3)
    w_a = jnp.concatenate([pad_heads(wq, H), pad_heads(wks, G), pad_heads(wkw, G), pad_heads(wvs, G),
                           pad_heads(wvw, G), wkc, wvc], axis=1).astype(BF16)
    hq, hg = H * LANES, G * LANES
    segs_a = (Seg(0, hq, "rope", d ** -0.5 * LOG2E, BF16), Seg(hq, hg, "rope", 1.0, BF16), Seg(hq + hg, hg, "rope", 1.0, BF16),
              Seg(hq + 2 * hg, hg, "plain", 1.0, BF16), Seg(hq + 3 * hg, hg, "plain", 1.0, BF16),
              Seg(hq + 4 * hg, gd, "plain", 1.0, F32), Seg(hq + 4 * hg + gd, gd, "plain", 1.0, F32))
    q, ks, kw, vs, vw, kc, vc = _project(h, w_a, tabs["c64p"], tabs["s64p"], segs_a, d // 2, "nsa_project")
    w_g = jnp.concatenate([jnp.repeat(wgl[:, :, b], d, axis=1) for b in range(3)], axis=1).astype(BF16)
    segs_g = tuple(Seg(b * H * d, H * d, "sigmoid", 1.0, F32) for b in range(3))
    gc, gs, gw = _project(h, w_g, tabs["c64p"], tabs["s64p"], segs_g, d // 2, "nsa_gates")

    def chunks(t):
        return t.reshape(nc, st, G, d).transpose(2, 0, 1, 3).reshape(G, nc, st * d)

    k_cmp = _compress(chunks(kc), pos_k, ck_w1, ck_w2, tabs["ccmp"], tabs["scmp"], True, "nsa_compress_k")
    v_cmp = _compress(chunks(vc), pos_v, cv_w1, cv_w2, tabs["ccmp"], tabs["scmp"], False, "nsa_compress_v")

    ratio = Ls // st
    amat_t = np.zeros((nslp, nc), np.float32)
    for jj in range(n_slc):
        for mm in range(ratio):
            for nn in range(NSA_CMP_LEN // st):
                c = ratio * jj + mm - nn
                if 0 <= c < n_cmp:
                    amat_t[jj, c] += 1.0
    ocw, sel = _nsa_local(q, k_cmp, v_cmp.transpose(0, 2, 1), kw, vw.T, gc, gw, jnp.asarray(amat_t, BF16), n_slc, n_top)

    tk = min(NSA_TK, S)
    per_tile = tk // Ls
    tiles_per_selblock = LANES // per_tile
    assert per_tile <= LANES - d
    kpat = np.zeros((tk, LANES), np.float32)
    for t in range(tk):
        kpat[t, d + t // Ls] = 1.0
    pm = np.zeros((tiles_per_selblock, LANES, LANES), np.float32)
    for u in range(tiles_per_selblock):
        for bb in range(per_tile):
            pm[u, per_tile * u + bb, d + bb] = MASK_BIG
    o = _nsa_select(q, ks, vs.T, sel, jnp.asarray(kpat, BF16), jnp.asarray(pm, BF16), ocw, gs)
    return _out_ln(o, w_out.astype(BF16), h, ln_g, ln_b, "nsa_out_ln")


def _rope_tables(S):
    pos = jnp.arange(S)
    c64, s64 = _rope_angles(pos, DA_HEAD_DIM)
    c32, s32 = _rope_angles(pos, MLA_ROPE)
    nc = S // NSA_CMP_STRIDE
    cc, sc = _rope_angles(jnp.arange(nc) * NSA_CMP_STRIDE + NSA_CMP_LEN - 1, NSA_HEAD_DIM)
    one = lambda n, w: jnp.ones((n, w), F32)
    zero = lambda n, w: jnp.zeros((n, w), F32)
    cat = lambda *a: jnp.concatenate(a, axis=1)
    return {
        "c64": cat(c64, c64, c64, c64), "s64": cat(-s64, s64, -s64, s64),
        "c64p": cat(c64, c64, one(S, 64)), "s64p": cat(-s64, s64, zero(S, 64)),
        "c32": cat(one(S, 64), c32, c32, one(S, 32)), "s32": cat(zero(S, 64), -s32, s32, zero(S, 32)),
        "ccmp": cat(cc, cc, one(nc, 64)), "scmp": cat(-sc, sc, zero(nc, 64)),
    }


def kernel(x, da_w_in, da_lambda_q1, da_lambda_k1, da_lambda_q2, da_lambda_k2, da_subln, da_w_out, mla_w_in, mla_q_norm, mla_kv_norm, mla_w_uq, mla_w_ukv, mla_w_out, nsa_w_in, nsa_cmp_pos_k, nsa_cmp_pos_v, nsa_cmp_k_w1, nsa_cmp_k_w2, nsa_cmp_v_w1, nsa_cmp_v_w2, nsa_w_out, ln1_g, ln1_b, ln2_g, ln2_b, moe_w_router, moe_b_router, moe_w_gu, moe_b_gu, moe_w_down, moe_b_down):
    B, S, D = x.shape
    assert B == 1 and D == D_MODEL
    tabs = _rope_tables(S)
    expert_params = _expert_params(moe_w_gu, moe_b_gu, moe_w_down, moe_b_down)
    h = x.reshape(S, D)
    for i in range(DEPTH):
        m, j = i % N_MIXERS, i // N_MIXERS
        if m == 0:
            h = _diff_attention(h, da_w_in[j], da_lambda_q1[j], da_lambda_k1[j], da_lambda_q2[j], da_lambda_k2[j],
                                da_subln[j], da_w_out[j], ln1_g[i], ln1_b[i], tabs, i)
        elif m == 1:
            h = _mla(h, mla_w_in[j], mla_q_norm[j], mla_kv_norm[j], mla_w_uq[j], mla_w_ukv[j], mla_w_out[j],
                     ln1_g[i], ln1_b[i], tabs)
        else:
            h = _nsa(h, nsa_w_in[j], nsa_cmp_pos_k[j], nsa_cmp_pos_v[j], nsa_cmp_k_w1[j], nsa_cmp_k_w2[j],
                     nsa_cmp_v_w1[j], nsa_cmp_v_w2[j], nsa_w_out[j], ln1_g[i], ln1_b[i], tabs)
        h = _moe_block(h, moe_w_router[i], moe_b_router[i], i, expert_params, ln2_g[i], ln2_b[i])
    return h.reshape(B, S, D)
```

```python
import collections
import functools
import math

import numpy as np
import jax
import jax.numpy as jnp
from jax import lax
from jax.experimental import pallas as pl
from jax.experimental.pallas import tpu as pltpu

F32 = jnp.float32
BF16 = jnp.bfloat16
I32 = jnp.int32

D_MODEL = 1024
DEPTH = 4
N_MIXERS = 3
ROPE_THETA = 10000.0
LN_EPS = 1e-5
RMS_EPS = 1e-6
NEG = -1e30
ALPHA_DN = (2 * DEPTH) ** 0.25
LOG2E = math.log2(math.e)

DA_HEADS = 8
DA_HEAD_DIM = 64
MLA_HEADS = 16
MLA_NOPE = 64
MLA_ROPE = 32
MLA_V = 64
MLA_Q_RANK = 256
MLA_KV_RANK = 128
NSA_HEADS = 16
NSA_GROUPS = 4
NSA_HEAD_DIM = 64
NSA_CMP_LEN = 32
NSA_CMP_STRIDE = 16
NSA_CMP_HIDDEN = 256
NSA_SLC_LEN = 64
NSA_SLC_TOPK = 16
NSA_WINDOW = 512
NSA_FORCE = 1e9
N_EXPERTS = 32
TOP_K = 4
D_EXPERT = 1024
SWIGLU_LIMIT = 7.0
SWIGLU_ALPHA = 1.702

LANES = 128
SUB = 8
VMEM_LIMIT = 48 * 1024 * 1024

PROJ_TM = 256
ATT_TQ = 512
ATT_TK = 512
KV_UNROLL = 2
NSA_TQ = 128
NSA_SEL_TQ = 256
NSA_TK = 512
NSA_KV_UNROLL = 4
MOE_TM = 512
LN_TM = 512
COMBINE_TM = 256
GATHER_SLOTS = 3


def _cparams(sem):
    return pltpu.CompilerParams(dimension_semantics=sem, vmem_limit_bytes=VMEM_LIMIT)


def _rope_angles(pos, dim):
    inv = ROPE_THETA ** (-jnp.arange(0, dim, 2, dtype=F32) / dim)
    ang = pos.astype(F32)[:, None] * inv[None, :]
    return jnp.cos(ang), jnp.sin(ang)


def _rope128(t, c, sg, half):
    lane = lax.broadcasted_iota(I32, t.shape, 1)
    first = (lane % (2 * half)) < half
    partner = jnp.where(first, pltpu.roll(t, LANES - half, 1), pltpu.roll(t, half, 1))
    return t * c + partner * sg


Seg = collections.namedtuple("Seg", "start width mode scale dtype")


def _proj_kernel(*refs, segs, half):
    x_ref, w_ref, c_ref, s_ref = refs[:4]
    out_refs = refs[4:]
    xb = x_ref[...].astype(BF16)
    for seg, o_ref in zip(segs, out_refs):
        acc = jnp.dot(xb, w_ref[:, seg.start:seg.start + seg.width], preferred_element_type=F32)
        if seg.mode == "rope":
            c = c_ref[...]
            sg = s_ref[...]
            for g in range(seg.width // LANES):
                t = _rope128(acc[:, g * LANES:(g + 1) * LANES], c, sg, half)
                o_ref[:, g * LANES:(g + 1) * LANES] = (t * seg.scale).astype(o_ref.dtype)
        elif seg.mode == "sigmoid":
            o_ref[...] = jax.nn.sigmoid(acc).astype(o_ref.dtype)
        else:
            o_ref[...] = (acc * seg.scale).astype(o_ref.dtype) if seg.scale != 1.0 else acc.astype(o_ref.dtype)


def _project(x, w, tab_c, tab_s, segs, half, name):
    S, K = x.shape
    tm = min(PROJ_TM, S)
    return pl.pallas_call(
        functools.partial(_proj_kernel, segs=segs, half=half),
        grid=(S // tm,),
        in_specs=[
            pl.BlockSpec((tm, K), lambda i: (i, 0)),
            pl.BlockSpec(w.shape, lambda i: (0, 0)),
            pl.BlockSpec((tm, LANES), lambda i: (i, 0)),
            pl.BlockSpec((tm, LANES), lambda i: (i, 0)),
        ],
        out_specs=[pl.BlockSpec((tm, s.width), lambda i: (i, 0)) for s in segs],
        out_shape=[jax.ShapeDtypeStruct((S, s.width), s.dtype) for s in segs],
        compiler_params=_cparams(("parallel",)),
        name=name,
    )(x, w, tab_c, tab_s)


def _qk(q, k):
    return lax.dot_general(q, k, (((1,), (1,)), ((), ())), preferred_element_type=F32)


ONES_ROWS = 16


def _flash_update(s_t, vt_ext, m_ref, acc_ref, st):
    m_old = m_ref[st]
    m_new = jnp.maximum(m_old, jnp.max(s_t, axis=0, keepdims=True))
    alpha = jnp.exp2(m_old - m_new)
    p_t = jnp.exp2(s_t - m_new).astype(BF16)
    acc_ref[st] = alpha * acc_ref[st] + jnp.dot(vt_ext, p_t, preferred_element_type=F32)
    m_ref[st] = m_new


def _flash_kernel(*refs, variant, tq, tk, lam_init):
    if variant == "da":
        lam_ref, g_ref, q_ref, k_ref, vt_ref, o_ref, qa, qb, m_ref, acc_ref = refs
    else:
        q_ref, k_ref, vt_ref, o_ref, m_ref, acc_ref = refs
    i = pl.program_id(1)
    dv = vt_ref.shape[0]

    if variant == "da":
        q = q_ref[...]
        lane = lax.broadcasted_iota(I32, q.shape, 1)
        qa[...] = jnp.where(lane < DA_HEAD_DIM, q, jnp.zeros_like(q))
        qb[...] = jnp.where(lane >= DA_HEAD_DIM, q, jnp.zeros_like(q))
    m_ref[...] = jnp.full(m_ref.shape, NEG, F32)
    acc_ref[...] = jnp.zeros(acc_ref.shape, F32)
    ones = jnp.ones((ONES_ROWS, tk), BF16)

    def scores_of(js):
        out = []
        for j in js:
            start = pl.multiple_of(j * tk, tk)
            if variant == "da":
                k = k_ref[pl.ds(start, tk), :]
                ops = ((k, qa[...]), (k, qb[...]))
            else:
                ops = ((k_ref[pl.ds(start, tk), :LANES], q_ref[:, :LANES]),
                       (k_ref[pl.ds(start, tk), LANES:], q_ref[:, LANES:]))
            out.append([_qk(kk, qq) for kk, qq in ops])
        return out

    def update(js, all_scores, masked):
        for j, scores in zip(js, all_scores):
            start = pl.multiple_of(j * tk, tk)
            vt_ext = jnp.concatenate([vt_ref[:, pl.ds(start, tk)], ones], axis=0)
            for st, s_t in enumerate(scores):
                if masked:
                    kpos = j * tk + lax.broadcasted_iota(I32, s_t.shape, 0)
                    qpos = i * tq + lax.broadcasted_iota(I32, s_t.shape, 1)
                    s_t = jnp.where(kpos <= qpos, s_t, NEG)
                _flash_update(s_t, vt_ext, m_ref, acc_ref, st)

    def tiles(js, masked):
        update(js, scores_of(js), masked)

    n_full = (i * tq + 1) // tk

    def body(p, carry):
        tiles([p * KV_UNROLL + u for u in range(KV_UNROLL)], False)
        return carry

    lax.fori_loop(0, n_full // KV_UNROLL, body, 0)
    for u in range(KV_UNROLL - 1):
        @pl.when(u < n_full % KV_UNROLL)
        def _():
            tiles([n_full - n_full % KV_UNROLL + u], False)
    tiles([n_full + d for d in range(max(1, tq // tk))], True)

    outs = []
    for st in range(2):
        acc = acc_ref[st]
        outs.append((acc[:dv] / acc[dv:dv + 1]).T)
    o1, o2 = outs
    if variant == "da":
        lam4 = lam_ref[...]
        lam = (jnp.exp(jnp.sum(lam4[0:1] * lam4[1:2], axis=1, keepdims=True))
               - jnp.exp(jnp.sum(lam4[2:3] * lam4[3:4], axis=1, keepdims=True)) + lam_init)
        o = o1 - lam * o2
        ms = jnp.mean(o * o, axis=-1, keepdims=True)
        o = o * lax.rsqrt(ms + LN_EPS) * g_ref[...] * (1.0 - lam_init)
    else:
        lane = lax.broadcasted_iota(I32, o1.shape, 1)
        o = jnp.where(lane < MLA_V, o1, o2)
    o_ref[...] = o.astype(o_ref.dtype)


def _flash(variant, q, k, vt, extra, lam_init, name):
    S = q.shape[0]
    tq, tk = min(ATT_TQ, S), min(ATT_TK, S)
    assert tk % tq == 0 or tq % tk == 0
    qw = LANES if variant == "da" else 2 * LANES
    n_groups = q.shape[1] // qw
    in_specs = [
        pl.BlockSpec((tq, qw), lambda h, i: (i, h)),
        pl.BlockSpec((S, qw), lambda h, i: (0, h), pipeline_mode=pl.Buffered(1)),
        pl.BlockSpec((LANES, S), lambda h, i: (h, 0), pipeline_mode=pl.Buffered(1)),
    ]
    scratch = [pltpu.VMEM((2, 1, tq), F32), pltpu.VMEM((2, LANES + ONES_ROWS, tq), F32)]
    if variant == "da":
        in_specs = [pl.BlockSpec(extra[0].shape, lambda h, i: (0, 0)),
                    pl.BlockSpec(extra[1].shape, lambda h, i: (0, 0))] + in_specs
        scratch = [pltpu.VMEM((tq, LANES), BF16)] * 2 + scratch
    return pl.pallas_call(
        functools.partial(_flash_kernel, variant=variant, tq=tq, tk=tk, lam_init=lam_init),
        grid=(n_groups, S // tq),
        in_specs=in_specs,
        out_specs=pl.BlockSpec((tq, LANES), lambda h, i: (i, h)),
        out_shape=jax.ShapeDtypeStruct((S, n_groups * LANES), BF16),
        scratch_shapes=scratch,
        compiler_params=_cparams(("parallel", "parallel")),
        name=name,
    )(*extra, q, k, vt)


def _layer_norm(x, g, b):
    mu = jnp.mean(x, axis=-1, keepdims=True)
    xc = x - mu
    var = jnp.mean(xc * xc, axis=-1, keepdims=True)
    return xc * lax.rsqrt(var + LN_EPS) * g + b


def _out_ln_kernel(o_ref, w_ref, h_ref, g_ref, b_ref, out_ref):
    y = jnp.dot(o_ref[...], w_ref[...], preferred_element_type=F32)
    out_ref[...] = _layer_norm(ALPHA_DN * h_ref[...] + y, g_ref[...], b_ref[...])


def _out_ln(o, w, h, g, b, name):
    S, K = o.shape
    D = h.shape[1]
    tm = min(LN_TM, S)
    return pl.pallas_call(
        _out_ln_kernel,
        grid=(S // tm,),
        in_specs=[
            pl.BlockSpec((tm, K), lambda i: (i, 0)),
            pl.BlockSpec((K, D), lambda i: (0, 0)),
            pl.BlockSpec((tm, D), lambda i: (i, 0)),
            pl.BlockSpec((1, D), lambda i: (0, 0)),
            pl.BlockSpec((1, D), lambda i: (0, 0)),
        ],
        out_specs=pl.BlockSpec((tm, D), lambda i: (i, 0)),
        out_shape=jax.ShapeDtypeStruct((S, D), F32),
        compiler_params=_cparams(("parallel",)),
        name=name,
    )(o, w, h, g.reshape(1, D), b.reshape(1, D))


def _rms(x, g, eps):
    return x * lax.rsqrt(jnp.mean(x * x, axis=-1, keepdims=True) + eps) * g


def _mla_proj_kernel(x_ref, win_ref, wuq_ref, wuk_ref, wuv_ref, gq_ref, gkv_ref, c_ref, s_ref,
                     q_ref, k_ref, v_ref, *, scale):
    xb = x_ref[...].astype(BF16)
    p1 = jnp.dot(xb, win_ref[...], preferred_element_type=F32)
    c_q = _rms(p1[:, :MLA_Q_RANK], gq_ref[...], RMS_EPS).astype(BF16)
    c_kv = _rms(p1[:, MLA_Q_RANK:MLA_Q_RANK + MLA_KV_RANK], gkv_ref[...], RMS_EPS).astype(BF16)
    c = c_ref[...]
    sg = s_ref[...]
    half = MLA_ROPE // 2
    k_rope = _rope128(p1[:, MLA_Q_RANK + MLA_KV_RANK:], c, sg, half)
    q = jnp.dot(c_q, wuq_ref[...], preferred_element_type=F32)
    k = jnp.dot(c_kv, wuk_ref[...], preferred_element_type=F32)
    for h in range(MLA_HEADS):
        sl = slice(h * LANES, (h + 1) * LANES)
        q_ref[:, sl] = (_rope128(q[:, sl], c, sg, half) * scale).astype(q_ref.dtype)
        k_ref[:, sl] = (k[:, sl] + k_rope).astype(k_ref.dtype)
    v_ref[...] = jnp.dot(c_kv, wuv_ref[...], preferred_element_type=F32).astype(v_ref.dtype)


def _mla_project(x, w_in, w_uq, w_uk, w_uv, gq, gkv, tab_c, tab_s, scale):
    S, D = x.shape
    tm = min(PROJ_TM, S)
    full = lambda a: pl.BlockSpec(a.shape, lambda i: (0,) * a.ndim)
    row = lambda w: pl.BlockSpec((tm, w), lambda i: (i, 0))
    hw = MLA_HEADS * LANES
    return pl.pallas_call(
        functools.partial(_mla_proj_kernel, scale=scale),
        grid=(S // tm,),
        in_specs=[row(D), full(w_in), full(w_uq), full(w_uk), full(w_uv), full(gq), full(gkv), row(LANES), row(LANES)],
        out_specs=[row(hw), row(hw), row(MLA_HEADS * MLA_V)],
        out_shape=[jax.ShapeDtypeStruct((S, hw), BF16), jax.ShapeDtypeStruct((S, hw), BF16),
                   jax.ShapeDtypeStruct((S, MLA_HEADS * MLA_V), BF16)],
        compiler_params=_cparams(("parallel",)),
        name="mla_project",
    )(x, w_in, w_uq, w_uk, w_uv, gq, gkv, tab_c, tab_s)


def _gelu_tanh(x):
    return 0.5 * x * (1.0 + jnp.tanh(math.sqrt(2.0 / math.pi) * (x + 0.044715 * (x * x * x))))


def _compress_kernel(x_ref, pa_ref, pb_ref, w1a_ref, w1b_ref, w2_ref, c_ref, s_ref, o_ref, *, nc, rope):
    x = x_ref[0]
    ha = jnp.dot((x + pa_ref[...]).astype(BF16), w1a_ref[...], preferred_element_type=F32)
    hb = jnp.dot((x + pb_ref[...]).astype(BF16), w1b_ref[...], preferred_element_type=F32)
    hid = ha + pltpu.roll(hb, nc - 1, 0)
    out = jnp.dot(_gelu_tanh(hid).astype(BF16), w2_ref[...], preferred_element_type=F32)
    if rope:
        out = _rope128(out, c_ref[...], s_ref[...], NSA_HEAD_DIM // 2)
    o_ref[0] = out.astype(o_ref.dtype)


def _compress(xc, pos, w1, w2, tab_c, tab_s, rope, name):
    G, nc, cw = xc.shape
    half = NSA_CMP_LEN // 2
    pa = pos[:half].reshape(1, cw)
    pb = pos[half:].reshape(1, cw)
    w1a = w1[:cw].astype(BF16)
    w1b = w1[cw:].astype(BF16)
    w2p = jnp.pad(w2, ((0, 0), (0, LANES - w2.shape[1]))).astype(BF16)
    full = lambda a: pl.BlockSpec(a.shape, lambda g: (0,) * a.ndim)
    return pl.pallas_call(
        functools.partial(_compress_kernel, nc=nc, rope=rope),
        grid=(G,),
        in_specs=[pl.BlockSpec((1, nc, cw), lambda g: (g, 0, 0)), full(pa), full(pb), full(w1a), full(w1b),
                  full(w2p), full(tab_c), full(tab_s)],
        out_specs=pl.BlockSpec((1, nc, LANES), lambda g: (g, 0, 0)),
        out_shape=jax.ShapeDtypeStruct((G, nc, LANES), BF16),
        compiler_params=_cparams(("parallel",)),
        name=name,
    )(xc, pa, pb, w1a, w1b, w2p, tab_c, tab_s)


def _split3(x):
    hi = x.astype(BF16)
    r = x - hi.astype(F32)
    mid = r.astype(BF16)
    lo = (r - mid.astype(F32)).astype(BF16)
    return hi, mid, lo


def _heads_to_rows(o_t, tq):
    R = NSA_HEADS // NSA_GROUPS
    return jnp.concatenate([o_t[:NSA_HEAD_DIM, r * tq:(r + 1) * tq] for r in range(R)], axis=0).T


def _nsa_local_kernel(q_ref, kc_ref, vct_ref, kw0, kw1, kw2, kw3, kw4, vw0, vw1, vw2, vw3, vw4,
                      gc_ref, gw_ref, amat_ref, o_ref, sel_ref, *, tq, nc, n_slc, n_top):
    i = pl.program_id(1)
    R = NSA_HEADS // NSA_GROUPS
    q = jnp.concatenate([q_ref[:, r * LANES:(r + 1) * LANES] for r in range(R)], axis=0)
    cols = R * tq
    qpos = i * tq + lax.broadcasted_iota(I32, (1, cols), 1) % tq

    s_t = _qk(kc_ref[0], q)
    cmp_end = lax.broadcasted_iota(I32, (nc, cols), 0) * NSA_CMP_STRIDE + (NSA_CMP_LEN - 1)
    s_t = jnp.where(cmp_end <= qpos, s_t, NEG)
    e = jnp.exp2(s_t - jnp.max(s_t, axis=0, keepdims=True))
    den = jnp.sum(e, axis=0, keepdims=True)
    p_t = e * jnp.where(qpos >= NSA_CMP_LEN - 1, 1.0 / den, 0.0)
    o_ct = jnp.dot(vct_ref[0], p_t.astype(BF16), preferred_element_type=F32)

    p_g = p_t[:, 0:tq]
    for r in range(1, R):
        p_g = p_g + p_t[:, r * tq:(r + 1) * tq]
    amat = amat_ref[...]
    imp = None
    for part in _split3(p_g):
        t = jnp.dot(amat, part, preferred_element_type=F32)
        imp = t if imp is None else imp + t
    nslp = imp.shape[0]
    jb = lax.broadcasted_iota(I32, (nslp, tq), 0)
    cur = (i * tq + lax.broadcasted_iota(I32, (1, tq), 1)) // NSA_SLC_LEN
    valid = jb <= cur
    forced = valid & ((jb == 0) | (jb >= cur - 1))
    score = jnp.where(forced, NSA_FORCE, jnp.where(valid, imp, -1.0))
    score = jnp.where(jb < n_slc, score, -2.0)
    sel = jnp.zeros((nslp, tq), F32)
    for _ in range(n_top):
        mx = jnp.max(score, axis=0, keepdims=True)
        idx = jnp.min(jnp.where(score == mx, jb, nslp), axis=0, keepdims=True)
        hit = jb == idx
        sel = jnp.where(hit, 1.0, sel)
        score = jnp.where(hit, -jnp.inf, score)
    sel_ref[0] = sel.T.astype(sel_ref.dtype)

    nwin = NSA_WINDOW // tq + 1
    k_w = jnp.concatenate([r[...] for r in (kw0, kw1, kw2, kw3, kw4)], axis=0)
    vt_w = jnp.concatenate([r[...] for r in (vw0, vw1, vw2, vw3, vw4)], axis=1)
    s_t = _qk(k_w, q)
    kpos = (i - (nwin - 1)) * tq + lax.broadcasted_iota(I32, (nwin * tq, cols), 0)
    m_w = (kpos <= qpos) & (kpos > qpos - NSA_WINDOW) & (kpos >= 0)
    s_t = jnp.where(m_w, s_t, NEG)
    e = jnp.exp2(s_t - jnp.max(s_t, axis=0, keepdims=True))
    p_w = e * (1.0 / jnp.sum(e, axis=0, keepdims=True))
    o_wt = jnp.dot(vt_w, p_w.astype(BF16), preferred_element_type=F32)

    o_ref[...] = gc_ref[...] * _heads_to_rows(o_ct, tq) + gw_ref[...] * _heads_to_rows(o_wt, tq)


def _nsa_local(q, k_cmp, vt_cmp, kw, vwt, gc, gw, amat_t, n_slc, n_top):
    S = q.shape[0]
    tq = NSA_TQ
    G = NSA_GROUPS
    nc = k_cmp.shape[1]
    nslp = amat_t.shape[0]
    gw_lanes = (NSA_HEADS // G) * NSA_HEAD_DIM
    nwin = NSA_WINDOW // tq + 1
    assert nwin == 5

    def kwin(t):
        return pl.BlockSpec((tq, LANES), lambda g, i: (jnp.maximum(i - (nwin - 1) + t, 0), g))

    def vwin(t):
        return pl.BlockSpec((LANES, tq), lambda g, i: (g, jnp.maximum(i - (nwin - 1) + t, 0)))

    in_specs = ([pl.BlockSpec((tq, (NSA_HEADS // G) * LANES), lambda g, i: (i, g)),
                 pl.BlockSpec((1, nc, LANES), lambda g, i: (g, 0, 0)),
                 pl.BlockSpec((1, LANES, nc), lambda g, i: (g, 0, 0))]
                + [kwin(t) for t in range(nwin)] + [vwin(t) for t in range(nwin)]
                + [pl.BlockSpec((tq, gw_lanes), lambda g, i: (i, g)),
                   pl.BlockSpec((tq, gw_lanes), lambda g, i: (i, g)),
                   pl.BlockSpec(amat_t.shape, lambda g, i: (0, 0))])
    return pl.pallas_call(
        functools.partial(_nsa_local_kernel, tq=tq, nc=nc, n_slc=n_slc, n_top=n_top),
        grid=(G, S // tq),
        in_specs=in_specs,
        out_specs=[pl.BlockSpec((tq, gw_lanes), lambda g, i: (i, g)),
                   pl.BlockSpec((1, tq, nslp), lambda g, i: (g, i, 0))],
        out_shape=[jax.ShapeDtypeStruct((S, NSA_HEADS * NSA_HEAD_DIM), F32),
                   jax.ShapeDtypeStruct((G, S, nslp), BF16)],
        compiler_params=_cparams(("parallel", "parallel")),
        name="nsa_local",
    )(q, k_cmp, vt_cmp, *([kw] * nwin), *([vwt] * nwin), gc, gw, amat_t)


MASK_BIG = 2.0 ** 100


def _nsa_select_kernel(q_ref, k_ref, vt_ref, sel_ref, kpat_ref, pm_ref, ocw_ref, gs_ref, o_ref, m_ref, acc_ref,
                       *, tq, tk, tiles_per_selblock):
    i = pl.program_id(1)
    R = NSA_HEADS // NSA_GROUPS
    dv = vt_ref.shape[0]
    m_ref[...] = jnp.full(m_ref.shape, NEG, F32)
    acc_ref[...] = jnp.zeros(acc_ref.shape, F32)
    ones = jnp.ones((ONES_ROWS, tk), BF16)

    def tiles(js, masked):
        exts = []
        for j in js:
            sel_blk = sel_ref[0, :, pl.ds(pl.multiple_of((j // tiles_per_selblock) * LANES, LANES), LANES)]
            exts.append(jnp.dot(sel_blk - 1.0, pm_ref[j % tiles_per_selblock], preferred_element_type=F32).astype(BF16))
        work = []
        for j, ext in zip(js, exts):
            start = pl.multiple_of(j * tk, tk)
            k_ext = k_ref[pl.ds(start, tk), :] + kpat_ref[...]
            q_ext = jnp.concatenate([q_ref[:, r * LANES:(r + 1) * LANES] + ext for r in range(R)], axis=0)
            work.append((j, start, _qk(k_ext, q_ext)))
        for j, start, s_t in work:
            vt_ext = jnp.concatenate([vt_ref[:, pl.ds(start, tk)], ones], axis=0)
            if masked:
                kpos = j * tk + lax.broadcasted_iota(I32, s_t.shape, 0)
                qpos = i * tq + lax.broadcasted_iota(I32, s_t.shape, 1) % tq
                s_t = jnp.where(kpos <= qpos, s_t, NEG)
            _flash_update(s_t, vt_ext, m_ref, acc_ref, 0)

    n_full = (i * tq + 1) // tk

    def body(p, carry):
        tiles([p * NSA_KV_UNROLL + u for u in range(NSA_KV_UNROLL)], False)
        return carry

    lax.fori_loop(0, n_full // NSA_KV_UNROLL, body, 0)
    rem = n_full % NSA_KV_UNROLL
    base = n_full - rem

    @pl.when(rem >= 2)
    def _():
        tiles([base, base + 1], False)

    @pl.when(rem % 2 == 1)
    def _():
        tiles([base + rem - 1], False)

    tiles([n_full], True)

    acc = acc_ref[0]
    o_t = acc[:dv] / acc[dv:dv + 1]
    o_ref[...] = (ocw_ref[...] + gs_ref[...] * _heads_to_rows(o_t, tq)).astype(o_ref.dtype)


def _nsa_select(q, ks, vst, sel, kpat, pm, ocw, gs):
    S = q.shape[0]
    tq, tk = min(NSA_SEL_TQ, S), kpat.shape[0]
    G = NSA_GROUPS
    R = NSA_HEADS // G
    gw_lanes = R * NSA_HEAD_DIM
    nslp = sel.shape[2]
    tiles_per_selblock = pm.shape[0]
    return pl.pallas_call(
        functools.partial(_nsa_select_kernel, tq=tq, tk=tk, tiles_per_selblock=tiles_per_selblock),
        grid=(G, S // tq),
        in_specs=[pl.BlockSpec((tq, R * LANES), lambda g, i: (i, g)),
                  pl.BlockSpec((S, LANES), lambda g, i: (0, g)),
                  pl.BlockSpec((LANES, S), lambda g, i: (g, 0)),
                  pl.BlockSpec((1, tq, nslp), lambda g, i: (g, i, 0)),
                  pl.BlockSpec(kpat.shape, lambda g, i: (0, 0)),
                  pl.BlockSpec(pm.shape, lambda g, i: (0, 0, 0)),
                  pl.BlockSpec((tq, gw_lanes), lambda g, i: (i, g)),
                  pl.BlockSpec((tq, gw_lanes), lambda g, i: (i, g))],
        out_specs=pl.BlockSpec((tq, gw_lanes), lambda g, i: (i, g)),
        out_shape=jax.ShapeDtypeStruct((S, NSA_HEADS * NSA_HEAD_DIM), BF16),
        scratch_shapes=[pltpu.VMEM((1, 1, R * tq), F32), pltpu.VMEM((1, LANES + ONES_ROWS, R * tq), F32)],
        compiler_params=_cparams(("parallel", "parallel")),
        name="nsa_select",
    )(q, ks, vst, sel, kpat, pm, ocw, gs)


def _router_kernel(h_ref, w_ref, b_ref, e_ref, g_ref, rank_ref, cnt_ref, carry):
    @pl.when(pl.program_id(0) == 0)
    def _():
        carry[...] = jnp.zeros(carry.shape, F32)

    x = h_ref[...]
    w = w_ref[...]
    x_hi = x.astype(BF16)
    x_lo = (x - x_hi.astype(F32)).astype(BF16)
    w_hi = w.astype(BF16)
    w_lo = (w - w_hi.astype(F32)).astype(BF16)
    logits = (jnp.dot(x_hi, w_hi, preferred_element_type=F32) + jnp.dot(x_hi, w_lo, preferred_element_type=F32)
              + jnp.dot(x_lo, w_hi, preferred_element_type=F32)) + b_ref[...]
    tm = logits.shape[0]
    lane = lax.broadcasted_iota(I32, logits.shape, 1)
    work = jnp.where(lane < N_EXPERTS, logits, -jnp.inf)
    vals, idxs = [], []
    for _ in range(TOP_K):
        mx = jnp.max(work, axis=1, keepdims=True)
        idx = jnp.min(jnp.where(work == mx, lane, LANES), axis=1, keepdims=True)
        vals.append(mx)
        idxs.append(idx)
        work = jnp.where(lane == idx, -jnp.inf, work)
    exps = [jnp.exp(v - vals[0]) for v in vals]
    den = exps[0]
    for t in exps[1:]:
        den = den + t

    chosen = jnp.zeros(logits.shape, F32)
    for kk in range(TOP_K):
        chosen = jnp.where(lane == idxs[kk], 1.0, chosen)
    r_i = lax.broadcasted_iota(I32, (tm, tm), 0)
    c_i = lax.broadcasted_iota(I32, (tm, tm), 1)
    tri = jnp.where(c_i < r_i, 1.0, 0.0).astype(BF16)
    before = carry[...] + jnp.dot(tri, chosen.astype(BF16), preferred_element_type=F32)
    carry[...] = carry[...] + jnp.sum(chosen, axis=0, keepdims=True)
    cnt_ref[...] = jnp.broadcast_to(carry[...], cnt_ref.shape)

    e_out = jnp.zeros(logits.shape, I32)
    g_out = jnp.zeros(logits.shape, F32)
    r_out = jnp.zeros(logits.shape, F32)
    for kk in range(TOP_K):
        e_out = jnp.where(lane == kk, idxs[kk], e_out)
        g_out = jnp.where(lane == kk, exps[kk] / den, g_out)
        rank = jnp.sum(jnp.where(lane == idxs[kk], before, 0.0), axis=1, keepdims=True)
        r_out = jnp.where(lane == kk, rank, r_out)
    e_ref[...] = e_out
    g_ref[...] = g_out
    rank_ref[...] = r_out.astype(I32)


def _router(h, w_router, b_router):
    S, D = h.shape
    tm = min(LN_TM, S)
    w = jnp.pad(w_router, ((0, 0), (0, LANES - N_EXPERTS)))
    b = jnp.pad(b_router, (0, LANES - N_EXPERTS)).reshape(1, LANES)
    return pl.pallas_call(
        _router_kernel,
        grid=(S // tm,),
        in_specs=[pl.BlockSpec((tm, D), lambda i: (i, 0)),
                  pl.BlockSpec((D, LANES), lambda i: (0, 0)),
                  pl.BlockSpec((1, LANES), lambda i: (0, 0))],
        out_specs=[pl.BlockSpec((tm, LANES), lambda i: (i, 0))] * 3 + [pl.BlockSpec((SUB, LANES), lambda i: (0, 0))],
        out_shape=[jax.ShapeDtypeStruct((S, LANES), I32), jax.ShapeDtypeStruct((S, LANES), F32),
                   jax.ShapeDtypeStruct((S, LANES), I32), jax.ShapeDtypeStruct((SUB, LANES), F32)],
        scratch_shapes=[pltpu.VMEM((1, LANES), F32)],
        compiler_params=_cparams(("arbitrary",)),
        name="moe_router",
    )(h, w, b)


def _start_row_copies(idx_ref, idx_base, n, src_ref, dst_ref, dst_base, sem):
    def body(r, carry):
        tok = idx_ref[idx_base + r]
        pltpu.make_async_copy(src_ref.at[pl.ds(pl.multiple_of(tok * SUB, SUB), SUB)],
                              dst_ref.at[pl.ds(pl.multiple_of((dst_base + r) * SUB, SUB), SUB)], sem).start()
        return carry
    lax.fori_loop(0, n, body, 0, unroll=8)


def _wait_row_copies(n, src_ref, dst_ref, dst_base, sem):
    pltpu.make_async_copy(src_ref.at[pl.ds(0, n * SUB)], dst_ref.at[pl.ds(dst_base * SUB, n * SUB)], sem).wait()


def _rows_to_matrix(ref, base, n):
    return jnp.concatenate([ref[pl.ds(base * SUB + s, n, stride=SUB), :] for s in range(SUB)], axis=1)


def _expert_kernel(te_ref, nt_ref, tok_ref, nxt_ref, h_ref, wgu_ref, bgu_ref, wd_ref, bd_ref, pmat_ref, o_ref,
                   xbuf, sems, wg_stage, wd_stage, wsems, wperm, wdown, *, tm, layer):
    t = pl.program_id(0)
    nt = nt_ref[0]
    slot = t % GATHER_SLOTS

    def weight_copies(e):
        return (pltpu.make_async_copy(wgu_ref.at[layer, e], wg_stage, wsems.at[0]),
                pltpu.make_async_copy(wd_ref.at[layer, e], wd_stage, wsems.at[1]))

    @pl.when(t == 0)
    def _():
        for cp in weight_copies(te_ref[0]):
            cp.start()

    @pl.when(t == 0)
    def _():
        for a in range(GATHER_SLOTS - 1):
            @pl.when(a < nt)
            def _():
                _start_row_copies(tok_ref, a * tm, tm, h_ref, xbuf, a * tm, sems.at[a])

    ahead = t + GATHER_SLOTS - 1

    @pl.when(ahead < nt)
    def _():
        aslot = ahead % GATHER_SLOTS
        _start_row_copies(tok_ref, ahead * tm, tm, h_ref, xbuf, aslot * tm, sems.at[aslot])

    changed = jnp.logical_or(t == 0, te_ref[t] != te_ref[jnp.maximum(t - 1, 0)])

    @pl.when(jnp.logical_and(t < nt, changed))
    def _():
        for cp in weight_copies(te_ref[t]):
            cp.wait()
        blk = pmat_ref.shape[0]
        for c in range(wperm.shape[1] // blk):
            sl = slice(c * blk, (c + 1) * blk)
            wperm[:, sl] = jnp.dot(wg_stage[:, sl].astype(BF16), pmat_ref[...], preferred_element_type=F32).astype(BF16)
        wdown[...] = wd_stage[...].astype(BF16)

        @pl.when(nxt_ref[t] != te_ref[t])
        def _():
            for cp in weight_copies(nxt_ref[t]):
                cp.start()

    @pl.when(t < nt)
    def _():
        _wait_row_copies(tm, h_ref, xbuf, slot * tm, sems.at[slot])
        xb = _rows_to_matrix(xbuf, slot * tm, tm).astype(BF16)
        hgu = jnp.dot(xb, wperm[...], preferred_element_type=F32) + bgu_ref[0, 0]
        acts = []
        for c in range(hgu.shape[1] // (2 * LANES)):
            hg = jnp.minimum(hgu[:, 2 * c * LANES:(2 * c + 1) * LANES], SWIGLU_LIMIT)
            hl = jnp.clip(hgu[:, (2 * c + 1) * LANES:(2 * c + 2) * LANES], -SWIGLU_LIMIT, SWIGLU_LIMIT)
            acts.append((hg * jax.nn.sigmoid(SWIGLU_ALPHA * hg) * (hl + 1.0)).astype(BF16))
        act = jnp.concatenate(acts, axis=1)
        y = jnp.dot(act, wdown[...], preferred_element_type=F32) + bd_ref[0, 0]
        for s in range(SUB):
            o_ref[pl.ds(s, tm, stride=SUB), :] = y[:, s * LANES:(s + 1) * LANES]

    @pl.when(t >= nt)
    def _():
        o_ref[...] = jnp.zeros(o_ref.shape, o_ref.dtype)


def _experts(h_rows, row_tok, tile_e, n_tiles_used, next_e, layer, wgu, bgu, wd, bd, pmat):
    n_rows = row_tok.shape[0]
    tm = MOE_TM
    D, F2 = wgu.shape[2], wgu.shape[3]
    row_map = lambda t, te, nt, tok, nxt: (t, 0)
    b_map = lambda t, te, nt, tok, nxt: (layer, te[t], 0, 0)
    return pl.pallas_call(
        functools.partial(_expert_kernel, tm=tm, layer=layer),
        grid_spec=pltpu.PrefetchScalarGridSpec(
            num_scalar_prefetch=4,
            grid=(n_rows // tm,),
            in_specs=[pl.BlockSpec(memory_space=pl.ANY),
                      pl.BlockSpec(memory_space=pl.ANY), pl.BlockSpec((1, 1, 1, F2), b_map),
                      pl.BlockSpec(memory_space=pl.ANY), pl.BlockSpec((1, 1, 1, D), b_map),
                      pl.BlockSpec(pmat.shape, lambda t, te, nt, tok, nxt: (0, 0))],
            out_specs=pl.BlockSpec((tm * SUB, LANES), row_map),
            scratch_shapes=[pltpu.VMEM((GATHER_SLOTS * tm * SUB, LANES), F32), pltpu.SemaphoreType.DMA((GATHER_SLOTS,)),
                            pltpu.VMEM((D, F2), F32), pltpu.VMEM((F2 // 2, D), F32), pltpu.SemaphoreType.DMA((2,)),
                            pltpu.VMEM((D, F2), BF16), pltpu.VMEM((F2 // 2, D), BF16)],
        ),
        out_shape=jax.ShapeDtypeStruct((n_rows * SUB, LANES), F32),
        compiler_params=_cparams(("arbitrary",)),
        name="moe_experts",
    )(tile_e, n_tiles_used, row_tok, next_e, h_rows, wgu, bgu, wd, bd, pmat)


def _combine_ln_kernel(dest_ref, y_ref, gate_ref, h_ref, g_ref, b_ref, out_ref, ybuf, sems, *, tm):
    i = pl.program_id(0)
    n = pl.num_programs(0)
    slot = i % 2
    per = TOP_K * tm

    @pl.when(i == 0)
    def _():
        _start_row_copies(dest_ref, 0, per, y_ref, ybuf, 0, sems.at[0])

    @pl.when(i + 1 < n)
    def _():
        _start_row_copies(dest_ref, (i + 1) * per, per, y_ref, ybuf, (1 - slot) * per, sems.at[1 - slot])

    _wait_row_copies(per, y_ref, ybuf, slot * per, sems.at[slot])
    gate = gate_ref[...]
    pieces = []
    for s in range(SUB):
        acc = None
        for kk in range(TOP_K):
            rows = ybuf[pl.ds((slot * per + kk) * SUB + s, tm, stride=TOP_K * SUB), :]
            term = gate[:, kk:kk + 1] * rows
            acc = term if acc is None else acc + term
        pieces.append(acc)
    y = jnp.concatenate(pieces, axis=1)
    out_ref[...] = _layer_norm(ALPHA_DN * h_ref[...] + y, g_ref[...], b_ref[...])


def _combine_ln(ys_rows, dest, gate, h, g, b):
    S, D = h.shape
    tm = min(COMBINE_TM, S)
    return pl.pallas_call(
        functools.partial(_combine_ln_kernel, tm=tm),
        grid_spec=pltpu.PrefetchScalarGridSpec(
            num_scalar_prefetch=1,
            grid=(S // tm,),
            in_specs=[pl.BlockSpec(memory_space=pl.ANY),
                      pl.BlockSpec((tm, LANES), lambda i, d: (i, 0)),
                      pl.BlockSpec((tm, D), lambda i, d: (i, 0)),
                      pl.BlockSpec((1, D), lambda i, d: (0, 0)),
                      pl.BlockSpec((1, D), lambda i, d: (0, 0))],
            out_specs=pl.BlockSpec((tm, D), lambda i, d: (i, 0)),
            scratch_shapes=[pltpu.VMEM((2 * TOP_K * tm * SUB, LANES), F32), pltpu.SemaphoreType.DMA((2,))],
        ),
        out_shape=jax.ShapeDtypeStruct((S, D), F32),
        compiler_params=_cparams(("arbitrary",)),
        name="moe_combine_ln",
    )(dest, ys_rows, gate, h, g.reshape(1, D), b.reshape(1, D))


def _moe_block(h, w_router, b_router, layer, expert_params, ln_g, ln_b):
    S, D = h.shape
    E, tm = N_EXPERTS, MOE_TM
    e_full, g_full, rank_full, cnt = _router(h, w_router, b_router)
    top_e = e_full[:, :TOP_K]

    counts = cnt[0, :E].astype(I32)
    padded = (counts + tm - 1) // tm * tm
    pends = jnp.cumsum(padded)
    pstarts = pends - padded
    dest = pstarts[top_e] + rank_full[:, :TOP_K]
    A = S * TOP_K
    n_tiles = A // tm + E
    n_rows = n_tiles * tm
    tok = jnp.broadcast_to(jnp.arange(S, dtype=I32)[:, None], (S, TOP_K))
    row_tok = jnp.zeros((n_rows,), I32).at[dest.reshape(A)].set(tok.reshape(A))
    n_used = (pends[-1] // tm).astype(I32)
    tile_start = jnp.arange(n_tiles, dtype=I32) * tm
    tile_e = jnp.minimum((pends[None, :] <= tile_start[:, None]).astype(I32).sum(axis=1), E - 1)
    tile_e = jnp.where(jnp.arange(n_tiles) < n_used, tile_e, tile_e[jnp.maximum(n_used - 1, 0)])
    group_end = pends[tile_e] // tm
    next_e = jnp.where(group_end < n_used, tile_e[jnp.minimum(group_end, n_tiles - 1)], tile_e)

    h_rows = h.reshape(S * SUB, LANES)
    ys = _experts(h_rows, row_tok, tile_e, n_used.reshape(1), next_e, layer, *expert_params)
    return _combine_ln(ys, dest.reshape(A), g_full, h, ln_g, ln_b)


def _expert_params(w_gu, b_gu, w_down, b_down):
    L, E, _, F2 = w_gu.shape
    blk = 2 * LANES
    pm = np.zeros((blk, blk), np.float32)
    for qq in range(LANES):
        pm[2 * qq, qq] = 1.0
        pm[2 * qq + 1, LANES + qq] = 1.0
    bgu = b_gu.reshape(L, E, F2 // blk, LANES, 2).transpose(0, 1, 2, 4, 3).reshape(L, E, 1, F2)
    return w_gu, bgu, w_down, b_down[:, :, None, :], jnp.asarray(pm, BF16)


def _diff_attention(h, w_in, lq1, lk1, lq2, lk2, subln_g, w_out, ln_g, ln_b, tabs, layer_idx):
    dv = DA_HEADS * 2 * DA_HEAD_DIM
    scale = DA_HEAD_DIM ** -0.5 * LOG2E
    segs = (Seg(0, dv, "rope", scale, BF16), Seg(dv, dv, "rope", 1.0, BF16), Seg(2 * dv, dv, "plain", 1.0, BF16))
    q, k, v = _project(h, w_in.astype(BF16), tabs["c64"], tabs["s64"], segs, DA_HEAD_DIM // 2, "da_project")
    lam_init = 0.8 - 0.6 * math.exp(-0.3 * layer_idx)
    lam4 = jnp.stack([lq1, lk1, lq2, lk2]).astype(F32)
    o = _flash("da", q, k, v.T, (lam4, subln_g.reshape(1, -1).astype(F32)), lam_init, "da_attention")
    return _out_ln(o, w_out.astype(BF16), h, ln_g, ln_b, "da_out_ln")


def _mla(h, w_in, q_norm_g, kv_norm_g, w_uq, w_ukv, w_out, ln_g, ln_b, tabs):
    H = MLA_HEADS
    D = h.shape[1]
    qk_dim = MLA_NOPE + MLA_ROPE
    pad_rope = ((0, 0), (MLA_NOPE, LANES - MLA_NOPE - MLA_ROPE))
    w_in_p = jnp.concatenate([w_in[:, :MLA_Q_RANK + MLA_KV_RANK],
                              jnp.pad(w_in[:, MLA_Q_RANK + MLA_KV_RANK:], pad_rope)], axis=1).astype(BF16)
    w_uq_p = jnp.pad(w_uq.reshape(MLA_Q_RANK, H, qk_dim), ((0, 0), (0, 0), (0, LANES - qk_dim)))
    w_uq_p = w_uq_p.reshape(MLA_Q_RANK, H * LANES).astype(BF16)
    w_ukv3 = w_ukv.reshape(MLA_KV_RANK, H, MLA_NOPE + MLA_V)
    w_uk_p = jnp.pad(w_ukv3[:, :, :MLA_NOPE], ((0, 0), (0, 0), (0, LANES - MLA_NOPE)))
    w_uk_p = w_uk_p.reshape(MLA_KV_RANK, H * LANES).astype(BF16)
    w_uv = w_ukv3[:, :, MLA_NOPE:].reshape(MLA_KV_RANK, H * MLA_V).astype(BF16)
    q, k, v = _mla_project(h, w_in_p, w_uq_p, w_uk_p, w_uv, q_norm_g.reshape(1, -1), kv_norm_g.reshape(1, -1),
                           tabs["c32"], tabs["s32"], qk_dim ** -0.5 * LOG2E)
    o = _flash("mla", q, k, v.T, (), 0.0, "mla_attention")
    return _out_ln(o, w_out.astype(BF16), h, ln_g, ln_b, "mla_out_ln")


def _nsa(h, w_in, pos_k, pos_v, ck_w1, ck_w2, cv_w1, cv_w2, w_out, ln_g, ln_b, tabs):
    S, D = h.shape
    H, G, d = NSA_HEADS, NSA_GROUPS, NSA_HEAD_DIM
    R = H // G
    st, Ls = NSA_CMP_STRIDE, NSA_SLC_LEN
    n_cmp = (S - NSA_CMP_LEN) // st + 1
    nc = S // st
    n_slc = S // Ls
    n_top = min(NSA_SLC_TOPK, n_slc)
    nslp = -(-n_slc // LANES) * LANES
    gd = G * d

    def pad_heads(w, n):
        return jnp.pad(w.reshape(D, n, d), ((0, 0), (0, 0), (0, LANES - d))).reshape(D, n * LANES)

    off = np.cumsum([0, H * d] + [gd] * 6)
    wq, wkc, wvc, wks, wvs, wkw, wvw = [w_in[:, off[t]:off[t + 1]] for t in range(7)]
    wgl = w_in[:, off[7]:].reshape(D, H, 3)
    w_a = jnp.concatenate([pad_heads(wq, H), pad_heads(wks, G), pad_heads(wkw, G), pad_heads(wvs, G),
                           pad_heads(wvw, G), wkc, wvc], axis=1).astype(BF16)
    hq, hg = H * LANES, G * LANES
    segs_a = (Seg(0, hq, "rope", d ** -0.5 * LOG2E, BF16), Seg(hq, hg, "rope", 1.0, BF16), Seg(hq + hg, hg, "rope", 1.0, BF16),
              Seg(hq + 2 * hg, hg, "plain", 1.0, BF16), Seg(hq + 3 * hg, hg, "plain", 1.0, BF16),
              Seg(hq + 4 * hg, gd, "plain", 1.0, F32), Seg(hq + 4 * hg + gd, gd, "plain", 1.0, F32))
    q, ks, kw, vs, vw, kc, vc = _project(h, w_a, tabs["c64p"], tabs["s64p"], segs_a, d // 2, "nsa_project")
    w_g = jnp.concatenate([jnp.repeat(wgl[:, :, b], d, axis=1) for b in range(3)], axis=1).astype(BF16)
    segs_g = tuple(Seg(b * H * d, H * d, "sigmoid", 1.0, F32) for b in range(3))
    gc, gs, gw = _project(h, w_g, tabs["c64p"], tabs["s64p"], segs_g, d // 2, "nsa_gates")

    def chunks(t):
        return t.reshape(nc, st, G, d).transpose(2, 0, 1, 3).reshape(G, nc, st * d)

    k_cmp = _compress(chunks(kc), pos_k, ck_w1, ck_w2, tabs["ccmp"], tabs["scmp"], True, "nsa_compress_k")
    v_cmp = _compress(chunks(vc), pos_v, cv_w1, cv_w2, tabs["ccmp"], tabs["scmp"], False, "nsa_compress_v")

    ratio = Ls // st
    amat_t = np.zeros((nslp, nc), np.float32)
    for jj in range(n_slc):
        for mm in range(ratio):
            for nn in range(NSA_CMP_LEN // st):
                c = ratio * jj + mm - nn
                if 0 <= c < n_cmp:
                    amat_t[jj, c] += 1.0
    ocw, sel = _nsa_local(q, k_cmp, v_cmp.transpose(0, 2, 1), kw, vw.T, gc, gw, jnp.asarray(amat_t, BF16), n_slc, n_top)

    tk = min(NSA_TK, S)
    per_tile = tk // Ls
    tiles_per_selblock = LANES // per_tile
    assert per_tile <= LANES - d
    kpat = np.zeros((tk, LANES), np.float32)
    for t in range(tk):
        kpat[t, d + t // Ls] = 1.0
    pm = np.zeros((tiles_per_selblock, LANES, LANES), np.float32)
    for u in range(tiles_per_selblock):
        for bb in range(per_tile):
            pm[u, per_tile * u + bb, d + bb] = MASK_BIG
    o = _nsa_select(q, ks, vs.T, sel, jnp.asarray(kpat, BF16), jnp.asarray(pm, BF16), ocw, gs)
    return _out_ln(o, w_out.astype(BF16), h, ln_g, ln_b, "nsa_out_ln")


def _rope_tables(S):
    pos = jnp.arange(S)
    c64, s64 = _rope_angles(pos, DA_HEAD_DIM)
    c32, s32 = _rope_angles(pos, MLA_ROPE)
    nc = S // NSA_CMP_STRIDE
    cc, sc = _rope_angles(jnp.arange(nc) * NSA_CMP_STRIDE + NSA_CMP_LEN - 1, NSA_HEAD_DIM)
    one = lambda n, w: jnp.ones((n, w), F32)
    zero = lambda n, w: jnp.zeros((n, w), F32)
    cat = lambda *a: jnp.concatenate(a, axis=1)
    return {
        "c64": cat(c64, c64, c64, c64), "s64": cat(-s64, s64, -s64, s64),
        "c64p": cat(c64, c64, one(S, 64)), "s64p": cat(-s64, s64, zero(S, 64)),
        "c32": cat(one(S, 64), c32, c32, one(S, 32)), "s32": cat(zero(S, 64), -s32, s32, zero(S, 32)),
        "ccmp": cat(cc, cc, one(nc, 64)), "scmp": cat(-sc, sc, zero(nc, 64)),
    }


def kernel(x, da_w_in, da_lambda_q1, da_lambda_k1, da_lambda_q2, da_lambda_k2, da_subln, da_w_out, mla_w_in, mla_q_norm, mla_kv_norm, mla_w_uq, mla_w_ukv, mla_w_out, nsa_w_in, nsa_cmp_pos_k, nsa_cmp_pos_v, nsa_cmp_k_w1, nsa_cmp_k_w2, nsa_cmp_v_w1, nsa_cmp_v_w2, nsa_w_out, ln1_g, ln1_b, ln2_g, ln2_b, moe_w_router, moe_b_router, moe_w_gu, moe_b_gu, moe_w_down, moe_b_down):
    B, S, D = x.shape
    assert B == 1 and D == D_MODEL
    tabs = _rope_tables(S)
    expert_params = _expert_params(moe_w_gu, moe_b_gu, moe_w_down, moe_b_down)
    h = x.reshape(S, D)
    for i in range(DEPTH):
        m, j = i % N_MIXERS, i // N_MIXERS
        if m == 0:
            h = _diff_attention(h, da_w_in[j], da_lambda_q1[j], da_lambda_k1[j], da_lambda_q2[j], da_lambda_k2[j],
                                da_subln[j], da_w_out[j], ln1_g[i], ln1_b[i], tabs, i)
        elif m == 1:
            h = _mla(h, mla_w_in[j], mla_q_norm[j], mla_kv_norm[j], mla_w_uq[j], mla_w_ukv[j], mla_w_out[j],
                     ln1_g[i], ln1_b[i], tabs)
        else:
            h = _nsa(h, nsa_w_in[j], nsa_cmp_pos_k[j], nsa_cmp_pos_v[j], nsa_cmp_k_w1[j], nsa_cmp_k_w2[j],
                     nsa_cmp_v_w1[j], nsa_cmp_v_w2[j], nsa_w_out[j], ln1_g[i], ln1_b[i], tabs)
        h = _moe_block(h, moe_w_router[i], moe_b_router[i], i, expert_params, ln2_g[i], ln2_b[i])
    return h.reshape(B, S, D)
```

```python
import collections
import functools
import math

import numpy as np
import jax
import jax.numpy as jnp
from jax import lax
from jax.experimental import pallas as pl
from jax.experimental.pallas import tpu as pltpu

F32 = jnp.float32
BF16 = jnp.bfloat16
I32 = jnp.int32

D_MODEL = 1024
DEPTH = 4
N_MIXERS = 3
ROPE_THETA = 10000.0
LN_EPS = 1e-5
RMS_EPS = 1e-6
NEG = -1e30
ALPHA_DN = (2 * DEPTH) ** 0.25
LOG2E = math.log2(math.e)

DA_HEADS = 8
DA_HEAD_DIM = 64
MLA_HEADS = 16
MLA_NOPE = 64
MLA_ROPE = 32
MLA_V = 64
MLA_Q_RANK = 256
MLA_KV_RANK = 128
NSA_HEADS = 16
NSA_GROUPS = 4
NSA_HEAD_DIM = 64
NSA_CMP_LEN = 32
NSA_CMP_STRIDE = 16
NSA_CMP_HIDDEN = 256
NSA_SLC_LEN = 64
NSA_SLC_TOPK = 16
NSA_WINDOW = 512
NSA_FORCE = 1e9
N_EXPERTS = 32
TOP_K = 4
D_EXPERT = 1024
SWIGLU_LIMIT = 7.0
SWIGLU_ALPHA = 1.702

LANES = 128
SUB = 8
VMEM_LIMIT = 48 * 1024 * 1024

PROJ_TM = 256
ATT_TQ = 512
ATT_TK = 512
KV_UNROLL = 2
NSA_TQ = 128
NSA_SEL_TQ = 256
NSA_TK = 512
NSA_KV_UNROLL = 4
MOE_TM = 512
LN_TM = 512
COMBINE_TM = 256
GATHER_SLOTS = 3


def _cparams(sem):
    return pltpu.CompilerParams(dimension_semantics=sem, vmem_limit_bytes=VMEM_LIMIT)


def _rope_angles(pos, dim):
    inv = ROPE_THETA ** (-jnp.arange(0, dim, 2, dtype=F32) / dim)
    ang = pos.astype(F32)[:, None] * inv[None, :]
    return jnp.cos(ang), jnp.sin(ang)


def _rope128(t, c, sg, half):
    lane = lax.broadcasted_iota(I32, t.shape, 1)
    first = (lane % (2 * half)) < half
    partner = jnp.where(first, pltpu.roll(t, LANES - half, 1), pltpu.roll(t, half, 1))
    return t * c + partner * sg


Seg = collections.namedtuple("Seg", "start width mode scale dtype")


def _proj_kernel(*refs, segs, half):
    x_ref, w_ref, c_ref, s_ref = refs[:4]
    out_refs = refs[4:]
    xb = x_ref[...].astype(BF16)
    for seg, o_ref in zip(segs, out_refs):
        acc = jnp.dot(xb, w_ref[:, seg.start:seg.start + seg.width], preferred_element_type=F32)
        if seg.mode == "rope":
            c = c_ref[...]
            sg = s_ref[...]
            for g in range(seg.width // LANES):
                t = _rope128(acc[:, g * LANES:(g + 1) * LANES], c, sg, half)
                o_ref[:, g * LANES:(g + 1) * LANES] = (t * seg.scale).astype(o_ref.dtype)
        elif seg.mode == "sigmoid":
            o_ref[...] = jax.nn.sigmoid(acc).astype(o_ref.dtype)
        else:
            o_ref[...] = (acc * seg.scale).astype(o_ref.dtype) if seg.scale != 1.0 else acc.astype(o_ref.dtype)


def _project(x, w, tab_c, tab_s, segs, half, name):
    S, K = x.shape
    tm = min(PROJ_TM, S)
    return pl.pallas_call(
        functools.partial(_proj_kernel, segs=segs, half=half),
        grid=(S // tm,),
        in_specs=[
            pl.BlockSpec((tm, K), lambda i: (i, 0)),
            pl.BlockSpec(w.shape, lambda i: (0, 0)),
            pl.BlockSpec((tm, LANES), lambda i: (i, 0)),
            pl.BlockSpec((tm, LANES), lambda i: (i, 0)),
        ],
        out_specs=[pl.BlockSpec((tm, s.width), lambda i: (i, 0)) for s in segs],
        out_shape=[jax.ShapeDtypeStruct((S, s.width), s.dtype) for s in segs],
        compiler_params=_cparams(("parallel",)),
        name=name,
    )(x, w, tab_c, tab_s)


def _qk(q, k):
    return lax.dot_general(q, k, (((1,), (1,)), ((), ())), preferred_element_type=F32)


ONES_ROWS = 16


def _flash_update(s_t, vt_ext, m_ref, acc_ref, st):
    m_old = m_ref[st]
    m_new = jnp.maximum(m_old, jnp.max(s_t, axis=0, keepdims=True))
    alpha = jnp.exp2(m_old - m_new)
    p_t = jnp.exp2(s_t - m_new).astype(BF16)
    acc_ref[st] = alpha * acc_ref[st] + jnp.dot(vt_ext, p_t, preferred_element_type=F32)
    m_ref[st] = m_new


def _flash_kernel(*refs, variant, tq, tk, lam_init):
    if variant == "da":
        lam_ref, g_ref, q_ref, k_ref, vt_ref, o_ref, qa, qb, m_ref, acc_ref = refs
    else:
        q_ref, k_ref, vt_ref, o_ref, m_ref, acc_ref = refs
    i = pl.program_id(1)
    dv = vt_ref.shape[0]

    if variant == "da":
        q = q_ref[...]
        lane = lax.broadcasted_iota(I32, q.shape, 1)
        qa[...] = jnp.where(lane < DA_HEAD_DIM, q, jnp.zeros_like(q))
        qb[...] = jnp.where(lane >= DA_HEAD_DIM, q, jnp.zeros_like(q))
    m_ref[...] = jnp.full(m_ref.shape, NEG, F32)
    acc_ref[...] = jnp.zeros(acc_ref.shape, F32)
    ones = jnp.ones((ONES_ROWS, tk), BF16)

    def scores_of(js):
        out = []
        for j in js:
            start = pl.multiple_of(j * tk, tk)
            if variant == "da":
                k = k_ref[pl.ds(start, tk), :]
                ops = ((k, qa[...]), (k, qb[...]))
            else:
                ops = ((k_ref[pl.ds(start, tk), :LANES], q_ref[:, :LANES]),
                       (k_ref[pl.ds(start, tk), LANES:], q_ref[:, LANES:]))
            out.append([_qk(kk, qq) for kk, qq in ops])
        return out

    def update(js, all_scores, masked):
        for j, scores in zip(js, all_scores):
            start = pl.multiple_of(j * tk, tk)
            vt_ext = jnp.concatenate([vt_ref[:, pl.ds(start, tk)], ones], axis=0)
            for st, s_t in enumerate(scores):
                if masked:
                    kpos = j * tk + lax.broadcasted_iota(I32, s_t.shape, 0)
                    qpos = i * tq + lax.broadcasted_iota(I32, s_t.shape, 1)
                    s_t = jnp.where(kpos <= qpos, s_t, NEG)
                _flash_update(s_t, vt_ext, m_ref, acc_ref, st)

    def tiles(js, masked):
        update(js, scores_of(js), masked)

    n_full = (i * tq + 1) // tk

    def body(p, carry):
        tiles([p * KV_UNROLL + u for u in range(KV_UNROLL)], False)
        return carry

    lax.fori_loop(0, n_full // KV_UNROLL, body, 0)
    for u in range(KV_UNROLL - 1):
        @pl.when(u < n_full % KV_UNROLL)
        def _():
            tiles([n_full - n_full % KV_UNROLL + u], False)
    tiles([n_full + d for d in range(max(1, tq // tk))], True)

    outs = []
    for st in range(2):
        acc = acc_ref[st]
        outs.append((acc[:dv] / acc[dv:dv + 1]).T)
    o1, o2 = outs
    if variant == "da":
        lam4 = lam_ref[...]
        lam = (jnp.exp(jnp.sum(lam4[0:1] * lam4[1:2], axis=1, keepdims=True))
               - jnp.exp(jnp.sum(lam4[2:3] * lam4[3:4], axis=1, keepdims=True)) + lam_init)
        o = o1 - lam * o2
        ms = jnp.mean(o * o, axis=-1, keepdims=True)
        o = o * lax.rsqrt(ms + LN_EPS) * g_ref[...] * (1.0 - lam_init)
    else:
        lane = lax.broadcasted_iota(I32, o1.shape, 1)
        o = jnp.where(lane < MLA_V, o1, o2)
    o_ref[...] = o.astype(o_ref.dtype)


def _flash(variant, q, k, vt, extra, lam_init, name):
    S = q.shape[0]
    tq, tk = min(ATT_TQ, S), min(ATT_TK, S)
    assert tk % tq == 0 or tq % tk == 0
    qw = LANES if variant == "da" else 2 * LANES
    n_groups = q.shape[1] // qw
    in_specs = [
        pl.BlockSpec((tq, qw), lambda h, i: (i, h)),
        pl.BlockSpec((S, qw), lambda h, i: (0, h), pipeline_mode=pl.Buffered(1)),
        pl.BlockSpec((LANES, S), lambda h, i: (h, 0), pipeline_mode=pl.Buffered(1)),
    ]
    scratch = [pltpu.VMEM((2, 1, tq), F32), pltpu.VMEM((2, LANES + ONES_ROWS, tq), F32)]
    if variant == "da":
        in_specs = [pl.BlockSpec(extra[0].shape, lambda h, i: (0, 0)),
                    pl.BlockSpec(extra[1].shape, lambda h, i: (0, 0))] + in_specs
        scratch = [pltpu.VMEM((tq, LANES), BF16)] * 2 + scratch
    return pl.pallas_call(
        functools.partial(_flash_kernel, variant=variant, tq=tq, tk=tk, lam_init=lam_init),
        grid=(n_groups, S // tq),
        in_specs=in_specs,
        out_specs=pl.BlockSpec((tq, LANES), lambda h, i: (i, h)),
        out_shape=jax.ShapeDtypeStruct((S, n_groups * LANES), BF16),
        scratch_shapes=scratch,
        compiler_params=_cparams(("parallel", "parallel")),
        name=name,
    )(*extra, q, k, vt)


def _layer_norm(x, g, b):
    mu = jnp.mean(x, axis=-1, keepdims=True)
    xc = x - mu
    var = jnp.mean(xc * xc, axis=-1, keepdims=True)
    return xc * lax.rsqrt(var + LN_EPS) * g + b


def _out_ln_kernel(o_ref, w_ref, h_ref, g_ref, b_ref, out_ref, rows_ref):
    y = jnp.dot(o_ref[...], w_ref[...], preferred_element_type=F32)
    hn = _layer_norm(ALPHA_DN * h_ref[...] + y, g_ref[...], b_ref[...])
    out_ref[...] = hn
    tm = hn.shape[0]
    for s in range(SUB):
        rows_ref[pl.ds(s, tm, stride=SUB), :] = hn[:, s * LANES:(s + 1) * LANES]


def _out_ln(o, w, h, g, b, name):
    S, K = o.shape
    D = h.shape[1]
    tm = min(LN_TM, S)
    return pl.pallas_call(
        _out_ln_kernel,
        grid=(S // tm,),
        in_specs=[
            pl.BlockSpec((tm, K), lambda i: (i, 0)),
            pl.BlockSpec((K, D), lambda i: (0, 0)),
            pl.BlockSpec((tm, D), lambda i: (i, 0)),
            pl.BlockSpec((1, D), lambda i: (0, 0)),
            pl.BlockSpec((1, D), lambda i: (0, 0)),
        ],
        out_specs=[pl.BlockSpec((tm, D), lambda i: (i, 0)), pl.BlockSpec((tm * SUB, LANES), lambda i: (i, 0))],
        out_shape=[jax.ShapeDtypeStruct((S, D), F32), jax.ShapeDtypeStruct((S * SUB, LANES), F32)],
        compiler_params=_cparams(("parallel",)),
        name=name,
    )(o, w, h, g.reshape(1, D), b.reshape(1, D))


def _rms(x, g, eps):
    return x * lax.rsqrt(jnp.mean(x * x, axis=-1, keepdims=True) + eps) * g


def _mla_proj_kernel(x_ref, win_ref, wuq_ref, wuk_ref, wuv_ref, gq_ref, gkv_ref, c_ref, s_ref,
                     q_ref, k_ref, v_ref, *, scale):
    xb = x_ref[...].astype(BF16)
    p1 = jnp.dot(xb, win_ref[...], preferred_element_type=F32)
    c_q = _rms(p1[:, :MLA_Q_RANK], gq_ref[...], RMS_EPS).astype(BF16)
    c_kv = _rms(p1[:, MLA_Q_RANK:MLA_Q_RANK + MLA_KV_RANK], gkv_ref[...], RMS_EPS).astype(BF16)
    c = c_ref[...]
    sg = s_ref[...]
    half = MLA_ROPE // 2
    k_rope = _rope128(p1[:, MLA_Q_RANK + MLA_KV_RANK:], c, sg, half)
    q = jnp.dot(c_q, wuq_ref[...], preferred_element_type=F32)
    k = jnp.dot(c_kv, wuk_ref[...], preferred_element_type=F32)
    for h in range(MLA_HEADS):
        sl = slice(h * LANES, (h + 1) * LANES)
        q_ref[:, sl] = (_rope128(q[:, sl], c, sg, half) * scale).astype(q_ref.dtype)
        k_ref[:, sl] = (k[:, sl] + k_rope).astype(k_ref.dtype)
    v_ref[...] = jnp.dot(c_kv, wuv_ref[...], preferred_element_type=F32).astype(v_ref.dtype)


def _mla_project(x, w_in, w_uq, w_uk, w_uv, gq, gkv, tab_c, tab_s, scale):
    S, D = x.shape
    tm = min(PROJ_TM, S)
    full = lambda a: pl.BlockSpec(a.shape, lambda i: (0,) * a.ndim)
    row = lambda w: pl.BlockSpec((tm, w), lambda i: (i, 0))
    hw = MLA_HEADS * LANES
    return pl.pallas_call(
        functools.partial(_mla_proj_kernel, scale=scale),
        grid=(S // tm,),
        in_specs=[row(D), full(w_in), full(w_uq), full(w_uk), full(w_uv), full(gq), full(gkv), row(LANES), row(LANES)],
        out_specs=[row(hw), row(hw), row(MLA_HEADS * MLA_V)],
        out_shape=[jax.ShapeDtypeStruct((S, hw), BF16), jax.ShapeDtypeStruct((S, hw), BF16),
                   jax.ShapeDtypeStruct((S, MLA_HEADS * MLA_V), BF16)],
        compiler_params=_cparams(("parallel",)),
        name="mla_project",
    )(x, w_in, w_uq, w_uk, w_uv, gq, gkv, tab_c, tab_s)


def _gelu_tanh(x):
    return 0.5 * x * (1.0 + jnp.tanh(math.sqrt(2.0 / math.pi) * (x + 0.044715 * (x * x * x))))


def _compress_kernel(x_ref, pa_ref, pb_ref, w1a_ref, w1b_ref, w2_ref, c_ref, s_ref, o_ref, *, nc, rope):
    x = x_ref[0]
    ha = jnp.dot((x + pa_ref[...]).astype(BF16), w1a_ref[...], preferred_element_type=F32)
    hb = jnp.dot((x + pb_ref[...]).astype(BF16), w1b_ref[...], preferred_element_type=F32)
    hid = ha + pltpu.roll(hb, nc - 1, 0)
    out = jnp.dot(_gelu_tanh(hid).astype(BF16), w2_ref[...], preferred_element_type=F32)
    if rope:
        out = _rope128(out, c_ref[...], s_ref[...], NSA_HEAD_DIM // 2)
    o_ref[0] = out.astype(o_ref.dtype)


def _compress(xc, pos, w1, w2, tab_c, tab_s, rope, name):
    G, nc, cw = xc.shape
    half = NSA_CMP_LEN // 2
    pa = pos[:half].reshape(1, cw)
    pb = pos[half:].reshape(1, cw)
    w1a = w1[:cw].astype(BF16)
    w1b = w1[cw:].astype(BF16)
    w2p = jnp.pad(w2, ((0, 0), (0, LANES - w2.shape[1]))).astype(BF16)
    full = lambda a: pl.BlockSpec(a.shape, lambda g: (0,) * a.ndim)
    return pl.pallas_call(
        functools.partial(_compress_kernel, nc=nc, rope=rope),
        grid=(G,),
        in_specs=[pl.BlockSpec((1, nc, cw), lambda g: (g, 0, 0)), full(pa), full(pb), full(w1a), full(w1b),
                  full(w2p), full(tab_c), full(tab_s)],
        out_specs=pl.BlockSpec((1, nc, LANES), lambda g: (g, 0, 0)),
        out_shape=jax.ShapeDtypeStruct((G, nc, LANES), BF16),
        compiler_params=_cparams(("parallel",)),
        name=name,
    )(xc, pa, pb, w1a, w1b, w2p, tab_c, tab_s)


def _split3(x):
    hi = x.astype(BF16)
    r = x - hi.astype(F32)
    mid = r.astype(BF16)
    lo = (r - mid.astype(F32)).astype(BF16)
    return hi, mid, lo


def _heads_to_rows(o_t, tq):
    R = NSA_HEADS // NSA_GROUPS
    return jnp.concatenate([o_t[:NSA_HEAD_DIM, r * tq:(r + 1) * tq] for r in range(R)], axis=0).T


def _nsa_local_kernel(q_ref, kc_ref, vct_ref, kw0, kw1, kw2, kw3, kw4, vw0, vw1, vw2, vw3, vw4,
                      gc_ref, gw_ref, amat_ref, o_ref, sel_ref, *, tq, nc, n_slc, n_top):
    i = pl.program_id(1)
    R = NSA_HEADS // NSA_GROUPS
    q = jnp.concatenate([q_ref[:, r * LANES:(r + 1) * LANES] for r in range(R)], axis=0)
    cols = R * tq
    qpos = i * tq + lax.broadcasted_iota(I32, (1, cols), 1) % tq

    s_t = _qk(kc_ref[0], q)
    cmp_end = lax.broadcasted_iota(I32, (nc, cols), 0) * NSA_CMP_STRIDE + (NSA_CMP_LEN - 1)
    s_t = jnp.where(cmp_end <= qpos, s_t, NEG)
    e = jnp.exp2(s_t - jnp.max(s_t, axis=0, keepdims=True))
    den = jnp.sum(e, axis=0, keepdims=True)
    p_t = e * jnp.where(qpos >= NSA_CMP_LEN - 1, 1.0 / den, 0.0)
    o_ct = jnp.dot(vct_ref[0], p_t.astype(BF16), preferred_element_type=F32)

    p_g = p_t[:, 0:tq]
    for r in range(1, R):
        p_g = p_g + p_t[:, r * tq:(r + 1) * tq]
    amat = amat_ref[...]
    imp = None
    for part in _split3(p_g):
        t = jnp.dot(amat, part, preferred_element_type=F32)
        imp = t if imp is None else imp + t
    nslp = imp.shape[0]
    jb = lax.broadcasted_iota(I32, (nslp, tq), 0)
    cur = (i * tq + lax.broadcasted_iota(I32, (1, tq), 1)) // NSA_SLC_LEN
    valid = jb <= cur
    forced = valid & ((jb == 0) | (jb >= cur - 1))
    score = jnp.where(forced, NSA_FORCE, jnp.where(valid, imp, -1.0))
    score = jnp.where(jb < n_slc, score, -2.0)
    sel = jnp.zeros((nslp, tq), F32)
    for _ in range(n_top):
        mx = jnp.max(score, axis=0, keepdims=True)
        idx = jnp.min(jnp.where(score == mx, jb, nslp), axis=0, keepdims=True)
        hit = jb == idx
        sel = jnp.where(hit, 1.0, sel)
        score = jnp.where(hit, -jnp.inf, score)
    sel_ref[0] = sel.T.astype(sel_ref.dtype)

    nwin = NSA_WINDOW // tq + 1
    k_w = jnp.concatenate([r[...] for r in (kw0, kw1, kw2, kw3, kw4)], axis=0)
    vt_w = jnp.concatenate([r[...] for r in (vw0, vw1, vw2, vw3, vw4)], axis=1)
    s_t = _qk(k_w, q)
    kpos = (i - (nwin - 1)) * tq + lax.broadcasted_iota(I32, (nwin * tq, cols), 0)
    m_w = (kpos <= qpos) & (kpos > qpos - NSA_WINDOW) & (kpos >= 0)
    s_t = jnp.where(m_w, s_t, NEG)
    e = jnp.exp2(s_t - jnp.max(s_t, axis=0, keepdims=True))
    p_w = e * (1.0 / jnp.sum(e, axis=0, keepdims=True))
    o_wt = jnp.dot(vt_w, p_w.astype(BF16), preferred_element_type=F32)

    o_ref[...] = gc_ref[...] * _heads_to_rows(o_ct, tq) + gw_ref[...] * _heads_to_rows(o_wt, tq)


def _nsa_local(q, k_cmp, vt_cmp, kw, vwt, gc, gw, amat_t, n_slc, n_top):
    S = q.shape[0]
    tq = NSA_TQ
    G = NSA_GROUPS
    nc = k_cmp.shape[1]
    nslp = amat_t.shape[0]
    gw_lanes = (NSA_HEADS // G) * NSA_HEAD_DIM
    nwin = NSA_WINDOW // tq + 1
    assert nwin == 5

    def kwin(t):
        return pl.BlockSpec((tq, LANES), lambda g, i: (jnp.maximum(i - (nwin - 1) + t, 0), g))

    def vwin(t):
        return pl.BlockSpec((LANES, tq), lambda g, i: (g, jnp.maximum(i - (nwin - 1) + t, 0)))

    in_specs = ([pl.BlockSpec((tq, (NSA_HEADS // G) * LANES), lambda g, i: (i, g)),
                 pl.BlockSpec((1, nc, LANES), lambda g, i: (g, 0, 0)),
                 pl.BlockSpec((1, LANES, nc), lambda g, i: (g, 0, 0))]
                + [kwin(t) for t in range(nwin)] + [vwin(t) for t in range(nwin)]
                + [pl.BlockSpec((tq, gw_lanes), lambda g, i: (i, g)),
                   pl.BlockSpec((tq, gw_lanes), lambda g, i: (i, g)),
                   pl.BlockSpec(amat_t.shape, lambda g, i: (0, 0))])
    return pl.pallas_call(
        functools.partial(_nsa_local_kernel, tq=tq, nc=nc, n_slc=n_slc, n_top=n_top),
        grid=(G, S // tq),
        in_specs=in_specs,
        out_specs=[pl.BlockSpec((tq, gw_lanes), lambda g, i: (i, g)),
                   pl.BlockSpec((1, tq, nslp), lambda g, i: (g, i, 0))],
        out_shape=[jax.ShapeDtypeStruct((S, NSA_HEADS * NSA_HEAD_DIM), F32),
                   jax.ShapeDtypeStruct((G, S, nslp), BF16)],
        compiler_params=_cparams(("parallel", "parallel")),
        name="nsa_local",
    )(q, k_cmp, vt_cmp, *([kw] * nwin), *([vwt] * nwin), gc, gw, amat_t)


MASK_BIG = 2.0 ** 100


def _nsa_select_kernel(q_ref, k_ref, vt_ref, sel_ref, kpat_ref, pm_ref, ocw_ref, gs_ref, o_ref, m_ref, acc_ref,
                       *, tq, tk, tiles_per_selblock):
    i = pl.program_id(1)
    R = NSA_HEADS // NSA_GROUPS
    dv = vt_ref.shape[0]
    m_ref[...] = jnp.full(m_ref.shape, NEG, F32)
    acc_ref[...] = jnp.zeros(acc_ref.shape, F32)
    ones = jnp.ones((ONES_ROWS, tk), BF16)

    def tiles(js, masked):
        exts = []
        for j in js:
            sel_blk = sel_ref[0, :, pl.ds(pl.multiple_of((j // tiles_per_selblock) * LANES, LANES), LANES)]
            exts.append(jnp.dot(sel_blk - 1.0, pm_ref[j % tiles_per_selblock], preferred_element_type=F32).astype(BF16))
        work = []
        for j, ext in zip(js, exts):
            start = pl.multiple_of(j * tk, tk)
            k_ext = k_ref[pl.ds(start, tk), :] + kpat_ref[...]
            q_ext = jnp.concatenate([q_ref[:, r * LANES:(r + 1) * LANES] + ext for r in range(R)], axis=0)
            work.append((j, start, _qk(k_ext, q_ext)))
        for j, start, s_t in work:
            vt_ext = jnp.concatenate([vt_ref[:, pl.ds(start, tk)], ones], axis=0)
            if masked:
                kpos = j * tk + lax.broadcasted_iota(I32, s_t.shape, 0)
                qpos = i * tq + lax.broadcasted_iota(I32, s_t.shape, 1) % tq
                s_t = jnp.where(kpos <= qpos, s_t, NEG)
            _flash_update(s_t, vt_ext, m_ref, acc_ref, 0)

    n_full = (i * tq + 1) // tk

    def body(p, carry):
        tiles([p * NSA_KV_UNROLL + u for u in range(NSA_KV_UNROLL)], False)
        return carry

    lax.fori_loop(0, n_full // NSA_KV_UNROLL, body, 0)
    rem = n_full % NSA_KV_UNROLL
    base = n_full - rem

    @pl.when(rem >= 2)
    def _():
        tiles([base, base + 1], False)

    @pl.when(rem % 2 == 1)
    def _():
        tiles([base + rem - 1], False)

    tiles([n_full], True)

    acc = acc_ref[0]
    o_t = acc[:dv] / acc[dv:dv + 1]
    o_ref[...] = (ocw_ref[...] + gs_ref[...] * _heads_to_rows(o_t, tq)).astype(o_ref.dtype)


def _nsa_select(q, ks, vst, sel, kpat, pm, ocw, gs):
    S = q.shape[0]
    tq, tk = min(NSA_SEL_TQ, S), kpat.shape[0]
    G = NSA_GROUPS
    R = NSA_HEADS // G
    gw_lanes = R * NSA_HEAD_DIM
    nslp = sel.shape[2]
    tiles_per_selblock = pm.shape[0]
    return pl.pallas_call(
        functools.partial(_nsa_select_kernel, tq=tq, tk=tk, tiles_per_selblock=tiles_per_selblock),
        grid=(G, S // tq),
        in_specs=[pl.BlockSpec((tq, R * LANES), lambda g, i: (i, g)),
                  pl.BlockSpec((S, LANES), lambda g, i: (0, g)),
                  pl.BlockSpec((LANES, S), lambda g, i: (g, 0)),
                  pl.BlockSpec((1, tq, nslp), lambda g, i: (g, i, 0)),
                  pl.BlockSpec(kpat.shape, lambda g, i: (0, 0)),
                  pl.BlockSpec(pm.shape, lambda g, i: (0, 0, 0)),
                  pl.BlockSpec((tq, gw_lanes), lambda g, i: (i, g)),
                  pl.BlockSpec((tq, gw_lanes), lambda g, i: (i, g))],
        out_specs=pl.BlockSpec((tq, gw_lanes), lambda g, i: (i, g)),
        out_shape=jax.ShapeDtypeStruct((S, NSA_HEADS * NSA_HEAD_DIM), BF16),
        scratch_shapes=[pltpu.VMEM((1, 1, R * tq), F32), pltpu.VMEM((1, LANES + ONES_ROWS, R * tq), F32)],
        compiler_params=_cparams(("parallel", "parallel")),
        name="nsa_select",
    )(q, ks, vst, sel, kpat, pm, ocw, gs)


def _router_kernel(h_ref, w_ref, b_ref, e_ref, g_ref, rank_ref, cnt_ref, carry):
    @pl.when(pl.program_id(0) == 0)
    def _():
        carry[...] = jnp.zeros(carry.shape, F32)

    x = h_ref[...]
    w = w_ref[...]
    x_hi = x.astype(BF16)
    x_lo = (x - x_hi.astype(F32)).astype(BF16)
    w_hi = w.astype(BF16)
    w_lo = (w - w_hi.astype(F32)).astype(BF16)
    logits = (jnp.dot(x_hi, w_hi, preferred_element_type=F32) + jnp.dot(x_hi, w_lo, preferred_element_type=F32)
              + jnp.dot(x_lo, w_hi, preferred_element_type=F32)) + b_ref[...]
    tm = logits.shape[0]
    lane = lax.broadcasted_iota(I32, logits.shape, 1)
    work = jnp.where(lane < N_EXPERTS, logits, -jnp.inf)
    vals, idxs = [], []
    for _ in range(TOP_K):
        mx = jnp.max(work, axis=1, keepdims=True)
        idx = jnp.min(jnp.where(work == mx, lane, LANES), axis=1, keepdims=True)
        vals.append(mx)
        idxs.append(idx)
        work = jnp.where(lane == idx, -jnp.inf, work)
    exps = [jnp.exp(v - vals[0]) for v in vals]
    den = exps[0]
    for t in exps[1:]:
        den = den + t

    chosen = jnp.zeros(logits.shape, F32)
    for kk in range(TOP_K):
        chosen = jnp.where(lane == idxs[kk], 1.0, chosen)
    r_i = lax.broadcasted_iota(I32, (tm, tm), 0)
    c_i = lax.broadcasted_iota(I32, (tm, tm), 1)
    tri = jnp.where(c_i < r_i, 1.0, 0.0).astype(BF16)
    before = carry[...] + jnp.dot(tri, chosen.astype(BF16), preferred_element_type=F32)
    carry[...] = carry[...] + jnp.sum(chosen, axis=0, keepdims=True)
    cnt_ref[...] = jnp.broadcast_to(carry[...], cnt_ref.shape)

    e_out = jnp.zeros(logits.shape, I32)
    g_out = jnp.zeros(logits.shape, F32)
    r_out = jnp.zeros(logits.shape, F32)
    for kk in range(TOP_K):
        e_out = jnp.where(lane == kk, idxs[kk], e_out)
        g_out = jnp.where(lane == kk, exps[kk] / den, g_out)
        rank = jnp.sum(jnp.where(lane == idxs[kk], before, 0.0), axis=1, keepdims=True)
        r_out = jnp.where(lane == kk, rank, r_out)
    e_ref[...] = e_out
    g_ref[...] = g_out
    rank_ref[...] = r_out.astype(I32)


def _router(h, w_router, b_router):
    S, D = h.shape
    tm = min(LN_TM, S)
    w = jnp.pad(w_router, ((0, 0), (0, LANES - N_EXPERTS)))
    b = jnp.pad(b_router, (0, LANES - N_EXPERTS)).reshape(1, LANES)
    return pl.pallas_call(
        _router_kernel,
        grid=(S // tm,),
        in_specs=[pl.BlockSpec((tm, D), lambda i: (i, 0)),
                  pl.BlockSpec((D, LANES), lambda i: (0, 0)),
                  pl.BlockSpec((1, LANES), lambda i: (0, 0))],
        out_specs=[pl.BlockSpec((tm, LANES), lambda i: (i, 0))] * 3 + [pl.BlockSpec((SUB, LANES), lambda i: (0, 0))],
        out_shape=[jax.ShapeDtypeStruct((S, LANES), I32), jax.ShapeDtypeStruct((S, LANES), F32),
                   jax.ShapeDtypeStruct((S, LANES), I32), jax.ShapeDtypeStruct((SUB, LANES), F32)],
        scratch_shapes=[pltpu.VMEM((1, LANES), F32)],
        compiler_params=_cparams(("arbitrary",)),
        name="moe_router",
    )(h, w, b)


def _start_row_copies(idx_ref, idx_base, n, src_ref, dst_ref, dst_base, sem):
    def body(r, carry):
        tok = idx_ref[idx_base + r]
        pltpu.make_async_copy(src_ref.at[pl.ds(pl.multiple_of(tok * SUB, SUB), SUB)],
                              dst_ref.at[pl.ds(pl.multiple_of((dst_base + r) * SUB, SUB), SUB)], sem).start()
        return carry
    lax.fori_loop(0, n, body, 0, unroll=8)


def _wait_row_copies(n, src_ref, dst_ref, dst_base, sem):
    pltpu.make_async_copy(src_ref.at[pl.ds(0, n * SUB)], dst_ref.at[pl.ds(dst_base * SUB, n * SUB)], sem).wait()


def _rows_to_matrix(ref, base, n):
    return jnp.concatenate([ref[pl.ds(base * SUB + s, n, stride=SUB), :] for s in range(SUB)], axis=1)


def _expert_kernel(te_ref, nt_ref, tok_ref, nxt_ref, h_ref, wgu_ref, bgu_ref, wd_ref, bd_ref, pmat_ref, o_ref,
                   xbuf, sems, wg_stage, wd_stage, wsems, wperm, wdown, *, tm, layer):
    t = pl.program_id(0)
    nt = nt_ref[0]
    slot = t % GATHER_SLOTS

    def weight_copies(e):
        return (pltpu.make_async_copy(wgu_ref.at[layer, e], wg_stage, wsems.at[0]),
                pltpu.make_async_copy(wd_ref.at[layer, e], wd_stage, wsems.at[1]))

    @pl.when(t == 0)
    def _():
        for cp in weight_copies(te_ref[0]):
            cp.start()

    @pl.when(t == 0)
    def _():
        for a in range(GATHER_SLOTS - 1):
            @pl.when(a < nt)
            def _():
                _start_row_copies(tok_ref, a * tm, tm, h_ref, xbuf, a * tm, sems.at[a])

    ahead = t + GATHER_SLOTS - 1

    @pl.when(ahead < nt)
    def _():
        aslot = ahead % GATHER_SLOTS
        _start_row_copies(tok_ref, ahead * tm, tm, h_ref, xbuf, aslot * tm, sems.at[aslot])

    changed = jnp.logical_or(t == 0, te_ref[t] != te_ref[jnp.maximum(t - 1, 0)])

    @pl.when(jnp.logical_and(t < nt, changed))
    def _():
        for cp in weight_copies(te_ref[t]):
            cp.wait()
        blk = pmat_ref.shape[0]
        for c in range(wperm.shape[1] // blk):
            sl = slice(c * blk, (c + 1) * blk)
            wperm[:, sl] = jnp.dot(wg_stage[:, sl].astype(BF16), pmat_ref[...], preferred_element_type=F32).astype(BF16)
        wdown[...] = wd_stage[...].astype(BF16)

        @pl.when(nxt_ref[t] != te_ref[t])
        def _():
            for cp in weight_copies(nxt_ref[t]):
                cp.start()

    @pl.when(t < nt)
    def _():
        _wait_row_copies(tm, h_ref, xbuf, slot * tm, sems.at[slot])
        xb = _rows_to_matrix(xbuf, slot * tm, tm).astype(BF16)
        hgu = jnp.dot(xb, wperm[...], preferred_element_type=F32) + bgu_ref[0, 0]
        acts = []
        for c in range(hgu.shape[1] // (2 * LANES)):
            hg = jnp.minimum(hgu[:, 2 * c * LANES:(2 * c + 1) * LANES], SWIGLU_LIMIT)
            hl = jnp.clip(hgu[:, (2 * c + 1) * LANES:(2 * c + 2) * LANES], -SWIGLU_LIMIT, SWIGLU_LIMIT)
            acts.append((hg * jax.nn.sigmoid(SWIGLU_ALPHA * hg) * (hl + 1.0)).astype(BF16))
        act = jnp.concatenate(acts, axis=1)
        y = jnp.dot(act, wdown[...], preferred_element_type=F32) + bd_ref[0, 0]
        for s in range(SUB):
            o_ref[pl.ds(s, tm, stride=SUB), :] = y[:, s * LANES:(s + 1) * LANES]

    @pl.when(t >= nt)
    def _():
        o_ref[...] = jnp.zeros(o_ref.shape, o_ref.dtype)


def _experts(h_rows, row_tok, tile_e, n_tiles_used, next_e, layer, wgu, bgu, wd, bd, pmat):
    n_rows = row_tok.shape[0]
    tm = MOE_TM
    D, F2 = wgu.shape[2], wgu.shape[3]
    row_map = lambda t, te, nt, tok, nxt: (t, 0)
    b_map = lambda t, te, nt, tok, nxt: (layer, te[t], 0, 0)
    return pl.pallas_call(
        functools.partial(_expert_kernel, tm=tm, layer=layer),
        grid_spec=pltpu.PrefetchScalarGridSpec(
            num_scalar_prefetch=4,
            grid=(n_rows // tm,),
            in_specs=[pl.BlockSpec(memory_space=pl.ANY),
                      pl.BlockSpec(memory_space=pl.ANY), pl.BlockSpec((1, 1, 1, F2), b_map),
                      pl.BlockSpec(memory_space=pl.ANY), pl.BlockSpec((1, 1, 1, D), b_map),
                      pl.BlockSpec(pmat.shape, lambda t, te, nt, tok, nxt: (0, 0))],
            out_specs=pl.BlockSpec((tm * SUB, LANES), row_map),
            scratch_shapes=[pltpu.VMEM((GATHER_SLOTS * tm * SUB, LANES), F32), pltpu.SemaphoreType.DMA((GATHER_SLOTS,)),
                            pltpu.VMEM((D, F2), F32), pltpu.VMEM((F2 // 2, D), F32), pltpu.SemaphoreType.DMA((2,)),
                            pltpu.VMEM((D, F2), BF16), pltpu.VMEM((F2 // 2, D), BF16)],
        ),
        out_shape=jax.ShapeDtypeStruct((n_rows * SUB, LANES), F32),
        compiler_params=_cparams(("arbitrary",)),
        name="moe_experts",
    )(tile_e, n_tiles_used, row_tok, next_e, h_rows, wgu, bgu, wd, bd, pmat)


def _combine_ln_kernel(dest_ref, y_ref, gate_ref, h_ref, g_ref, b_ref, out_ref, ybuf, sems, *, tm):
    i = pl.program_id(0)
    n = pl.num_programs(0)
    slot = i % 2
    per = TOP_K * tm

    @pl.when(i == 0)
    def _():
        _start_row_copies(dest_ref, 0, per, y_ref, ybuf, 0, sems.at[0])

    @pl.when(i + 1 < n)
    def _():
        _start_row_copies(dest_ref, (i + 1) * per, per, y_ref, ybuf, (1 - slot) * per, sems.at[1 - slot])

    _wait_row_copies(per, y_ref, ybuf, slot * per, sems.at[slot])
    gate = gate_ref[...]
    pieces = []
    for s in range(SUB):
        acc = None
        for kk in range(TOP_K):
            rows = ybuf[pl.ds((slot * per + kk) * SUB + s, tm, stride=TOP_K * SUB), :]
            term = gate[:, kk:kk + 1] * rows
            acc = term if acc is None else acc + term
        pieces.append(acc)
    y = jnp.concatenate(pieces, axis=1)
    out_ref[...] = _layer_norm(ALPHA_DN * h_ref[...] + y, g_ref[...], b_ref[...])


def _combine_ln(ys_rows, dest, gate, h, g, b):
    S, D = h.shape
    tm = min(COMBINE_TM, S)
    return pl.pallas_call(
        functools.partial(_combine_ln_kernel, tm=tm),
        grid_spec=pltpu.PrefetchScalarGridSpec(
            num_scalar_prefetch=1,
            grid=(S // tm,),
            in_specs=[pl.BlockSpec(memory_space=pl.ANY),
                      pl.BlockSpec((tm, LANES), lambda i, d: (i, 0)),
                      pl.BlockSpec((tm, D), lambda i, d: (i, 0)),
                      pl.BlockSpec((1, D), lambda i, d: (0, 0)),
                      pl.BlockSpec((1, D), lambda i, d: (0, 0))],
            out_specs=pl.BlockSpec((tm, D), lambda i, d: (i, 0)),
            scratch_shapes=[pltpu.VMEM((2 * TOP_K * tm * SUB, LANES), F32), pltpu.SemaphoreType.DMA((2,))],
        ),
        out_shape=jax.ShapeDtypeStruct((S, D), F32),
        compiler_params=_cparams(("arbitrary",)),
        name="moe_combine_ln",
    )(dest, ys_rows, gate, h, g.reshape(1, D), b.reshape(1, D))


def _moe_block(h, h_rows, w_router, b_router, layer, expert_params, ln_g, ln_b):
    S, D = h.shape
    E, tm = N_EXPERTS, MOE_TM
    e_full, g_full, rank_full, cnt = _router(h, w_router, b_router)
    top_e = e_full[:, :TOP_K]

    counts = cnt[0, :E].astype(I32)
    padded = (counts + tm - 1) // tm * tm
    pends = jnp.cumsum(padded)
    pstarts = pends - padded
    dest = pstarts[top_e] + rank_full[:, :TOP_K]
    A = S * TOP_K
    n_tiles = A // tm + E
    n_rows = n_tiles * tm
    tok = jnp.broadcast_to(jnp.arange(S, dtype=I32)[:, None], (S, TOP_K))
    row_tok = jnp.zeros((n_rows,), I32).at[dest.reshape(A)].set(tok.reshape(A), unique_indices=True)
    n_used = (pends[-1] // tm).astype(I32)
    tile_start = jnp.arange(n_tiles, dtype=I32) * tm
    tile_e = jnp.minimum((pends[None, :] <= tile_start[:, None]).astype(I32).sum(axis=1), E - 1)
    tile_e = jnp.where(jnp.arange(n_tiles) < n_used, tile_e, tile_e[jnp.maximum(n_used - 1, 0)])
    group_end = pends[tile_e] // tm
    next_e = jnp.where(group_end < n_used, tile_e[jnp.minimum(group_end, n_tiles - 1)], tile_e)

    ys = _experts(h_rows, row_tok, tile_e, n_used.reshape(1), next_e, layer, *expert_params)
    return _combine_ln(ys, dest.reshape(A), g_full, h, ln_g, ln_b)


def _expert_params(w_gu, b_gu, w_down, b_down):
    L, E, _, F2 = w_gu.shape
    blk = 2 * LANES
    pm = np.zeros((blk, blk), np.float32)
    for qq in range(LANES):
        pm[2 * qq, qq] = 1.0
        pm[2 * qq + 1, LANES + qq] = 1.0
    bgu = b_gu.reshape(L, E, F2 // blk, LANES, 2).transpose(0, 1, 2, 4, 3).reshape(L, E, 1, F2)
    return w_gu, bgu, w_down, b_down[:, :, None, :], jnp.asarray(pm, BF16)


def _diff_attention(h, w_in, lq1, lk1, lq2, lk2, subln_g, w_out, ln_g, ln_b, tabs, layer_idx):
    dv = DA_HEADS * 2 * DA_HEAD_DIM
    scale = DA_HEAD_DIM ** -0.5 * LOG2E
    segs = (Seg(0, dv, "rope", scale, BF16), Seg(dv, dv, "rope", 1.0, BF16), Seg(2 * dv, dv, "plain", 1.0, BF16))
    q, k, v = _project(h, w_in.astype(BF16), tabs["c64"], tabs["s64"], segs, DA_HEAD_DIM // 2, "da_project")
    lam_init = 0.8 - 0.6 * math.exp(-0.3 * layer_idx)
    lam4 = jnp.stack([lq1, lk1, lq2, lk2]).astype(F32)
    o = _flash("da", q, k, v.T, (lam4, subln_g.reshape(1, -1).astype(F32)), lam_init, "da_attention")
    return _out_ln(o, w_out.astype(BF16), h, ln_g, ln_b, "da_out_ln")


def _mla(h, w_in, q_norm_g, kv_norm_g, w_uq, w_ukv, w_out, ln_g, ln_b, tabs):
    H = MLA_HEADS
    D = h.shape[1]
    qk_dim = MLA_NOPE + MLA_ROPE
    pad_rope = ((0, 0), (MLA_NOPE, LANES - MLA_NOPE - MLA_ROPE))
    w_in_p = jnp.concatenate([w_in[:, :MLA_Q_RANK + MLA_KV_RANK],
                              jnp.pad(w_in[:, MLA_Q_RANK + MLA_KV_RANK:], pad_rope)], axis=1).astype(BF16)
    w_uq_p = jnp.pad(w_uq.reshape(MLA_Q_RANK, H, qk_dim), ((0, 0), (0, 0), (0, LANES - qk_dim)))
    w_uq_p = w_uq_p.reshape(MLA_Q_RANK, H * LANES).astype(BF16)
    w_ukv3 = w_ukv.reshape(MLA_KV_RANK, H, MLA_NOPE + MLA_V)
    w_uk_p = jnp.pad(w_ukv3[:, :, :MLA_NOPE], ((0, 0), (0, 0), (0, LANES - MLA_NOPE)))
    w_uk_p = w_uk_p.reshape(MLA_KV_RANK, H * LANES).astype(BF16)
    w_uv = w_ukv3[:, :, MLA_NOPE:].reshape(MLA_KV_RANK, H * MLA_V).astype(BF16)
    q, k, v = _mla_project(h, w_in_p, w_uq_p, w_uk_p, w_uv, q_norm_g.reshape(1, -1), kv_norm_g.reshape(1, -1),
                           tabs["c32"], tabs["s32"], qk_dim ** -0.5 * LOG2E)
    o = _flash("mla", q, k, v.T, (), 0.0, "mla_attention")
    return _out_ln(o, w_out.astype(BF16), h, ln_g, ln_b, "mla_out_ln")


def _nsa(h, w_in, pos_k, pos_v, ck_w1, ck_w2, cv_w1, cv_w2, w_out, ln_g, ln_b, tabs):
    S, D = h.shape
    H, G, d = NSA_HEADS, NSA_GROUPS, NSA_HEAD_DIM
    R = H // G
    st, Ls = NSA_CMP_STRIDE, NSA_SLC_LEN
    n_cmp = (S - NSA_CMP_LEN) // st + 1
    nc = S // st
    n_slc = S // Ls
    n_top = min(NSA_SLC_TOPK, n_slc)
    nslp = -(-n_slc // LANES) * LANES
    gd = G * d

    def pad_heads(w, n):
        return jnp.pad(w.reshape(D, n, d), ((0, 0), (0, 0), (0, LANES - d))).reshape(D, n * LANES)

    off = np.cumsum([0, H * d] + [gd] * 6)
    wq, wkc, wvc, wks, wvs, wkw, wvw = [w_in[:, off[t]:off[t + 1]] for t in range(7)]
    wgl = w_in[:, off[7]:].reshape(D, H, 3)
    w_a = jnp.concatenate([pad_heads(wq, H), pad_heads(wks, G), pad_heads(wkw, G), pad_heads(wvs, G),
                           pad_heads(wvw, G), wkc, wvc], axis=1).astype(BF16)
    hq, hg = H * LANES, G * LANES
    segs_a = (Seg(0, hq, "rope", d ** -0.5 * LOG2E, BF16), Seg(hq, hg, "rope", 1.0, BF16), Seg(hq + hg, hg, "rope", 1.0, BF16),
              Seg(hq + 2 * hg, hg, "plain", 1.0, BF16), Seg(hq + 3 * hg, hg, "plain", 1.0, BF16),
              Seg(hq + 4 * hg, gd, "plain", 1.0, F32), Seg(hq + 4 * hg + gd, gd, "plain", 1.0, F32))
    q, ks, kw, vs, vw, kc, vc = _project(h, w_a, tabs["c64p"], tabs["s64p"], segs_a, d // 2, "nsa_project")
    w_g = jnp.concatenate([jnp.repeat(wgl[:, :, b], d, axis=1) for b in range(3)], axis=1).astype(BF16)
    segs_g = tuple(Seg(b * H * d, H * d, "sigmoid", 1.0, F32) for b in range(3))
    gc, gs, gw = _project(h, w_g, tabs["c64p"], tabs["s64p"], segs_g, d // 2, "nsa_gates")

    def chunks(t):
        return t.reshape(nc, st, G, d).transpose(2, 0, 1, 3).reshape(G, nc, st * d)

    k_cmp = _compress(chunks(kc), pos_k, ck_w1, ck_w2, tabs["ccmp"], tabs["scmp"], True, "nsa_compress_k")
    v_cmp = _compress(chunks(vc), pos_v, cv_w1, cv_w2, tabs["ccmp"], tabs["scmp"], False, "nsa_compress_v")

    ratio = Ls // st
    amat_t = np.zeros((nslp, nc), np.float32)
    for jj in range(n_slc):
        for mm in range(ratio):
            for nn in range(NSA_CMP_LEN // st):
                c = ratio * jj + mm - nn
                if 0 <= c < n_cmp:
                    amat_t[jj, c] += 1.0
    ocw, sel = _nsa_local(q, k_cmp, v_cmp.transpose(0, 2, 1), kw, vw.T, gc, gw, jnp.asarray(amat_t, BF16), n_slc, n_top)

    tk = min(NSA_TK, S)
    per_tile = tk // Ls
    tiles_per_selblock = LANES // per_tile
    assert per_tile <= LANES - d
    kpat = np.zeros((tk, LANES), np.float32)
    for t in range(tk):
        kpat[t, d + t // Ls] = 1.0
    pm = np.zeros((tiles_per_selblock, LANES, LANES), np.float32)
    for u in range(tiles_per_selblock):
        for bb in range(per_tile):
            pm[u, per_tile * u + bb, d + bb] = MASK_BIG
    o = _nsa_select(q, ks, vs.T, sel, jnp.asarray(kpat, BF16), jnp.asarray(pm, BF16), ocw, gs)
    return _out_ln(o, w_out.astype(BF16), h, ln_g, ln_b, "nsa_out_ln")


def _rope_tables(S):
    pos = jnp.arange(S)
    c64, s64 = _rope_angles(pos, DA_HEAD_DIM)
    c32, s32 = _rope_angles(pos, MLA_ROPE)
    nc = S // NSA_CMP_STRIDE
    cc, sc = _rope_angles(jnp.arange(nc) * NSA_CMP_STRIDE + NSA_CMP_LEN - 1, NSA_HEAD_DIM)
    one = lambda n, w: jnp.ones((n, w), F32)
    zero = lambda n, w: jnp.zeros((n, w), F32)
    cat = lambda *a: jnp.concatenate(a, axis=1)
    return {
        "c64": cat(c64, c64, c64, c64), "s64": cat(-s64, s64, -s64, s64),
        "c64p": cat(c64, c64, one(S, 64)), "s64p": cat(-s64, s64, zero(S, 64)),
        "c32": cat(one(S, 64), c32, c32, one(S, 32)), "s32": cat(zero(S, 64), -s32, s32, zero(S, 32)),
        "ccmp": cat(cc, cc, one(nc, 64)), "scmp": cat(-sc, sc, zero(nc, 64)),
    }


def kernel(x, da_w_in, da_lambda_q1, da_lambda_k1, da_lambda_q2, da_lambda_k2, da_subln, da_w_out, mla_w_in, mla_q_norm, mla_kv_norm, mla_w_uq, mla_w_ukv, mla_w_out, nsa_w_in, nsa_cmp_pos_k, nsa_cmp_pos_v, nsa_cmp_k_w1, nsa_cmp_k_w2, nsa_cmp_v_w1, nsa_cmp_v_w2, nsa_w_out, ln1_g, ln1_b, ln2_g, ln2_b, moe_w_router, moe_b_router, moe_w_gu, moe_b_gu, moe_w_down, moe_b_down):
    B, S, D = x.shape
    assert B == 1 and D == D_MODEL
    tabs = _rope_tables(S)
    expert_params = _expert_params(moe_w_gu, moe_b_gu, moe_w_down, moe_b_down)
    h = x.reshape(S, D)
    for i in range(DEPTH):
        m, j = i % N_MIXERS, i // N_MIXERS
        if m == 0:
            h, h_rows = _diff_attention(h, da_w_in[j], da_lambda_q1[j], da_lambda_k1[j], da_lambda_q2[j],
                                        da_lambda_k2[j], da_subln[j], da_w_out[j], ln1_g[i], ln1_b[i], tabs, i)
        elif m == 1:
            h, h_rows = _mla(h, mla_w_in[j], mla_q_norm[j], mla_kv_norm[j], mla_w_uq[j], mla_w_ukv[j], mla_w_out[j],
                             ln1_g[i], ln1_b[i], tabs)
        else:
            h, h_rows = _nsa(h, nsa_w_in[j], nsa_cmp_pos_k[j], nsa_cmp_pos_v[j], nsa_cmp_k_w1[j], nsa_cmp_k_w2[j],
                             nsa_cmp_v_w1[j], nsa_cmp_v_w2[j], nsa_w_out[j], ln1_g[i], ln1_b[i], tabs)
        h = _moe_block(h, h_rows, moe_w_router[i], moe_b_router[i], i, expert_params, ln2_g[i], ln2_b[i])
    return h.reshape(B, S, D)
```

```python
import collections
import functools
import math

import numpy as np
import jax
import jax.numpy as jnp
from jax import lax
from jax.experimental import pallas as pl
from jax.experimental.pallas import tpu as pltpu

F32 = jnp.float32
BF16 = jnp.bfloat16
I32 = jnp.int32

D_MODEL = 1024
DEPTH = 4
N_MIXERS = 3
ROPE_THETA = 10000.0
LN_EPS = 1e-5
RMS_EPS = 1e-6
NEG = -1e30
ALPHA_DN = (2 * DEPTH) ** 0.25
LOG2E = math.log2(math.e)

DA_HEADS = 8
DA_HEAD_DIM = 64
MLA_HEADS = 16
MLA_NOPE = 64
MLA_ROPE = 32
MLA_V = 64
MLA_Q_RANK = 256
MLA_KV_RANK = 128
NSA_HEADS = 16
NSA_GROUPS = 4
NSA_HEAD_DIM = 64
NSA_CMP_LEN = 32
NSA_CMP_STRIDE = 16
NSA_CMP_HIDDEN = 256
NSA_SLC_LEN = 64
NSA_SLC_TOPK = 16
NSA_WINDOW = 512
NSA_FORCE = 1e9
N_EXPERTS = 32
TOP_K = 4
D_EXPERT = 1024
SWIGLU_LIMIT = 7.0
SWIGLU_ALPHA = 1.702

LANES = 128
SUB = 8
VMEM_LIMIT = 48 * 1024 * 1024

PROJ_TM = 256
ATT_TQ = 512
ATT_TK = 512
KV_UNROLL = 2
NSA_TQ = 128
NSA_SEL_TQ = 256
NSA_TK = 512
NSA_KV_UNROLL = 4
MOE_TM = 512
LN_TM = 512
COMBINE_TM = 256
GATHER_SLOTS = 3


def _cparams(sem):
    return pltpu.CompilerParams(dimension_semantics=sem, vmem_limit_bytes=VMEM_LIMIT)


def _rope_angles(pos, dim):
    inv = ROPE_THETA ** (-jnp.arange(0, dim, 2, dtype=F32) / dim)
    ang = pos.astype(F32)[:, None] * inv[None, :]
    return jnp.cos(ang), jnp.sin(ang)


def _rope128(t, c, sg, half):
    lane = lax.broadcasted_iota(I32, t.shape, 1)
    first = (lane % (2 * half)) < half
    partner = jnp.where(first, pltpu.roll(t, LANES - half, 1), pltpu.roll(t, half, 1))
    return t * c + partner * sg


Seg = collections.namedtuple("Seg", "start width mode scale dtype")


def _proj_kernel(*refs, segs, half):
    x_ref, w_ref, c_ref, s_ref = refs[:4]
    out_refs = refs[4:]
    xb = x_ref[...].astype(BF16)
    for seg, o_ref in zip(segs, out_refs):
        acc = jnp.dot(xb, w_ref[:, seg.start:seg.start + seg.width], preferred_element_type=F32)
        if seg.mode == "rope":
            c = c_ref[...]
            sg = s_ref[...]
            for g in range(seg.width // LANES):
                t = _rope128(acc[:, g * LANES:(g + 1) * LANES], c, sg, half)
                o_ref[:, g * LANES:(g + 1) * LANES] = (t * seg.scale).astype(o_ref.dtype)
        elif seg.mode == "sigmoid":
            o_ref[...] = jax.nn.sigmoid(acc).astype(o_ref.dtype)
        else:
            o_ref[...] = (acc * seg.scale).astype(o_ref.dtype) if seg.scale != 1.0 else acc.astype(o_ref.dtype)


def _project(x, w, tab_c, tab_s, segs, half, name):
    S, K = x.shape
    tm = min(PROJ_TM, S)
    return pl.pallas_call(
        functools.partial(_proj_kernel, segs=segs, half=half),
        grid=(S // tm,),
        in_specs=[
            pl.BlockSpec((tm, K), lambda i: (i, 0)),
            pl.BlockSpec(w.shape, lambda i: (0, 0)),
            pl.BlockSpec((tm, LANES), lambda i: (i, 0)),
            pl.BlockSpec((tm, LANES), lambda i: (i, 0)),
        ],
        out_specs=[pl.BlockSpec((tm, s.width), lambda i: (i, 0)) for s in segs],
        out_shape=[jax.ShapeDtypeStruct((S, s.width), s.dtype) for s in segs],
        compiler_params=_cparams(("parallel",)),
        name=name,
    )(x, w, tab_c, tab_s)


def _qk(q, k):
    return lax.dot_general(q, k, (((1,), (1,)), ((), ())), preferred_element_type=F32)


ONES_ROWS = 16


def _flash_update(s_t, vt_ext, m_ref, acc_ref, st):
    m_old = m_ref[st]
    m_new = jnp.maximum(m_old, jnp.max(s_t, axis=0, keepdims=True))
    alpha = jnp.exp2(m_old - m_new)
    p_t = jnp.exp2(s_t - m_new).astype(BF16)
    acc_ref[st] = alpha * acc_ref[st] + jnp.dot(vt_ext, p_t, preferred_element_type=F32)
    m_ref[st] = m_new


def _flash_kernel(*refs, variant, tq, tk, lam_init):
    if variant == "da":
        lam_ref, g_ref, q_ref, k_ref, vt_ref, o_ref, qa, qb, m_ref, acc_ref = refs
    else:
        q_ref, k_ref, vt_ref, o_ref, m_ref, acc_ref = refs
    i = pl.program_id(1)
    dv = vt_ref.shape[0] if variant == "da" else MLA_V
    v_rows = [(0, dv), (0, dv)] if variant == "da" else [(0, dv), (dv, 2 * dv)]

    if variant == "da":
        q = q_ref[...]
        lane = lax.broadcasted_iota(I32, q.shape, 1)
        qa[...] = jnp.where(lane < DA_HEAD_DIM, q, jnp.zeros_like(q))
        qb[...] = jnp.where(lane >= DA_HEAD_DIM, q, jnp.zeros_like(q))
    m_ref[...] = jnp.full(m_ref.shape, NEG, F32)
    acc_ref[...] = jnp.zeros(acc_ref.shape, F32)
    ones = jnp.ones((ONES_ROWS, tk), BF16)

    def scores_of(js):
        out = []
        for j in js:
            start = pl.multiple_of(j * tk, tk)
            if variant == "da":
                k = k_ref[pl.ds(start, tk), :]
                ops = ((k, qa[...]), (k, qb[...]))
            else:
                ops = ((k_ref[pl.ds(start, tk), :LANES], q_ref[:, :LANES]),
                       (k_ref[pl.ds(start, tk), LANES:], q_ref[:, LANES:]))
            out.append([_qk(kk, qq) for kk, qq in ops])
        return out

    def update(js, all_scores, masked):
        for j, scores in zip(js, all_scores):
            start = pl.multiple_of(j * tk, tk)
            vt_exts = {}
            for st, s_t in enumerate(scores):
                if masked:
                    kpos = j * tk + lax.broadcasted_iota(I32, s_t.shape, 0)
                    qpos = i * tq + lax.broadcasted_iota(I32, s_t.shape, 1)
                    s_t = jnp.where(kpos <= qpos, s_t, NEG)
                lo, hi = v_rows[st]
                if (lo, hi) not in vt_exts:
                    vt_exts[(lo, hi)] = jnp.concatenate([vt_ref[lo:hi, pl.ds(start, tk)], ones], axis=0)
                _flash_update(s_t, vt_exts[(lo, hi)], m_ref, acc_ref, st)

    def tiles(js, masked):
        update(js, scores_of(js), masked)

    n_full = (i * tq + 1) // tk

    def body(p, carry):
        tiles([p * KV_UNROLL + u for u in range(KV_UNROLL)], False)
        return carry

    lax.fori_loop(0, n_full // KV_UNROLL, body, 0)
    for u in range(KV_UNROLL - 1):
        @pl.when(u < n_full % KV_UNROLL)
        def _():
            tiles([n_full - n_full % KV_UNROLL + u], False)
    tiles([n_full + d for d in range(max(1, tq // tk))], True)

    outs_t = []
    for st in range(2):
        acc = acc_ref[st]
        outs_t.append(acc[:dv] / acc[dv:dv + 1])
    if variant == "da":
        o1, o2 = outs_t[0].T, outs_t[1].T
        lam4 = lam_ref[...]
        lam = (jnp.exp(jnp.sum(lam4[0:1] * lam4[1:2], axis=1, keepdims=True))
               - jnp.exp(jnp.sum(lam4[2:3] * lam4[3:4], axis=1, keepdims=True)) + lam_init)
        o = o1 - lam * o2
        ms = jnp.mean(o * o, axis=-1, keepdims=True)
        o = o * lax.rsqrt(ms + LN_EPS) * g_ref[...] * (1.0 - lam_init)
    else:
        o = jnp.concatenate(outs_t, axis=0).T
    o_ref[...] = o.astype(o_ref.dtype)


def _flash(variant, q, k, vt, extra, lam_init, name):
    S = q.shape[0]
    tq, tk = min(ATT_TQ, S), min(ATT_TK, S)
    assert tk % tq == 0 or tq % tk == 0
    qw = LANES if variant == "da" else 2 * LANES
    n_groups = q.shape[1] // qw
    in_specs = [
        pl.BlockSpec((tq, qw), lambda h, i: (i, h)),
        pl.BlockSpec((S, qw), lambda h, i: (0, h), pipeline_mode=pl.Buffered(1)),
        pl.BlockSpec((LANES, S), lambda h, i: (h, 0), pipeline_mode=pl.Buffered(1)),
    ]
    dv = LANES if variant == "da" else MLA_V
    scratch = [pltpu.VMEM((2, 1, tq), F32), pltpu.VMEM((2, dv + ONES_ROWS, tq), F32)]
    if variant == "da":
        in_specs = [pl.BlockSpec(extra[0].shape, lambda h, i: (0, 0)),
                    pl.BlockSpec(extra[1].shape, lambda h, i: (0, 0))] + in_specs
        scratch = [pltpu.VMEM((tq, LANES), BF16)] * 2 + scratch
    return pl.pallas_call(
        functools.partial(_flash_kernel, variant=variant, tq=tq, tk=tk, lam_init=lam_init),
        grid=(n_groups, S // tq),
        in_specs=in_specs,
        out_specs=pl.BlockSpec((tq, LANES), lambda h, i: (i, h)),
        out_shape=jax.ShapeDtypeStruct((S, n_groups * LANES), BF16),
        scratch_shapes=scratch,
        compiler_params=_cparams(("parallel", "parallel")),
        name=name,
    )(*extra, q, k, vt)


def _layer_norm(x, g, b):
    mu = jnp.mean(x, axis=-1, keepdims=True)
    xc = x - mu
    var = jnp.mean(xc * xc, axis=-1, keepdims=True)
    return xc * lax.rsqrt(var + LN_EPS) * g + b


def _out_ln_kernel(o_ref, w_ref, h_ref, g_ref, b_ref, out_ref, rows_ref):
    y = jnp.dot(o_ref[...], w_ref[...], preferred_element_type=F32)
    hn = _layer_norm(ALPHA_DN * h_ref[...] + y, g_ref[...], b_ref[...])
    out_ref[...] = hn
    tm = hn.shape[0]
    for s in range(SUB):
        rows_ref[pl.ds(s, tm, stride=SUB), :] = hn[:, s * LANES:(s + 1) * LANES]


def _out_ln(o, w, h, g, b, name):
    S, K = o.shape
    D = h.shape[1]
    tm = min(LN_TM, S)
    return pl.pallas_call(
        _out_ln_kernel,
        grid=(S // tm,),
        in_specs=[
            pl.BlockSpec((tm, K), lambda i: (i, 0)),
            pl.BlockSpec((K, D), lambda i: (0, 0)),
            pl.BlockSpec((tm, D), lambda i: (i, 0)),
            pl.BlockSpec((1, D), lambda i: (0, 0)),
            pl.BlockSpec((1, D), lambda i: (0, 0)),
        ],
        out_specs=[pl.BlockSpec((tm, D), lambda i: (i, 0)), pl.BlockSpec((tm * SUB, LANES), lambda i: (i, 0))],
        out_shape=[jax.ShapeDtypeStruct((S, D), F32), jax.ShapeDtypeStruct((S * SUB, LANES), F32)],
        compiler_params=_cparams(("parallel",)),
        name=name,
    )(o, w, h, g.reshape(1, D), b.reshape(1, D))


def _rms(x, g, eps):
    return x * lax.rsqrt(jnp.mean(x * x, axis=-1, keepdims=True) + eps) * g


def _mla_proj_kernel(x_ref, win_ref, wuq_ref, wuk_ref, wuv_ref, gq_ref, gkv_ref, c_ref, s_ref,
                     q_ref, k_ref, v_ref, *, scale):
    xb = x_ref[...].astype(BF16)
    p1 = jnp.dot(xb, win_ref[...], preferred_element_type=F32)
    c_q = _rms(p1[:, :MLA_Q_RANK], gq_ref[...], RMS_EPS).astype(BF16)
    c_kv = _rms(p1[:, MLA_Q_RANK:MLA_Q_RANK + MLA_KV_RANK], gkv_ref[...], RMS_EPS).astype(BF16)
    c = c_ref[...]
    sg = s_ref[...]
    half = MLA_ROPE // 2
    k_rope = _rope128(p1[:, MLA_Q_RANK + MLA_KV_RANK:], c, sg, half)
    q = jnp.dot(c_q, wuq_ref[...], preferred_element_type=F32)
    k = jnp.dot(c_kv, wuk_ref[...], preferred_element_type=F32)
    for h in range(MLA_HEADS):
        sl = slice(h * LANES, (h + 1) * LANES)
        q_ref[:, sl] = (_rope128(q[:, sl], c, sg, half) * scale).astype(q_ref.dtype)
        k_ref[:, sl] = (k[:, sl] + k_rope).astype(k_ref.dtype)
    v_ref[...] = jnp.dot(c_kv, wuv_ref[...], preferred_element_type=F32).astype(v_ref.dtype)


def _mla_project(x, w_in, w_uq, w_uk, w_uv, gq, gkv, tab_c, tab_s, scale):
    S, D = x.shape
    tm = min(PROJ_TM, S)
    full = lambda a: pl.BlockSpec(a.shape, lambda i: (0,) * a.ndim)
    row = lambda w: pl.BlockSpec((tm, w), lambda i: (i, 0))
    hw = MLA_HEADS * LANES
    return pl.pallas_call(
        functools.partial(_mla_proj_kernel, scale=scale),
        grid=(S // tm,),
        in_specs=[row(D), full(w_in), full(w_uq), full(w_uk), full(w_uv), full(gq), full(gkv), row(LANES), row(LANES)],
        out_specs=[row(hw), row(hw), row(MLA_HEADS * MLA_V)],
        out_shape=[jax.ShapeDtypeStruct((S, hw), BF16), jax.ShapeDtypeStruct((S, hw), BF16),
                   jax.ShapeDtypeStruct((S, MLA_HEADS * MLA_V), BF16)],
        compiler_params=_cparams(("parallel",)),
        name="mla_project",
    )(x, w_in, w_uq, w_uk, w_uv, gq, gkv, tab_c, tab_s)


def _gelu_tanh(x):
    return 0.5 * x * (1.0 + jnp.tanh(math.sqrt(2.0 / math.pi) * (x + 0.044715 * (x * x * x))))


def _compress_kernel(x_ref, pa_ref, pb_ref, w1a_ref, w1b_ref, w2_ref, c_ref, s_ref, o_ref, *, nc, rope):
    x = x_ref[0]
    ha = jnp.dot((x + pa_ref[...]).astype(BF16), w1a_ref[...], preferred_element_type=F32)
    hb = jnp.dot((x + pb_ref[...]).astype(BF16), w1b_ref[...], preferred_element_type=F32)
    hid = ha + pltpu.roll(hb, nc - 1, 0)
    out = jnp.dot(_gelu_tanh(hid).astype(BF16), w2_ref[...], preferred_element_type=F32)
    if rope:
        out = _rope128(out, c_ref[...], s_ref[...], NSA_HEAD_DIM // 2)
    o_ref[0] = out.astype(o_ref.dtype)


def _compress(xc, pos, w1, w2, tab_c, tab_s, rope, name):
    G, nc, cw = xc.shape
    half = NSA_CMP_LEN // 2
    pa = pos[:half].reshape(1, cw)
    pb = pos[half:].reshape(1, cw)
    w1a = w1[:cw].astype(BF16)
    w1b = w1[cw:].astype(BF16)
    w2p = jnp.pad(w2, ((0, 0), (0, LANES - w2.shape[1]))).astype(BF16)
    full = lambda a: pl.BlockSpec(a.shape, lambda g: (0,) * a.ndim)
    return pl.pallas_call(
        functools.partial(_compress_kernel, nc=nc, rope=rope),
        grid=(G,),
        in_specs=[pl.BlockSpec((1, nc, cw), lambda g: (g, 0, 0)), full(pa), full(pb), full(w1a), full(w1b),
                  full(w2p), full(tab_c), full(tab_s)],
        out_specs=pl.BlockSpec((1, nc, LANES), lambda g: (g, 0, 0)),
        out_shape=jax.ShapeDtypeStruct((G, nc, LANES), BF16),
        compiler_params=_cparams(("parallel",)),
        name=name,
    )(xc, pa, pb, w1a, w1b, w2p, tab_c, tab_s)


def _split3(x):
    hi = x.astype(BF16)
    r = x - hi.astype(F32)
    mid = r.astype(BF16)
    lo = (r - mid.astype(F32)).astype(BF16)
    return hi, mid, lo


def _heads_to_rows(o_t, tq):
    R = NSA_HEADS // NSA_GROUPS
    return jnp.concatenate([o_t[:NSA_HEAD_DIM, r * tq:(r + 1) * tq] for r in range(R)], axis=0).T


def _nsa_local_kernel(q_ref, kc_ref, vct_ref, kw0, kw1, kw2, kw3, kw4, vw0, vw1, vw2, vw3, vw4,
                      gc_ref, gw_ref, amat_ref, o_ref, sel_ref, *, tq, nc, n_slc, n_top):
    i = pl.program_id(1)
    R = NSA_HEADS // NSA_GROUPS
    q = jnp.concatenate([q_ref[:, r * LANES:(r + 1) * LANES] for r in range(R)], axis=0)
    cols = R * tq
    qpos = i * tq + lax.broadcasted_iota(I32, (1, cols), 1) % tq

    s_t = _qk(kc_ref[0], q)
    cmp_end = lax.broadcasted_iota(I32, (nc, cols), 0) * NSA_CMP_STRIDE + (NSA_CMP_LEN - 1)
    s_t = jnp.where(cmp_end <= qpos, s_t, NEG)
    e = jnp.exp2(s_t - jnp.max(s_t, axis=0, keepdims=True))
    den = jnp.sum(e, axis=0, keepdims=True)
    p_t = e * jnp.where(qpos >= NSA_CMP_LEN - 1, 1.0 / den, 0.0)
    o_ct = jnp.dot(vct_ref[0], p_t.astype(BF16), preferred_element_type=F32)

    p_g = p_t[:, 0:tq]
    for r in range(1, R):
        p_g = p_g + p_t[:, r * tq:(r + 1) * tq]
    amat = amat_ref[...]
    imp = None
    for part in _split3(p_g):
        t = jnp.dot(amat, part, preferred_element_type=F32)
        imp = t if imp is None else imp + t
    nslp = imp.shape[0]
    jb = lax.broadcasted_iota(I32, (nslp, tq), 0)
    cur = (i * tq + lax.broadcasted_iota(I32, (1, tq), 1)) // NSA_SLC_LEN
    valid = jb <= cur
    forced = valid & ((jb == 0) | (jb >= cur - 1))
    score = jnp.where(forced, NSA_FORCE, jnp.where(valid, imp, -1.0))
    score = jnp.where(jb < n_slc, score, -2.0)
    sel = jnp.zeros((nslp, tq), F32)
    for _ in range(n_top):
        mx = jnp.max(score, axis=0, keepdims=True)
        idx = jnp.min(jnp.where(score == mx, jb, nslp), axis=0, keepdims=True)
        hit = jb == idx
        sel = jnp.where(hit, 1.0, sel)
        score = jnp.where(hit, -jnp.inf, score)
    sel_ref[0] = sel.T.astype(sel_ref.dtype)

    nwin = NSA_WINDOW // tq + 1
    k_w = jnp.concatenate([r[...] for r in (kw0, kw1, kw2, kw3, kw4)], axis=0)
    vt_w = jnp.concatenate([r[...] for r in (vw0, vw1, vw2, vw3, vw4)], axis=1)
    s_t = _qk(k_w, q)
    kpos = (i - (nwin - 1)) * tq + lax.broadcasted_iota(I32, (nwin * tq, cols), 0)
    m_w = (kpos <= qpos) & (kpos > qpos - NSA_WINDOW) & (kpos >= 0)
    s_t = jnp.where(m_w, s_t, NEG)
    e = jnp.exp2(s_t - jnp.max(s_t, axis=0, keepdims=True))
    p_w = e * (1.0 / jnp.sum(e, axis=0, keepdims=True))
    o_wt = jnp.dot(vt_w, p_w.astype(BF16), preferred_element_type=F32)

    o_ref[...] = gc_ref[...] * _heads_to_rows(o_ct, tq) + gw_ref[...] * _heads_to_rows(o_wt, tq)


def _nsa_local(q, k_cmp, vt_cmp, kw, vwt, gc, gw, amat_t, n_slc, n_top):
    S = q.shape[0]
    tq = NSA_TQ
    G = NSA_GROUPS
    nc = k_cmp.shape[1]
    nslp = amat_t.shape[0]
    gw_lanes = (NSA_HEADS // G) * NSA_HEAD_DIM
    nwin = NSA_WINDOW // tq + 1
    assert nwin == 5

    def kwin(t):
        return pl.BlockSpec((tq, LANES), lambda g, i: (jnp.maximum(i - (nwin - 1) + t, 0), g))

    def vwin(t):
        return pl.BlockSpec((NSA_HEAD_DIM, tq), lambda g, i: (g, jnp.maximum(i - (nwin - 1) + t, 0)))

    in_specs = ([pl.BlockSpec((tq, (NSA_HEADS // G) * LANES), lambda g, i: (i, g)),
                 pl.BlockSpec((1, nc, LANES), lambda g, i: (g, 0, 0)),
                 pl.BlockSpec((1, NSA_HEAD_DIM, nc), lambda g, i: (g, 0, 0))]
                + [kwin(t) for t in range(nwin)] + [vwin(t) for t in range(nwin)]
                + [pl.BlockSpec((tq, gw_lanes), lambda g, i: (i, g)),
                   pl.BlockSpec((tq, gw_lanes), lambda g, i: (i, g)),
                   pl.BlockSpec(amat_t.shape, lambda g, i: (0, 0))])
    return pl.pallas_call(
        functools.partial(_nsa_local_kernel, tq=tq, nc=nc, n_slc=n_slc, n_top=n_top),
        grid=(G, S // tq),
        in_specs=in_specs,
        out_specs=[pl.BlockSpec((tq, gw_lanes), lambda g, i: (i, g)),
                   pl.BlockSpec((1, tq, nslp), lambda g, i: (g, i, 0))],
        out_shape=[jax.ShapeDtypeStruct((S, NSA_HEADS * NSA_HEAD_DIM), F32),
                   jax.ShapeDtypeStruct((G, S, nslp), BF16)],
        compiler_params=_cparams(("parallel", "parallel")),
        name="nsa_local",
    )(q, k_cmp, vt_cmp, *([kw] * nwin), *([vwt] * nwin), gc, gw, amat_t)


MASK_BIG = 2.0 ** 100


def _nsa_select_kernel(q_ref, k_ref, vt_ref, sel_ref, kpat_ref, pm_ref, ocw_ref, gs_ref, o_ref, m_ref, acc_ref,
                       *, tq, tk, tiles_per_selblock):
    i = pl.program_id(1)
    R = NSA_HEADS // NSA_GROUPS
    dv = vt_ref.shape[0]
    m_ref[...] = jnp.full(m_ref.shape, NEG, F32)
    acc_ref[...] = jnp.zeros(acc_ref.shape, F32)
    ones = jnp.ones((ONES_ROWS, tk), BF16)

    def tiles(js, masked):
        exts = []
        for j in js:
            sel_blk = sel_ref[0, :, pl.ds(pl.multiple_of((j // tiles_per_selblock) * LANES, LANES), LANES)]
            exts.append(jnp.dot(sel_blk - 1.0, pm_ref[j % tiles_per_selblock], preferred_element_type=F32).astype(BF16))
        work = []
        for j, ext in zip(js, exts):
            start = pl.multiple_of(j * tk, tk)
            k_ext = k_ref[pl.ds(start, tk), :] + kpat_ref[...]
            q_ext = jnp.concatenate([q_ref[:, r * LANES:(r + 1) * LANES] + ext for r in range(R)], axis=0)
            work.append((j, start, _qk(k_ext, q_ext)))
        for j, start, s_t in work:
            vt_ext = jnp.concatenate([vt_ref[:, pl.ds(start, tk)], ones], axis=0)
            if masked:
                kpos = j * tk + lax.broadcasted_iota(I32, s_t.shape, 0)
                qpos = i * tq + lax.broadcasted_iota(I32, s_t.shape, 1) % tq
                s_t = jnp.where(kpos <= qpos, s_t, NEG)
            _flash_update(s_t, vt_ext, m_ref, acc_ref, 0)

    n_full = (i * tq + 1) // tk

    def body(p, carry):
        tiles([p * NSA_KV_UNROLL + u for u in range(NSA_KV_UNROLL)], False)
        return carry

    lax.fori_loop(0, n_full // NSA_KV_UNROLL, body, 0)
    rem = n_full % NSA_KV_UNROLL
    base = n_full - rem

    @pl.when(rem >= 2)
    def _():
        tiles([base, base + 1], False)

    @pl.when(rem % 2 == 1)
    def _():
        tiles([base + rem - 1], False)

    tiles([n_full], True)

    acc = acc_ref[0]
    o_t = acc[:dv] / acc[dv:dv + 1]
    o_ref[...] = (ocw_ref[...] + gs_ref[...] * _heads_to_rows(o_t, tq)).astype(o_ref.dtype)


def _nsa_select(q, ks, vst, sel, kpat, pm, ocw, gs):
    S = q.shape[0]
    tq, tk = min(NSA_SEL_TQ, S), kpat.shape[0]
    G = NSA_GROUPS
    R = NSA_HEADS // G
    gw_lanes = R * NSA_HEAD_DIM
    nslp = sel.shape[2]
    tiles_per_selblock = pm.shape[0]
    return pl.pallas_call(
        functools.partial(_nsa_select_kernel, tq=tq, tk=tk, tiles_per_selblock=tiles_per_selblock),
        grid=(G, S // tq),
        in_specs=[pl.BlockSpec((tq, R * LANES), lambda g, i: (i, g)),
                  pl.BlockSpec((S, LANES), lambda g, i: (0, g)),
                  pl.BlockSpec((NSA_HEAD_DIM, S), lambda g, i: (g, 0)),
                  pl.BlockSpec((1, tq, nslp), lambda g, i: (g, i, 0)),
                  pl.BlockSpec(kpat.shape, lambda g, i: (0, 0)),
                  pl.BlockSpec(pm.shape, lambda g, i: (0, 0, 0)),
                  pl.BlockSpec((tq, gw_lanes), lambda g, i: (i, g)),
                  pl.BlockSpec((tq, gw_lanes), lambda g, i: (i, g))],
        out_specs=pl.BlockSpec((tq, gw_lanes), lambda g, i: (i, g)),
        out_shape=jax.ShapeDtypeStruct((S, NSA_HEADS * NSA_HEAD_DIM), BF16),
        scratch_shapes=[pltpu.VMEM((1, 1, R * tq), F32), pltpu.VMEM((1, NSA_HEAD_DIM + ONES_ROWS, R * tq), F32)],
        compiler_params=_cparams(("parallel", "parallel")),
        name="nsa_select",
    )(q, ks, vst, sel, kpat, pm, ocw, gs)


def _router_kernel(h_ref, w_ref, b_ref, e_ref, g_ref, rank_ref, cnt_ref, carry):
    @pl.when(pl.program_id(0) == 0)
    def _():
        carry[...] = jnp.zeros(carry.shape, F32)

    x = h_ref[...]
    w = w_ref[...]
    x_hi = x.astype(BF16)
    x_lo = (x - x_hi.astype(F32)).astype(BF16)
    w_hi = w.astype(BF16)
    w_lo = (w - w_hi.astype(F32)).astype(BF16)
    logits = (jnp.dot(x_hi, w_hi, preferred_element_type=F32) + jnp.dot(x_hi, w_lo, preferred_element_type=F32)
              + jnp.dot(x_lo, w_hi, preferred_element_type=F32)) + b_ref[...]
    tm = logits.shape[0]
    lane = lax.broadcasted_iota(I32, logits.shape, 1)
    work = jnp.where(lane < N_EXPERTS, logits, -jnp.inf)
    vals, idxs = [], []
    for _ in range(TOP_K):
        mx = jnp.max(work, axis=1, keepdims=True)
        idx = jnp.min(jnp.where(work == mx, lane, LANES), axis=1, keepdims=True)
        vals.append(mx)
        idxs.append(idx)
        work = jnp.where(lane == idx, -jnp.inf, work)
    exps = [jnp.exp(v - vals[0]) for v in vals]
    den = exps[0]
    for t in exps[1:]:
        den = den + t

    chosen = jnp.zeros(logits.shape, F32)
    for kk in range(TOP_K):
        chosen = jnp.where(lane == idxs[kk], 1.0, chosen)
    r_i = lax.broadcasted_iota(I32, (tm, tm), 0)
    c_i = lax.broadcasted_iota(I32, (tm, tm), 1)
    tri = jnp.where(c_i < r_i, 1.0, 0.0).astype(BF16)
    before = carry[...] + jnp.dot(tri, chosen.astype(BF16), preferred_element_type=F32)
    carry[...] = carry[...] + jnp.sum(chosen, axis=0, keepdims=True)
    cnt_ref[...] = jnp.broadcast_to(carry[...], cnt_ref.shape)

    e_out = jnp.zeros(logits.shape, I32)
    g_out = jnp.zeros(logits.shape, F32)
    r_out = jnp.zeros(logits.shape, F32)
    for kk in range(TOP_K):
        e_out = jnp.where(lane == kk, idxs[kk], e_out)
        g_out = jnp.where(lane == kk, exps[kk] / den, g_out)
        rank = jnp.sum(jnp.where(lane == idxs[kk], before, 0.0), axis=1, keepdims=True)
        r_out = jnp.where(lane == kk, rank, r_out)
    e_ref[...] = e_out
    g_ref[...] = g_out
    rank_ref[...] = r_out.astype(I32)


def _router(h, w_router, b_router):
    S, D = h.shape
    tm = min(LN_TM, S)
    w = jnp.pad(w_router, ((0, 0), (0, LANES - N_EXPERTS)))
    b = jnp.pad(b_router, (0, LANES - N_EXPERTS)).reshape(1, LANES)
    return pl.pallas_call(
        _router_kernel,
        grid=(S // tm,),
        in_specs=[pl.BlockSpec((tm, D), lambda i: (i, 0)),
                  pl.BlockSpec((D, LANES), lambda i: (0, 0)),
                  pl.BlockSpec((1, LANES), lambda i: (0, 0))],
        out_specs=[pl.BlockSpec((tm, LANES), lambda i: (i, 0))] * 3 + [pl.BlockSpec((SUB, LANES), lambda i: (0, 0))],
        out_shape=[jax.ShapeDtypeStruct((S, LANES), I32), jax.ShapeDtypeStruct((S, LANES), F32),
                   jax.ShapeDtypeStruct((S, LANES), I32), jax.ShapeDtypeStruct((SUB, LANES), F32)],
        scratch_shapes=[pltpu.VMEM((1, LANES), F32)],
        compiler_params=_cparams(("arbitrary",)),
        name="moe_router",
    )(h, w, b)


def _start_row_copies(idx_ref, idx_base, n, src_ref, dst_ref, dst_base, sem):
    def body(r, carry):
        tok = idx_ref[idx_base + r]
        pltpu.make_async_copy(src_ref.at[pl.ds(pl.multiple_of(tok * SUB, SUB), SUB)],
                              dst_ref.at[pl.ds(pl.multiple_of((dst_base + r) * SUB, SUB), SUB)], sem).start()
        return carry
    lax.fori_loop(0, n, body, 0, unroll=8)


def _wait_row_copies(n, src_ref, dst_ref, dst_base, sem):
    pltpu.make_async_copy(src_ref.at[pl.ds(0, n * SUB)], dst_ref.at[pl.ds(dst_base * SUB, n * SUB)], sem).wait()


def _rows_to_matrix(ref, base, n):
    return jnp.concatenate([ref[pl.ds(base * SUB + s, n, stride=SUB), :] for s in range(SUB)], axis=1)


def _expert_kernel(te_ref, nt_ref, tok_ref, nxt_ref, h_ref, wgu_ref, bgu_ref, wd_ref, bd_ref, pmat_ref, o_ref,
                   xbuf, sems, wg_stage, wd_stage, wsems, wperm, wdown, *, tm, layer):
    t = pl.program_id(0)
    nt = nt_ref[0]
    slot = t % GATHER_SLOTS

    def weight_copies(e):
        return (pltpu.make_async_copy(wgu_ref.at[layer, e], wg_stage, wsems.at[0]),
                pltpu.make_async_copy(wd_ref.at[layer, e], wd_stage, wsems.at[1]))

    @pl.when(t == 0)
    def _():
        for cp in weight_copies(te_ref[0]):
            cp.start()

    @pl.when(t == 0)
    def _():
        for a in range(GATHER_SLOTS - 1):
            @pl.when(a < nt)
            def _():
                _start_row_copies(tok_ref, a * tm, tm, h_ref, xbuf, a * tm, sems.at[a])

    ahead = t + GATHER_SLOTS - 1

    @pl.when(ahead < nt)
    def _():
        aslot = ahead % GATHER_SLOTS
        _start_row_copies(tok_ref, ahead * tm, tm, h_ref, xbuf, aslot * tm, sems.at[aslot])

    changed = jnp.logical_or(t == 0, te_ref[t] != te_ref[jnp.maximum(t - 1, 0)])

    @pl.when(jnp.logical_and(t < nt, changed))
    def _():
        for cp in weight_copies(te_ref[t]):
            cp.wait()
        blk = pmat_ref.shape[0]
        for c in range(wperm.shape[1] // blk):
            sl = slice(c * blk, (c + 1) * blk)
            wperm[:, sl] = jnp.dot(wg_stage[:, sl].astype(BF16), pmat_ref[...], preferred_element_type=F32).astype(BF16)
        wdown[...] = wd_stage[...].astype(BF16)

        @pl.when(nxt_ref[t] != te_ref[t])
        def _():
            for cp in weight_copies(nxt_ref[t]):
                cp.start()

    @pl.when(t < nt)
    def _():
        _wait_row_copies(tm, h_ref, xbuf, slot * tm, sems.at[slot])
        xb = _rows_to_matrix(xbuf, slot * tm, tm).astype(BF16)
        hgu = jnp.dot(xb, wperm[...], preferred_element_type=F32) + bgu_ref[0, 0]
        acts = []
        for c in range(hgu.shape[1] // (2 * LANES)):
            hg = jnp.minimum(hgu[:, 2 * c * LANES:(2 * c + 1) * LANES], SWIGLU_LIMIT)
            hl = jnp.clip(hgu[:, (2 * c + 1) * LANES:(2 * c + 2) * LANES], -SWIGLU_LIMIT, SWIGLU_LIMIT)
            acts.append((hg * jax.nn.sigmoid(SWIGLU_ALPHA * hg) * (hl + 1.0)).astype(BF16))
        act = jnp.concatenate(acts, axis=1)
        y = jnp.dot(act, wdown[...], preferred_element_type=F32) + bd_ref[0, 0]
        for s in range(SUB):
            o_ref[pl.ds(s, tm, stride=SUB), :] = y[:, s * LANES:(s + 1) * LANES]

    @pl.when(t >= nt)
    def _():
        o_ref[...] = jnp.zeros(o_ref.shape, o_ref.dtype)


def _experts(h_rows, row_tok, tile_e, n_tiles_used, next_e, layer, wgu, bgu, wd, bd, pmat):
    n_rows = row_tok.shape[0]
    tm = MOE_TM
    D, F2 = wgu.shape[2], wgu.shape[3]
    row_map = lambda t, te, nt, tok, nxt: (t, 0)
    b_map = lambda t, te, nt, tok, nxt: (layer, te[t], 0, 0)
    return pl.pallas_call(
        functools.partial(_expert_kernel, tm=tm, layer=layer),
        grid_spec=pltpu.PrefetchScalarGridSpec(
            num_scalar_prefetch=4,
            grid=(n_rows // tm,),
            in_specs=[pl.BlockSpec(memory_space=pl.ANY),
                      pl.BlockSpec(memory_space=pl.ANY), pl.BlockSpec((1, 1, 1, F2), b_map),
                      pl.BlockSpec(memory_space=pl.ANY), pl.BlockSpec((1, 1, 1, D), b_map),
                      pl.BlockSpec(pmat.shape, lambda t, te, nt, tok, nxt: (0, 0))],
            out_specs=pl.BlockSpec((tm * SUB, LANES), row_map),
            scratch_shapes=[pltpu.VMEM((GATHER_SLOTS * tm * SUB, LANES), F32), pltpu.SemaphoreType.DMA((GATHER_SLOTS,)),
                            pltpu.VMEM((D, F2), F32), pltpu.VMEM((F2 // 2, D), F32), pltpu.SemaphoreType.DMA((2,)),
                            pltpu.VMEM((D, F2), BF16), pltpu.VMEM((F2 // 2, D), BF16)],
        ),
        out_shape=jax.ShapeDtypeStruct((n_rows * SUB, LANES), F32),
        compiler_params=_cparams(("arbitrary",)),
        name="moe_experts",
    )(tile_e, n_tiles_used, row_tok, next_e, h_rows, wgu, bgu, wd, bd, pmat)


def _combine_ln_kernel(dest_ref, y_ref, gate_ref, h_ref, g_ref, b_ref, out_ref, ybuf, sems, *, tm):
    i = pl.program_id(0)
    n = pl.num_programs(0)
    slot = i % 2
    per = TOP_K * tm

    @pl.when(i == 0)
    def _():
        _start_row_copies(dest_ref, 0, per, y_ref, ybuf, 0, sems.at[0])

    @pl.when(i + 1 < n)
    def _():
        _start_row_copies(dest_ref, (i + 1) * per, per, y_ref, ybuf, (1 - slot) * per, sems.at[1 - slot])

    _wait_row_copies(per, y_ref, ybuf, slot * per, sems.at[slot])
    gate = gate_ref[...]
    pieces = []
    for s in range(SUB):
        acc = None
        for kk in range(TOP_K):
            rows = ybuf[pl.ds((slot * per + kk) * SUB + s, tm, stride=TOP_K * SUB), :]
            term = gate[:, kk:kk + 1] * rows
            acc = term if acc is None else acc + term
        pieces.append(acc)
    y = jnp.concatenate(pieces, axis=1)
    out_ref[...] = _layer_norm(ALPHA_DN * h_ref[...] + y, g_ref[...], b_ref[...])


def _combine_ln(ys_rows, dest, gate, h, g, b):
    S, D = h.shape
    tm = min(COMBINE_TM, S)
    return pl.pallas_call(
        functools.partial(_combine_ln_kernel, tm=tm),
        grid_spec=pltpu.PrefetchScalarGridSpec(
            num_scalar_prefetch=1,
            grid=(S // tm,),
            in_specs=[pl.BlockSpec(memory_space=pl.ANY),
                      pl.BlockSpec((tm, LANES), lambda i, d: (i, 0)),
                      pl.BlockSpec((tm, D), lambda i, d: (i, 0)),
                      pl.BlockSpec((1, D), lambda i, d: (0, 0)),
                      pl.BlockSpec((1, D), lambda i, d: (0, 0))],
            out_specs=pl.BlockSpec((tm, D), lambda i, d: (i, 0)),
            scratch_shapes=[pltpu.VMEM((2 * TOP_K * tm * SUB, LANES), F32), pltpu.SemaphoreType.DMA((2,))],
        ),
        out_shape=jax.ShapeDtypeStruct((S, D), F32),
        compiler_params=_cparams(("arbitrary",)),
        name="moe_combine_ln",
    )(dest, ys_rows, gate, h, g.reshape(1, D), b.reshape(1, D))


def _moe_block(h, h_rows, w_router, b_router, layer, expert_params, ln_g, ln_b):
    S, D = h.shape
    E, tm = N_EXPERTS, MOE_TM
    e_full, g_full, rank_full, cnt = _router(h, w_router, b_router)
    top_e = e_full[:, :TOP_K]

    counts = cnt[0, :E].astype(I32)
    padded = (counts + tm - 1) // tm * tm
    pends = jnp.cumsum(padded)
    pstarts = pends - padded
    dest = pstarts[top_e] + rank_full[:, :TOP_K]
    A = S * TOP_K
    n_tiles = A // tm + E
    n_rows = n_tiles * tm
    tok = jnp.broadcast_to(jnp.arange(S, dtype=I32)[:, None], (S, TOP_K))
    row_tok = jnp.zeros((n_rows,), I32).at[dest.reshape(A)].set(tok.reshape(A), unique_indices=True)
    n_used = (pends[-1] // tm).astype(I32)
    tile_start = jnp.arange(n_tiles, dtype=I32) * tm
    tile_e = jnp.minimum((pends[None, :] <= tile_start[:, None]).astype(I32).sum(axis=1), E - 1)
    tile_e = jnp.where(jnp.arange(n_tiles) < n_used, tile_e, tile_e[jnp.maximum(n_used - 1, 0)])
    group_end = pends[tile_e] // tm
    next_e = jnp.where(group_end < n_used, tile_e[jnp.minimum(group_end, n_tiles - 1)], tile_e)

    ys = _experts(h_rows, row_tok, tile_e, n_used.reshape(1), next_e, layer, *expert_params)
    return _combine_ln(ys, dest.reshape(A), g_full, h, ln_g, ln_b)


def _expert_params(w_gu, b_gu, w_down, b_down):
    L, E, _, F2 = w_gu.shape
    blk = 2 * LANES
    pm = np.zeros((blk, blk), np.float32)
    for qq in range(LANES):
        pm[2 * qq, qq] = 1.0
        pm[2 * qq + 1, LANES + qq] = 1.0
    bgu = b_gu.reshape(L, E, F2 // blk, LANES, 2).transpose(0, 1, 2, 4, 3).reshape(L, E, 1, F2)
    return w_gu, bgu, w_down, b_down[:, :, None, :], jnp.asarray(pm, BF16)


def _diff_attention(h, w_in, lq1, lk1, lq2, lk2, subln_g, w_out, ln_g, ln_b, tabs, layer_idx):
    dv = DA_HEADS * 2 * DA_HEAD_DIM
    scale = DA_HEAD_DIM ** -0.5 * LOG2E
    segs = (Seg(0, dv, "rope", scale, BF16), Seg(dv, dv, "rope", 1.0, BF16), Seg(2 * dv, dv, "plain", 1.0, BF16))
    q, k, v = _project(h, w_in.astype(BF16), tabs["c64"], tabs["s64"], segs, DA_HEAD_DIM // 2, "da_project")
    lam_init = 0.8 - 0.6 * math.exp(-0.3 * layer_idx)
    lam4 = jnp.stack([lq1, lk1, lq2, lk2]).astype(F32)
    o = _flash("da", q, k, v.T, (lam4, subln_g.reshape(1, -1).astype(F32)), lam_init, "da_attention")
    return _out_ln(o, w_out.astype(BF16), h, ln_g, ln_b, "da_out_ln")


def _mla(h, w_in, q_norm_g, kv_norm_g, w_uq, w_ukv, w_out, ln_g, ln_b, tabs):
    H = MLA_HEADS
    D = h.shape[1]
    qk_dim = MLA_NOPE + MLA_ROPE
    pad_rope = ((0, 0), (MLA_NOPE, LANES - MLA_NOPE - MLA_ROPE))
    w_in_p = jnp.concatenate([w_in[:, :MLA_Q_RANK + MLA_KV_RANK],
                              jnp.pad(w_in[:, MLA_Q_RANK + MLA_KV_RANK:], pad_rope)], axis=1).astype(BF16)
    w_uq_p = jnp.pad(w_uq.reshape(MLA_Q_RANK, H, qk_dim), ((0, 0), (0, 0), (0, LANES - qk_dim)))
    w_uq_p = w_uq_p.reshape(MLA_Q_RANK, H * LANES).astype(BF16)
    w_ukv3 = w_ukv.reshape(MLA_KV_RANK, H, MLA_NOPE + MLA_V)
    w_uk_p = jnp.pad(w_ukv3[:, :, :MLA_NOPE], ((0, 0), (0, 0), (0, LANES - MLA_NOPE)))
    w_uk_p = w_uk_p.reshape(MLA_KV_RANK, H * LANES).astype(BF16)
    w_uv = w_ukv3[:, :, MLA_NOPE:].reshape(MLA_KV_RANK, H * MLA_V).astype(BF16)
    q, k, v = _mla_project(h, w_in_p, w_uq_p, w_uk_p, w_uv, q_norm_g.reshape(1, -1), kv_norm_g.reshape(1, -1),
                           tabs["c32"], tabs["s32"], qk_dim ** -0.5 * LOG2E)
    o = _flash("mla", q, k, v.T, (), 0.0, "mla_attention")
    return _out_ln(o, w_out.astype(BF16), h, ln_g, ln_b, "mla_out_ln")


def _nsa(h, w_in, pos_k, pos_v, ck_w1, ck_w2, cv_w1, cv_w2, w_out, ln_g, ln_b, tabs):
    S, D = h.shape
    H, G, d = NSA_HEADS, NSA_GROUPS, NSA_HEAD_DIM
    R = H // G
    st, Ls = NSA_CMP_STRIDE, NSA_SLC_LEN
    n_cmp = (S - NSA_CMP_LEN) // st + 1
    nc = S // st
    n_slc = S // Ls
    n_top = min(NSA_SLC_TOPK, n_slc)
    nslp = -(-n_slc // LANES) * LANES
    gd = G * d

    def pad_heads(w, n):
        return jnp.pad(w.reshape(D, n, d), ((0, 0), (0, 0), (0, LANES - d))).reshape(D, n * LANES)

    off = np.cumsum([0, H * d] + [gd] * 6)
    wq, wkc, wvc, wks, wvs, wkw, wvw = [w_in[:, off[t]:off[t + 1]] for t in range(7)]
    wgl = w_in[:, off[7]:].reshape(D, H, 3)
    w_a = jnp.concatenate([pad_heads(wq, H), pad_heads(wks, G), pad_heads(wkw, G), wvs, wvw, wkc, wvc], axis=1).astype(BF16)
    hq, hg = H * LANES, G * LANES
    segs_a = (Seg(0, hq, "rope", d ** -0.5 * LOG2E, BF16), Seg(hq, hg, "rope", 1.0, BF16), Seg(hq + hg, hg, "rope", 1.0, BF16),
              Seg(hq + 2 * hg, gd, "plain", 1.0, BF16), Seg(hq + 2 * hg + gd, gd, "plain", 1.0, BF16),
              Seg(hq + 2 * hg + 2 * gd, gd, "plain", 1.0, F32), Seg(hq + 2 * hg + 3 * gd, gd, "plain", 1.0, F32))
    q, ks, kw, vs, vw, kc, vc = _project(h, w_a, tabs["c64p"], tabs["s64p"], segs_a, d // 2, "nsa_project")
    w_g = jnp.concatenate([jnp.repeat(wgl[:, :, b], d, axis=1) for b in range(3)], axis=1).astype(BF16)
    segs_g = tuple(Seg(b * H * d, H * d, "sigmoid", 1.0, F32) for b in range(3))
    gc, gs, gw = _project(h, w_g, tabs["c64p"], tabs["s64p"], segs_g, d // 2, "nsa_gates")

    def chunks(t):
        return t.reshape(nc, st, G, d).transpose(2, 0, 1, 3).reshape(G, nc, st * d)

    k_cmp = _compress(chunks(kc), pos_k, ck_w1, ck_w2, tabs["ccmp"], tabs["scmp"], True, "nsa_compress_k")
    v_cmp = _compress(chunks(vc), pos_v, cv_w1, cv_w2, tabs["ccmp"], tabs["scmp"], False, "nsa_compress_v")

    ratio = Ls // st
    amat_t = np.zeros((nslp, nc), np.float32)
    for jj in range(n_slc):
        for mm in range(ratio):
            for nn in range(NSA_CMP_LEN // st):
                c = ratio * jj + mm - nn
                if 0 <= c < n_cmp:
                    amat_t[jj, c] += 1.0
    vt_cmp = v_cmp[:, :, :d].transpose(0, 2, 1)
    ocw, sel = _nsa_local(q, k_cmp, vt_cmp, kw, vw.T, gc, gw, jnp.asarray(amat_t, BF16), n_slc, n_top)

    tk = min(NSA_TK, S)
    per_tile = tk // Ls
    tiles_per_selblock = LANES // per_tile
    assert per_tile <= LANES - d
    kpat = np.zeros((tk, LANES), np.float32)
    for t in range(tk):
        kpat[t, d + t // Ls] = 1.0
    pm = np.zeros((tiles_per_selblock, LANES, LANES), np.float32)
    for u in range(tiles_per_selblock):
        for bb in range(per_tile):
            pm[u, per_tile * u + bb, d + bb] = MASK_BIG
    o = _nsa_select(q, ks, vs.T, sel, jnp.asarray(kpat, BF16), jnp.asarray(pm, BF16), ocw, gs)
    return _out_ln(o, w_out.astype(BF16), h, ln_g, ln_b, "nsa_out_ln")


def _rope_tables(S):
    pos = jnp.arange(S)
    c64, s64 = _rope_angles(pos, DA_HEAD_DIM)
    c32, s32 = _rope_angles(pos, MLA_ROPE)
    nc = S // NSA_CMP_STRIDE
    cc, sc = _rope_angles(jnp.arange(nc) * NSA_CMP_STRIDE + NSA_CMP_LEN - 1, NSA_HEAD_DIM)
    one = lambda n, w: jnp.ones((n, w), F32)
    zero = lambda n, w: jnp.zeros((n, w), F32)
    cat = lambda *a: jnp.concatenate(a, axis=1)
    return {
        "c64": cat(c64, c64, c64, c64), "s64": cat(-s64, s64, -s64, s64),
        "c64p": cat(c64, c64, one(S, 64)), "s64p": cat(-s64, s64, zero(S, 64)),
        "c32": cat(one(S, 64), c32, c32, one(S, 32)), "s32": cat(zero(S, 64), -s32, s32, zero(S, 32)),
        "ccmp": cat(cc, cc, one(nc, 64)), "scmp": cat(-sc, sc, zero(nc, 64)),
    }


def kernel(x, da_w_in, da_lambda_q1, da_lambda_k1, da_lambda_q2, da_lambda_k2, da_subln, da_w_out, mla_w_in, mla_q_norm, mla_kv_norm, mla_w_uq, mla_w_ukv, mla_w_out, nsa_w_in, nsa_cmp_pos_k, nsa_cmp_pos_v, nsa_cmp_k_w1, nsa_cmp_k_w2, nsa_cmp_v_w1, nsa_cmp_v_w2, nsa_w_out, ln1_g, ln1_b, ln2_g, ln2_b, moe_w_router, moe_b_router, moe_w_gu, moe_b_gu, moe_w_down, moe_b_down):
    B, S, D = x.shape
    assert B == 1 and D == D_MODEL
    tabs = _rope_tables(S)
    expert_params = _expert_params(moe_w_gu, moe_b_gu, moe_w_down, moe_b_down)
    h = x.reshape(S, D)
    for i in range(DEPTH):
        m, j = i % N_MIXERS, i // N_MIXERS
        if m == 0:
            h, h_rows = _diff_attention(h, da_w_in[j], da_lambda_q1[j], da_lambda_k1[j], da_lambda_q2[j],
                                        da_lambda_k2[j], da_subln[j], da_w_out[j], ln1_g[i], ln1_b[i], tabs, i)
        elif m == 1:
            h, h_rows = _mla(h, mla_w_in[j], mla_q_norm[j], mla_kv_norm[j], mla_w_uq[j], mla_w_ukv[j], mla_w_out[j],
                             ln1_g[i], ln1_b[i], tabs)
        else:
            h, h_rows = _nsa(h, nsa_w_in[j], nsa_cmp_pos_k[j], nsa_cmp_pos_v[j], nsa_cmp_k_w1[j], nsa_cmp_k_w2[j],
                             nsa_cmp_v_w1[j], nsa_cmp_v_w2[j], nsa_w_out[j], ln1_g[i], ln1_b[i], tabs)
        h = _moe_block(h, h_rows, moe_w_router[i], moe_b_router[i], i, expert_params, ln2_g[i], ln2_b[i])
    return h.reshape(B, S, D)
```

```python
import collections
import functools
import math

import numpy as np
import jax
import jax.numpy as jnp
from jax import lax
from jax.experimental import pallas as pl
from jax.experimental.pallas import tpu as pltpu

F32 = jnp.float32
BF16 = jnp.bfloat16
I32 = jnp.int32

D_MODEL = 1024
DEPTH = 4
N_MIXERS = 3
ROPE_THETA = 10000.0
LN_EPS = 1e-5
RMS_EPS = 1e-6
NEG = -1e30
ALPHA_DN = (2 * DEPTH) ** 0.25
LOG2E = math.log2(math.e)

DA_HEADS = 8
DA_HEAD_DIM = 64
MLA_HEADS = 16
MLA_NOPE = 64
MLA_ROPE = 32
MLA_V = 64
MLA_Q_RANK = 256
MLA_KV_RANK = 128
NSA_HEADS = 16
NSA_GROUPS = 4
NSA_HEAD_DIM = 64
NSA_CMP_LEN = 32
NSA_CMP_STRIDE = 16
NSA_CMP_HIDDEN = 256
NSA_SLC_LEN = 64
NSA_SLC_TOPK = 16
NSA_WINDOW = 512
NSA_FORCE = 1e9
N_EXPERTS = 32
TOP_K = 4
D_EXPERT = 1024
SWIGLU_LIMIT = 7.0
SWIGLU_ALPHA = 1.702

LANES = 128
SUB = 8
VMEM_LIMIT = 48 * 1024 * 1024

PROJ_TM = 256
ATT_TQ = 512
ATT_TK = 512
KV_UNROLL = 2
NSA_TQ = 128
NSA_SEL_TQ = 256
NSA_TK = 512
NSA_KV_UNROLL = 4
MOE_TM = 512
LN_TM = 512
COMBINE_TM = 256
GATHER_SLOTS = 3


def _cparams(sem):
    return pltpu.CompilerParams(dimension_semantics=sem, vmem_limit_bytes=VMEM_LIMIT)


def _rope_angles(pos, dim):
    inv = ROPE_THETA ** (-jnp.arange(0, dim, 2, dtype=F32) / dim)
    ang = pos.astype(F32)[:, None] * inv[None, :]
    return jnp.cos(ang), jnp.sin(ang)


def _rope128(t, c, sg, half):
    lane = lax.broadcasted_iota(I32, t.shape, 1)
    first = (lane % (2 * half)) < half
    partner = jnp.where(first, pltpu.roll(t, LANES - half, 1), pltpu.roll(t, half, 1))
    return t * c + partner * sg


Seg = collections.namedtuple("Seg", "start width mode scale dtype")


def _proj_kernel(*refs, segs, half):
    x_ref, w_ref, c_ref, s_ref = refs[:4]
    out_refs = refs[4:]
    xb = x_ref[...].astype(BF16)
    for seg, o_ref in zip(segs, out_refs):
        acc = jnp.dot(xb, w_ref[:, seg.start:seg.start + seg.width], preferred_element_type=F32)
        if seg.mode == "rope":
            c = c_ref[...]
            sg = s_ref[...]
            for g in range(seg.width // LANES):
                t = _rope128(acc[:, g * LANES:(g + 1) * LANES], c, sg, half)
                o_ref[:, g * LANES:(g + 1) * LANES] = (t * seg.scale).astype(o_ref.dtype)
        elif seg.mode == "sigmoid":
            o_ref[...] = jax.nn.sigmoid(acc).astype(o_ref.dtype)
        else:
            o_ref[...] = (acc * seg.scale).astype(o_ref.dtype) if seg.scale != 1.0 else acc.astype(o_ref.dtype)


def _project(x, w, tab_c, tab_s, segs, half, name):
    S, K = x.shape
    tm = min(PROJ_TM, S)
    return pl.pallas_call(
        functools.partial(_proj_kernel, segs=segs, half=half),
        grid=(S // tm,),
        in_specs=[
            pl.BlockSpec((tm, K), lambda i: (i, 0)),
            pl.BlockSpec(w.shape, lambda i: (0, 0)),
            pl.BlockSpec((tm, LANES), lambda i: (i, 0)),
            pl.BlockSpec((tm, LANES), lambda i: (i, 0)),
        ],
        out_specs=[pl.BlockSpec((tm, s.width), lambda i: (i, 0)) for s in segs],
        out_shape=[jax.ShapeDtypeStruct((S, s.width), s.dtype) for s in segs],
        compiler_params=_cparams(("parallel",)),
        name=name,
    )(x, w, tab_c, tab_s)


def _qk(q, k):
    return lax.dot_general(q, k, (((1,), (1,)), ((), ())), preferred_element_type=F32)


ONES_ROWS = 16


def _flash_update(s_t, vt_ext, m_ref, acc_ref, st):
    m_old = m_ref[st]
    m_new = jnp.maximum(m_old, jnp.max(s_t, axis=0, keepdims=True))
    alpha = jnp.exp2(m_old - m_new)
    p_t = jnp.exp2(s_t - m_new).astype(BF16)
    acc_ref[st] = alpha * acc_ref[st] + jnp.dot(vt_ext, p_t, preferred_element_type=F32)
    m_ref[st] = m_new


def _flash_kernel(*refs, variant, tq, tk, lam_init):
    if variant == "da":
        lam_ref, g_ref, q_ref, k_ref, vt_ref, o_ref, qa, qb, m_ref, acc_ref = refs
    else:
        q_ref, k_ref, vt_ref, o_ref, m_ref, acc_ref = refs
    i = pl.program_id(1)
    dv = vt_ref.shape[0] if variant == "da" else MLA_V
    v_rows = [(0, dv), (0, dv)] if variant == "da" else [(0, dv), (dv, 2 * dv)]

    if variant == "da":
        q = q_ref[...]
        lane = lax.broadcasted_iota(I32, q.shape, 1)
        qa[...] = jnp.where(lane < DA_HEAD_DIM, q, jnp.zeros_like(q))
        qb[...] = jnp.where(lane >= DA_HEAD_DIM, q, jnp.zeros_like(q))
    m_ref[...] = jnp.full(m_ref.shape, NEG, F32)
    acc_ref[...] = jnp.zeros(acc_ref.shape, F32)
    ones = jnp.ones((ONES_ROWS, tk), BF16)

    def scores_of(js):
        out = []
        for j in js:
            start = pl.multiple_of(j * tk, tk)
            if variant == "da":
                k = k_ref[pl.ds(start, tk), :]
                ops = ((k, qa[...]), (k, qb[...]))
            else:
                ops = ((k_ref[pl.ds(start, tk), :LANES], q_ref[:, :LANES]),
                       (k_ref[pl.ds(start, tk), LANES:], q_ref[:, LANES:]))
            out.append([_qk(kk, qq) for kk, qq in ops])
        return out

    def update(js, all_scores, masks):
        for j, scores, masked in zip(js, all_scores, masks):
            start = pl.multiple_of(j * tk, tk)
            vt_exts = {}
            for st, s_t in enumerate(scores):
                if masked:
                    kpos = j * tk + lax.broadcasted_iota(I32, s_t.shape, 0)
                    qpos = i * tq + lax.broadcasted_iota(I32, s_t.shape, 1)
                    s_t = jnp.where(kpos <= qpos, s_t, NEG)
                lo, hi = v_rows[st]
                if (lo, hi) not in vt_exts:
                    vt_exts[(lo, hi)] = jnp.concatenate([vt_ref[lo:hi, pl.ds(start, tk)], ones], axis=0)
                _flash_update(s_t, vt_exts[(lo, hi)], m_ref, acc_ref, st)

    def tiles(js, masks):
        update(js, scores_of(js), masks)

    n_full = (i * tq + 1) // tk
    n_diag = max(1, tq // tk)
    diag = [n_full + d for d in range(n_diag)]

    def body(p, carry):
        tiles([p * KV_UNROLL + u for u in range(KV_UNROLL)], [False] * KV_UNROLL)
        return carry

    lax.fori_loop(0, n_full // KV_UNROLL, body, 0)
    if KV_UNROLL == 2 and n_diag == 1:
        @pl.when(n_full % 2 == 1)
        def _():
            tiles([n_full - 1, n_full], [False, True])

        @pl.when(n_full % 2 == 0)
        def _():
            tiles(diag, [True])
    else:
        for u in range(KV_UNROLL - 1):
            @pl.when(u < n_full % KV_UNROLL)
            def _():
                tiles([n_full - n_full % KV_UNROLL + u], [False])
        tiles(diag, [True] * n_diag)

    outs_t = []
    for st in range(2):
        acc = acc_ref[st]
        outs_t.append(acc[:dv] / acc[dv:dv + 1])
    if variant == "da":
        o1, o2 = outs_t[0].T, outs_t[1].T
        lam4 = lam_ref[...]
        lam = (jnp.exp(jnp.sum(lam4[0:1] * lam4[1:2], axis=1, keepdims=True))
               - jnp.exp(jnp.sum(lam4[2:3] * lam4[3:4], axis=1, keepdims=True)) + lam_init)
        o = o1 - lam * o2
        ms = jnp.mean(o * o, axis=-1, keepdims=True)
        o = o * lax.rsqrt(ms + LN_EPS) * g_ref[...] * (1.0 - lam_init)
    else:
        o = jnp.concatenate(outs_t, axis=0).T
    o_ref[...] = o.astype(o_ref.dtype)


def _flash(variant, q, k, vt, extra, lam_init, name):
    S = q.shape[0]
    tq, tk = min(ATT_TQ, S), min(ATT_TK, S)
    assert tk % tq == 0 or tq % tk == 0
    qw = LANES if variant == "da" else 2 * LANES
    n_groups = q.shape[1] // qw
    in_specs = [
        pl.BlockSpec((tq, qw), lambda h, i: (i, h)),
        pl.BlockSpec((S, qw), lambda h, i: (0, h), pipeline_mode=pl.Buffered(1)),
        pl.BlockSpec((LANES, S), lambda h, i: (h, 0), pipeline_mode=pl.Buffered(1)),
    ]
    dv = LANES if variant == "da" else MLA_V
    scratch = [pltpu.VMEM((2, 1, tq), F32), pltpu.VMEM((2, dv + ONES_ROWS, tq), F32)]
    if variant == "da":
        in_specs = [pl.BlockSpec(extra[0].shape, lambda h, i: (0, 0)),
                    pl.BlockSpec(extra[1].shape, lambda h, i: (0, 0))] + in_specs
        scratch = [pltpu.VMEM((tq, LANES), BF16)] * 2 + scratch
    return pl.pallas_call(
        functools.partial(_flash_kernel, variant=variant, tq=tq, tk=tk, lam_init=lam_init),
        grid=(n_groups, S // tq),
        in_specs=in_specs,
        out_specs=pl.BlockSpec((tq, LANES), lambda h, i: (i, h)),
        out_shape=jax.ShapeDtypeStruct((S, n_groups * LANES), BF16),
        scratch_shapes=scratch,
        compiler_params=_cparams(("parallel", "parallel")),
        name=name,
    )(*extra, q, k, vt)


def _layer_norm(x, g, b):
    mu = jnp.mean(x, axis=-1, keepdims=True)
    xc = x - mu
    var = jnp.mean(xc * xc, axis=-1, keepdims=True)
    return xc * lax.rsqrt(var + LN_EPS) * g + b


def _out_ln_kernel(o_ref, w_ref, h_ref, g_ref, b_ref, out_ref, rows_ref):
    y = jnp.dot(o_ref[...], w_ref[...], preferred_element_type=F32)
    hn = _layer_norm(ALPHA_DN * h_ref[...] + y, g_ref[...], b_ref[...])
    out_ref[...] = hn
    tm = hn.shape[0]
    for s in range(SUB):
        rows_ref[pl.ds(s, tm, stride=SUB), :] = hn[:, s * LANES:(s + 1) * LANES]


def _out_ln(o, w, h, g, b, name):
    S, K = o.shape
    D = h.shape[1]
    tm = min(LN_TM, S)
    return pl.pallas_call(
        _out_ln_kernel,
        grid=(S // tm,),
        in_specs=[
            pl.BlockSpec((tm, K), lambda i: (i, 0)),
            pl.BlockSpec((K, D), lambda i: (0, 0)),
            pl.BlockSpec((tm, D), lambda i: (i, 0)),
            pl.BlockSpec((1, D), lambda i: (0, 0)),
            pl.BlockSpec((1, D), lambda i: (0, 0)),
        ],
        out_specs=[pl.BlockSpec((tm, D), lambda i: (i, 0)), pl.BlockSpec((tm * SUB, LANES), lambda i: (i, 0))],
        out_shape=[jax.ShapeDtypeStruct((S, D), F32), jax.ShapeDtypeStruct((S * SUB, LANES), F32)],
        compiler_params=_cparams(("parallel",)),
        name=name,
    )(o, w, h, g.reshape(1, D), b.reshape(1, D))


def _rms(x, g, eps):
    return x * lax.rsqrt(jnp.mean(x * x, axis=-1, keepdims=True) + eps) * g


def _mla_proj_kernel(x_ref, win_ref, wuq_ref, wuk_ref, wuv_ref, gq_ref, gkv_ref, c_ref, s_ref,
                     q_ref, k_ref, v_ref, *, scale):
    xb = x_ref[...].astype(BF16)
    p1 = jnp.dot(xb, win_ref[...], preferred_element_type=F32)
    c_q = _rms(p1[:, :MLA_Q_RANK], gq_ref[...], RMS_EPS).astype(BF16)
    c_kv = _rms(p1[:, MLA_Q_RANK:MLA_Q_RANK + MLA_KV_RANK], gkv_ref[...], RMS_EPS).astype(BF16)
    c = c_ref[...]
    sg = s_ref[...]
    half = MLA_ROPE // 2
    k_rope = _rope128(p1[:, MLA_Q_RANK + MLA_KV_RANK:], c, sg, half)
    q = jnp.dot(c_q, wuq_ref[...], preferred_element_type=F32)
    k = jnp.dot(c_kv, wuk_ref[...], preferred_element_type=F32)
    for h in range(MLA_HEADS):
        sl = slice(h * LANES, (h + 1) * LANES)
        q_ref[:, sl] = (_rope128(q[:, sl], c, sg, half) * scale).astype(q_ref.dtype)
        k_ref[:, sl] = (k[:, sl] + k_rope).astype(k_ref.dtype)
    v_ref[...] = jnp.dot(c_kv, wuv_ref[...], preferred_element_type=F32).astype(v_ref.dtype)


def _mla_project(x, w_in, w_uq, w_uk, w_uv, gq, gkv, tab_c, tab_s, scale):
    S, D = x.shape
    tm = min(PROJ_TM, S)
    full = lambda a: pl.BlockSpec(a.shape, lambda i: (0,) * a.ndim)
    row = lambda w: pl.BlockSpec((tm, w), lambda i: (i, 0))
    hw = MLA_HEADS * LANES
    return pl.pallas_call(
        functools.partial(_mla_proj_kernel, scale=scale),
        grid=(S // tm,),
        in_specs=[row(D), full(w_in), full(w_uq), full(w_uk), full(w_uv), full(gq), full(gkv), row(LANES), row(LANES)],
        out_specs=[row(hw), row(hw), row(MLA_HEADS * MLA_V)],
        out_shape=[jax.ShapeDtypeStruct((S, hw), BF16), jax.ShapeDtypeStruct((S, hw), BF16),
                   jax.ShapeDtypeStruct((S, MLA_HEADS * MLA_V), BF16)],
        compiler_params=_cparams(("parallel",)),
        name="mla_project",
    )(x, w_in, w_uq, w_uk, w_uv, gq, gkv, tab_c, tab_s)


def _gelu_tanh(x):
    return 0.5 * x * (1.0 + jnp.tanh(math.sqrt(2.0 / math.pi) * (x + 0.044715 * (x * x * x))))


def _compress_kernel(x_ref, pa_ref, pb_ref, w1a_ref, w1b_ref, w2_ref, c_ref, s_ref, o_ref, *, nc, rope):
    x = x_ref[0]
    ha = jnp.dot((x + pa_ref[...]).astype(BF16), w1a_ref[...], preferred_element_type=F32)
    hb = jnp.dot((x + pb_ref[...]).astype(BF16), w1b_ref[...], preferred_element_type=F32)
    hid = ha + pltpu.roll(hb, nc - 1, 0)
    out = jnp.dot(_gelu_tanh(hid).astype(BF16), w2_ref[...], preferred_element_type=F32)
    if rope:
        out = _rope128(out, c_ref[...], s_ref[...], NSA_HEAD_DIM // 2)
    o_ref[0] = out.astype(o_ref.dtype)


def _compress(xc, pos, w1, w2, tab_c, tab_s, rope, name):
    G, nc, cw = xc.shape
    half = NSA_CMP_LEN // 2
    pa = pos[:half].reshape(1, cw)
    pb = pos[half:].reshape(1, cw)
    w1a = w1[:cw].astype(BF16)
    w1b = w1[cw:].astype(BF16)
    w2p = jnp.pad(w2, ((0, 0), (0, LANES - w2.shape[1]))).astype(BF16)
    full = lambda a: pl.BlockSpec(a.shape, lambda g: (0,) * a.ndim)
    return pl.pallas_call(
        functools.partial(_compress_kernel, nc=nc, rope=rope),
        grid=(G,),
        in_specs=[pl.BlockSpec((1, nc, cw), lambda g: (g, 0, 0)), full(pa), full(pb), full(w1a), full(w1b),
                  full(w2p), full(tab_c), full(tab_s)],
        out_specs=pl.BlockSpec((1, nc, LANES), lambda g: (g, 0, 0)),
        out_shape=jax.ShapeDtypeStruct((G, nc, LANES), BF16),
        compiler_params=_cparams(("parallel",)),
        name=name,
    )(xc, pa, pb, w1a, w1b, w2p, tab_c, tab_s)


def _split3(x):
    hi = x.astype(BF16)
    r = x - hi.astype(F32)
    mid = r.astype(BF16)
    lo = (r - mid.astype(F32)).astype(BF16)
    return hi, mid, lo


def _heads_to_rows(o_t, tq):
    R = NSA_HEADS // NSA_GROUPS
    return jnp.concatenate([o_t[:NSA_HEAD_DIM, r * tq:(r + 1) * tq] for r in range(R)], axis=0).T


def _nsa_local_kernel(q_ref, kc_ref, vct_ref, kw0, kw1, kw2, kw3, kw4, vw0, vw1, vw2, vw3, vw4,
                      gc_ref, gw_ref, amat_ref, o_ref, sel_ref, *, tq, nc, n_slc, n_top):
    i = pl.program_id(1)
    R = NSA_HEADS // NSA_GROUPS
    q = jnp.concatenate([q_ref[:, r * LANES:(r + 1) * LANES] for r in range(R)], axis=0)
    cols = R * tq
    qpos = i * tq + lax.broadcasted_iota(I32, (1, cols), 1) % tq

    s_t = _qk(kc_ref[0], q)
    cmp_end = lax.broadcasted_iota(I32, (nc, cols), 0) * NSA_CMP_STRIDE + (NSA_CMP_LEN - 1)
    s_t = jnp.where(cmp_end <= qpos, s_t, NEG)
    e = jnp.exp2(s_t - jnp.max(s_t, axis=0, keepdims=True))
    den = jnp.sum(e, axis=0, keepdims=True)
    p_t = e * jnp.where(qpos >= NSA_CMP_LEN - 1, 1.0 / den, 0.0)
    o_ct = jnp.dot(vct_ref[0], p_t.astype(BF16), preferred_element_type=F32)

    p_g = p_t[:, 0:tq]
    for r in range(1, R):
        p_g = p_g + p_t[:, r * tq:(r + 1) * tq]
    amat = amat_ref[...]
    imp = None
    for part in _split3(p_g):
        t = jnp.dot(amat, part, preferred_element_type=F32)
        imp = t if imp is None else imp + t
    nslp = imp.shape[0]
    jb = lax.broadcasted_iota(I32, (nslp, tq), 0)
    cur = (i * tq + lax.broadcasted_iota(I32, (1, tq), 1)) // NSA_SLC_LEN
    valid = jb <= cur
    forced = valid & ((jb == 0) | (jb >= cur - 1))
    score = jnp.where(forced, NSA_FORCE, jnp.where(valid, imp, -1.0))
    score = jnp.where(jb < n_slc, score, -2.0)
    sel = jnp.zeros((nslp, tq), F32)
    for _ in range(n_top):
        mx = jnp.max(score, axis=0, keepdims=True)
        idx = jnp.min(jnp.where(score == mx, jb, nslp), axis=0, keepdims=True)
        hit = jb == idx
        sel = jnp.where(hit, 1.0, sel)
        score = jnp.where(hit, -jnp.inf, score)
    sel_ref[0] = sel.T.astype(sel_ref.dtype)

    nwin = NSA_WINDOW // tq + 1
    k_w = jnp.concatenate([r[...] for r in (kw0, kw1, kw2, kw3, kw4)], axis=0)
    vt_w = jnp.concatenate([r[...] for r in (vw0, vw1, vw2, vw3, vw4)], axis=1)
    s_t = _qk(k_w, q)
    kpos = (i - (nwin - 1)) * tq + lax.broadcasted_iota(I32, (nwin * tq, cols), 0)
    m_w = (kpos <= qpos) & (kpos > qpos - NSA_WINDOW) & (kpos >= 0)
    s_t = jnp.where(m_w, s_t, NEG)
    e = jnp.exp2(s_t - jnp.max(s_t, axis=0, keepdims=True))
    p_w = e * (1.0 / jnp.sum(e, axis=0, keepdims=True))
    o_wt = jnp.dot(vt_w, p_w.astype(BF16), preferred_element_type=F32)

    o_ref[...] = gc_ref[...] * _heads_to_rows(o_ct, tq) + gw_ref[...] * _heads_to_rows(o_wt, tq)


def _nsa_local(q, k_cmp, vt_cmp, kw, vwt, gc, gw, amat_t, n_slc, n_top):
    S = q.shape[0]
    tq = NSA_TQ
    G = NSA_GROUPS
    nc = k_cmp.shape[1]
    nslp = amat_t.shape[0]
    gw_lanes = (NSA_HEADS // G) * NSA_HEAD_DIM
    nwin = NSA_WINDOW // tq + 1
    assert nwin == 5

    def kwin(t):
        return pl.BlockSpec((tq, LANES), lambda g, i: (jnp.maximum(i - (nwin - 1) + t, 0), g))

    def vwin(t):
        return pl.BlockSpec((NSA_HEAD_DIM, tq), lambda g, i: (g, jnp.maximum(i - (nwin - 1) + t, 0)))

    in_specs = ([pl.BlockSpec((tq, (NSA_HEADS // G) * LANES), lambda g, i: (i, g)),
                 pl.BlockSpec((1, nc, LANES), lambda g, i: (g, 0, 0)),
                 pl.BlockSpec((1, NSA_HEAD_DIM, nc), lambda g, i: (g, 0, 0))]
                + [kwin(t) for t in range(nwin)] + [vwin(t) for t in range(nwin)]
                + [pl.BlockSpec((tq, gw_lanes), lambda g, i: (i, g)),
                   pl.BlockSpec((tq, gw_lanes), lambda g, i: (i, g)),
                   pl.BlockSpec(amat_t.shape, lambda g, i: (0, 0))])
    return pl.pallas_call(
        functools.partial(_nsa_local_kernel, tq=tq, nc=nc, n_slc=n_slc, n_top=n_top),
        grid=(G, S // tq),
        in_specs=in_specs,
        out_specs=[pl.BlockSpec((tq, gw_lanes), lambda g, i: (i, g)),
                   pl.BlockSpec((1, tq, nslp), lambda g, i: (g, i, 0))],
        out_shape=[jax.ShapeDtypeStruct((S, NSA_HEADS * NSA_HEAD_DIM), F32),
                   jax.ShapeDtypeStruct((G, S, nslp), BF16)],
        compiler_params=_cparams(("parallel", "parallel")),
        name="nsa_local",
    )(q, k_cmp, vt_cmp, *([kw] * nwin), *([vwt] * nwin), gc, gw, amat_t)


MASK_BIG = 2.0 ** 100


def _nsa_select_kernel(q_ref, k_ref, vt_ref, sel_ref, kpat_ref, pm_ref, ocw_ref, gs_ref, o_ref, m_ref, acc_ref,
                       *, tq, tk, tiles_per_selblock):
    i = pl.program_id(1)
    R = NSA_HEADS // NSA_GROUPS
    dv = vt_ref.shape[0]
    m_ref[...] = jnp.full(m_ref.shape, NEG, F32)
    acc_ref[...] = jnp.zeros(acc_ref.shape, F32)
    ones = jnp.ones((ONES_ROWS, tk), BF16)

    def tiles(js, masks):
        exts = []
        for j in js:
            sel_blk = sel_ref[0, :, pl.ds(pl.multiple_of((j // tiles_per_selblock) * LANES, LANES), LANES)]
            exts.append(jnp.dot(sel_blk - 1.0, pm_ref[j % tiles_per_selblock], preferred_element_type=F32).astype(BF16))
        work = []
        for j, ext in zip(js, exts):
            start = pl.multiple_of(j * tk, tk)
            k_ext = k_ref[pl.ds(start, tk), :] + kpat_ref[...]
            q_ext = jnp.concatenate([q_ref[:, r * LANES:(r + 1) * LANES] + ext for r in range(R)], axis=0)
            work.append((j, start, _qk(k_ext, q_ext)))
        for (j, start, s_t), masked in zip(work, masks):
            vt_ext = jnp.concatenate([vt_ref[:, pl.ds(start, tk)], ones], axis=0)
            if masked:
                kpos = j * tk + lax.broadcasted_iota(I32, s_t.shape, 0)
                qpos = i * tq + lax.broadcasted_iota(I32, s_t.shape, 1) % tq
                s_t = jnp.where(kpos <= qpos, s_t, NEG)
            _flash_update(s_t, vt_ext, m_ref, acc_ref, 0)

    n_full = (i * tq + 1) // tk

    def body(p, carry):
        tiles([p * NSA_KV_UNROLL + u for u in range(NSA_KV_UNROLL)], [False] * NSA_KV_UNROLL)
        return carry

    lax.fori_loop(0, n_full // NSA_KV_UNROLL, body, 0)
    rem = n_full % NSA_KV_UNROLL
    base = n_full - rem

    @pl.when(rem >= 2)
    def _():
        tiles([base, base + 1], [False, False])

    @pl.when(rem % 2 == 1)
    def _():
        tiles([n_full - 1, n_full], [False, True])

    @pl.when(rem % 2 == 0)
    def _():
        tiles([n_full], [True])

    acc = acc_ref[0]
    o_t = acc[:dv] / acc[dv:dv + 1]
    o_ref[...] = (ocw_ref[...] + gs_ref[...] * _heads_to_rows(o_t, tq)).astype(o_ref.dtype)


def _nsa_select(q, ks, vst, sel, kpat, pm, ocw, gs):
    S = q.shape[0]
    tq, tk = min(NSA_SEL_TQ, S), kpat.shape[0]
    G = NSA_GROUPS
    R = NSA_HEADS // G
    gw_lanes = R * NSA_HEAD_DIM
    nslp = sel.shape[2]
    tiles_per_selblock = pm.shape[0]
    return pl.pallas_call(
        functools.partial(_nsa_select_kernel, tq=tq, tk=tk, tiles_per_selblock=tiles_per_selblock),
        grid=(G, S // tq),
        in_specs=[pl.BlockSpec((tq, R * LANES), lambda g, i: (i, g)),
                  pl.BlockSpec((S, LANES), lambda g, i: (0, g)),
                  pl.BlockSpec((NSA_HEAD_DIM, S), lambda g, i: (g, 0)),
                  pl.BlockSpec((1, tq, nslp), lambda g, i: (g, i, 0)),
                  pl.BlockSpec(kpat.shape, lambda g, i: (0, 0)),
                  pl.BlockSpec(pm.shape, lambda g, i: (0, 0, 0)),
                  pl.BlockSpec((tq, gw_lanes), lambda g, i: (i, g)),
                  pl.BlockSpec((tq, gw_lanes), lambda g, i: (i, g))],
        out_specs=pl.BlockSpec((tq, gw_lanes), lambda g, i: (i, g)),
        out_shape=jax.ShapeDtypeStruct((S, NSA_HEADS * NSA_HEAD_DIM), BF16),
        scratch_shapes=[pltpu.VMEM((1, 1, R * tq), F32), pltpu.VMEM((1, NSA_HEAD_DIM + ONES_ROWS, R * tq), F32)],
        compiler_params=_cparams(("parallel", "parallel")),
        name="nsa_select",
    )(q, ks, vst, sel, kpat, pm, ocw, gs)


def _router_kernel(h_ref, w_ref, b_ref, e_ref, g_ref, rank_ref, cnt_ref, carry):
    @pl.when(pl.program_id(0) == 0)
    def _():
        carry[...] = jnp.zeros(carry.shape, F32)

    x = h_ref[...]
    w = w_ref[...]
    x_hi = x.astype(BF16)
    x_lo = (x - x_hi.astype(F32)).astype(BF16)
    w_hi = w.astype(BF16)
    w_lo = (w - w_hi.astype(F32)).astype(BF16)
    logits = (jnp.dot(x_hi, w_hi, preferred_element_type=F32) + jnp.dot(x_hi, w_lo, preferred_element_type=F32)
              + jnp.dot(x_lo, w_hi, preferred_element_type=F32)) + b_ref[...]
    tm = logits.shape[0]
    lane = lax.broadcasted_iota(I32, logits.shape, 1)
    work = jnp.where(lane < N_EXPERTS, logits, -jnp.inf)
    vals, idxs = [], []
    for _ in range(TOP_K):
        mx = jnp.max(work, axis=1, keepdims=True)
        idx = jnp.min(jnp.where(work == mx, lane, LANES), axis=1, keepdims=True)
        vals.append(mx)
        idxs.append(idx)
        work = jnp.where(lane == idx, -jnp.inf, work)
    exps = [jnp.exp(v - vals[0]) for v in vals]
    den = exps[0]
    for t in exps[1:]:
        den = den + t

    chosen = jnp.zeros(logits.shape, F32)
    for kk in range(TOP_K):
        chosen = jnp.where(lane == idxs[kk], 1.0, chosen)
    r_i = lax.broadcasted_iota(I32, (tm, tm), 0)
    c_i = lax.broadcasted_iota(I32, (tm, tm), 1)
    tri = jnp.where(c_i < r_i, 1.0, 0.0).astype(BF16)
    before = carry[...] + jnp.dot(tri, chosen.astype(BF16), preferred_element_type=F32)
    carry[...] = carry[...] + jnp.sum(chosen, axis=0, keepdims=True)
    cnt_ref[...] = jnp.broadcast_to(carry[...], cnt_ref.shape)

    e_out = jnp.zeros(logits.shape, I32)
    g_out = jnp.zeros(logits.shape, F32)
    r_out = jnp.zeros(logits.shape, F32)
    for kk in range(TOP_K):
        e_out = jnp.where(lane == kk, idxs[kk], e_out)
        g_out = jnp.where(lane == kk, exps[kk] / den, g_out)
        rank = jnp.sum(jnp.where(lane == idxs[kk], before, 0.0), axis=1, keepdims=True)
        r_out = jnp.where(lane == kk, rank, r_out)
    e_ref[...] = e_out
    g_ref[...] = g_out
    rank_ref[...] = r_out.astype(I32)


def _router(h, w_router, b_router):
    S, D = h.shape
    tm = min(LN_TM, S)
    w = jnp.pad(w_router, ((0, 0), (0, LANES - N_EXPERTS)))
    b = jnp.pad(b_router, (0, LANES - N_EXPERTS)).reshape(1, LANES)
    return pl.pallas_call(
        _router_kernel,
        grid=(S // tm,),
        in_specs=[pl.BlockSpec((tm, D), lambda i: (i, 0)),
                  pl.BlockSpec((D, LANES), lambda i: (0, 0)),
                  pl.BlockSpec((1, LANES), lambda i: (0, 0))],
        out_specs=[pl.BlockSpec((tm, LANES), lambda i: (i, 0))] * 3 + [pl.BlockSpec((SUB, LANES), lambda i: (0, 0))],
        out_shape=[jax.ShapeDtypeStruct((S, LANES), I32), jax.ShapeDtypeStruct((S, LANES), F32),
                   jax.ShapeDtypeStruct((S, LANES), I32), jax.ShapeDtypeStruct((SUB, LANES), F32)],
        scratch_shapes=[pltpu.VMEM((1, LANES), F32)],
        compiler_params=_cparams(("arbitrary",)),
        name="moe_router",
    )(h, w, b)


def _start_row_copies(idx_ref, idx_base, n, src_ref, dst_ref, dst_base, sem):
    def body(r, carry):
        tok = idx_ref[idx_base + r]
        pltpu.make_async_copy(src_ref.at[pl.ds(pl.multiple_of(tok * SUB, SUB), SUB)],
                              dst_ref.at[pl.ds(pl.multiple_of((dst_base + r) * SUB, SUB), SUB)], sem).start()
        return carry
    lax.fori_loop(0, n, body, 0, unroll=8)


def _wait_row_copies(n, src_ref, dst_ref, dst_base, sem):
    pltpu.make_async_copy(src_ref.at[pl.ds(0, n * SUB)], dst_ref.at[pl.ds(dst_base * SUB, n * SUB)], sem).wait()


def _rows_to_matrix(ref, base, n):
    return jnp.concatenate([ref[pl.ds(base * SUB + s, n, stride=SUB), :] for s in range(SUB)], axis=1)


def _expert_kernel(te_ref, nt_ref, tok_ref, nxt_ref, h_ref, wgu_ref, bgu_ref, wd_ref, bd_ref, pmat_ref, o_ref,
                   xbuf, sems, wg_stage, wd_stage, wsems, wperm, wdown, *, tm, layer):
    t = pl.program_id(0)
    nt = nt_ref[0]
    slot = t % GATHER_SLOTS

    def weight_copies(e):
        return (pltpu.make_async_copy(wgu_ref.at[layer, e], wg_stage, wsems.at[0]),
                pltpu.make_async_copy(wd_ref.at[layer, e], wd_stage, wsems.at[1]))

    @pl.when(t == 0)
    def _():
        for cp in weight_copies(te_ref[0]):
            cp.start()

    @pl.when(t == 0)
    def _():
        for a in range(GATHER_SLOTS - 1):
            @pl.when(a < nt)
            def _():
                _start_row_copies(tok_ref, a * tm, tm, h_ref, xbuf, a * tm, sems.at[a])

    ahead = t + GATHER_SLOTS - 1

    @pl.when(ahead < nt)
    def _():
        aslot = ahead % GATHER_SLOTS
        _start_row_copies(tok_ref, ahead * tm, tm, h_ref, xbuf, aslot * tm, sems.at[aslot])

    changed = jnp.logical_or(t == 0, te_ref[t] != te_ref[jnp.maximum(t - 1, 0)])

    @pl.when(jnp.logical_and(t < nt, changed))
    def _():
        for cp in weight_copies(te_ref[t]):
            cp.wait()
        blk = pmat_ref.shape[0]
        for c in range(wperm.shape[1] // blk):
            sl = slice(c * blk, (c + 1) * blk)
            wperm[:, sl] = jnp.dot(wg_stage[:, sl].astype(BF16), pmat_ref[...], preferred_element_type=F32).astype(BF16)
        wdown[...] = wd_stage[...].astype(BF16)

        @pl.when(nxt_ref[t] != te_ref[t])
        def _():
            for cp in weight_copies(nxt_ref[t]):
                cp.start()

    @pl.when(t < nt)
    def _():
        _wait_row_copies(tm, h_ref, xbuf, slot * tm, sems.at[slot])
        xb = _rows_to_matrix(xbuf, slot * tm, tm).astype(BF16)
        hgu = jnp.dot(xb, wperm[...], preferred_element_type=F32) + bgu_ref[0, 0]
        acts = []
        for c in range(hgu.shape[1] // (2 * LANES)):
            hg = jnp.minimum(hgu[:, 2 * c * LANES:(2 * c + 1) * LANES], SWIGLU_LIMIT)
            hl = jnp.clip(hgu[:, (2 * c + 1) * LANES:(2 * c + 2) * LANES], -SWIGLU_LIMIT, SWIGLU_LIMIT)
            acts.append((hg * jax.nn.sigmoid(SWIGLU_ALPHA * hg) * (hl + 1.0)).astype(BF16))
        act = jnp.concatenate(acts, axis=1)
        y = jnp.dot(act, wdown[...], preferred_element_type=F32) + bd_ref[0, 0]
        for s in range(SUB):
            o_ref[pl.ds(s, tm, stride=SUB), :] = y[:, s * LANES:(s + 1) * LANES]

    @pl.when(t >= nt)
    def _():
        o_ref[...] = jnp.zeros(o_ref.shape, o_ref.dtype)


def _experts(h_rows, row_tok, tile_e, n_tiles_used, next_e, layer, wgu, bgu, wd, bd, pmat):
    n_rows = row_tok.shape[0]
    tm = MOE_TM
    D, F2 = wgu.shape[2], wgu.shape[3]
    row_map = lambda t, te, nt, tok, nxt: (t, 0)
    b_map = lambda t, te, nt, tok, nxt: (layer, te[t], 0, 0)
    return pl.pallas_call(
        functools.partial(_expert_kernel, tm=tm, layer=layer),
        grid_spec=pltpu.PrefetchScalarGridSpec(
            num_scalar_prefetch=4,
            grid=(n_rows // tm,),
            in_specs=[pl.BlockSpec(memory_space=pl.ANY),
                      pl.BlockSpec(memory_space=pl.ANY), pl.BlockSpec((1, 1, 1, F2), b_map),
                      pl.BlockSpec(memory_space=pl.ANY), pl.BlockSpec((1, 1, 1, D), b_map),
                      pl.BlockSpec(pmat.shape, lambda t, te, nt, tok, nxt: (0, 0))],
            out_specs=pl.BlockSpec((tm * SUB, LANES), row_map),
            scratch_shapes=[pltpu.VMEM((GATHER_SLOTS * tm * SUB, LANES), F32), pltpu.SemaphoreType.DMA((GATHER_SLOTS,)),
                            pltpu.VMEM((D, F2), F32), pltpu.VMEM((F2 // 2, D), F32), pltpu.SemaphoreType.DMA((2,)),
                            pltpu.VMEM((D, F2), BF16), pltpu.VMEM((F2 // 2, D), BF16)],
        ),
        out_shape=jax.ShapeDtypeStruct((n_rows * SUB, LANES), F32),
        compiler_params=_cparams(("arbitrary",)),
        name="moe_experts",
    )(tile_e, n_tiles_used, row_tok, next_e, h_rows, wgu, bgu, wd, bd, pmat)


def _combine_ln_kernel(dest_ref, y_ref, gate_ref, h_ref, g_ref, b_ref, out_ref, ybuf, sems, *, tm):
    i = pl.program_id(0)
    n = pl.num_programs(0)
    slot = i % 2
    per = TOP_K * tm

    @pl.when(i == 0)
    def _():
        _start_row_copies(dest_ref, 0, per, y_ref, ybuf, 0, sems.at[0])

    @pl.when(i + 1 < n)
    def _():
        _start_row_copies(dest_ref, (i + 1) * per, per, y_ref, ybuf, (1 - slot) * per, sems.at[1 - slot])

    _wait_row_copies(per, y_ref, ybuf, slot * per, sems.at[slot])
    gate = gate_ref[...]
    pieces = []
    for s in range(SUB):
        acc = None
        for kk in range(TOP_K):
            rows = ybuf[pl.ds((slot * per + kk) * SUB + s, tm, stride=TOP_K * SUB), :]
            term = gate[:, kk:kk + 1] * rows
            acc = term if acc is None else acc + term
        pieces.append(acc)
    y = jnp.concatenate(pieces, axis=1)
    out_ref[...] = _layer_norm(ALPHA_DN * h_ref[...] + y, g_ref[...], b_ref[...])


def _combine_ln(ys_rows, dest, gate, h, g, b):
    S, D = h.shape
    tm = min(COMBINE_TM, S)
    return pl.pallas_call(
        functools.partial(_combine_ln_kernel, tm=tm),
        grid_spec=pltpu.PrefetchScalarGridSpec(
            num_scalar_prefetch=1,
            grid=(S // tm,),
            in_specs=[pl.BlockSpec(memory_space=pl.ANY),
                      pl.BlockSpec((tm, LANES), lambda i, d: (i, 0)),
                      pl.BlockSpec((tm, D), lambda i, d: (i, 0)),
                      pl.BlockSpec((1, D), lambda i, d: (0, 0)),
                      pl.BlockSpec((1, D), lambda i, d: (0, 0))],
            out_specs=pl.BlockSpec((tm, D), lambda i, d: (i, 0)),
            scratch_shapes=[pltpu.VMEM((2 * TOP_K * tm * SUB, LANES), F32), pltpu.SemaphoreType.DMA((2,))],
        ),
        out_shape=jax.ShapeDtypeStruct((S, D), F32),
        compiler_params=_cparams(("arbitrary",)),
        name="moe_combine_ln",
    )(dest, ys_rows, gate, h, g.reshape(1, D), b.reshape(1, D))


def _moe_block(h, h_rows, w_router, b_router, layer, expert_params, ln_g, ln_b):
    S, D = h.shape
    E, tm = N_EXPERTS, MOE_TM
    e_full, g_full, rank_full, cnt = _router(h, w_router, b_router)
    top_e = e_full[:, :TOP_K]

    counts = cnt[0, :E].astype(I32)
    padded = (counts + tm - 1) // tm * tm
    pends = jnp.cumsum(padded)
    pstarts = pends - padded
    dest = pstarts[top_e] + rank_full[:, :TOP_K]
    A = S * TOP_K
    n_tiles = A // tm + E
    n_rows = n_tiles * tm
    tok = jnp.broadcast_to(jnp.arange(S, dtype=I32)[:, None], (S, TOP_K))
    row_tok = jnp.zeros((n_rows,), I32).at[dest.reshape(A)].set(tok.reshape(A), unique_indices=True)
    n_used = (pends[-1] // tm).astype(I32)
    tile_start = jnp.arange(n_tiles, dtype=I32) * tm
    tile_e = jnp.minimum((pends[None, :] <= tile_start[:, None]).astype(I32).sum(axis=1), E - 1)
    tile_e = jnp.where(jnp.arange(n_tiles) < n_used, tile_e, tile_e[jnp.maximum(n_used - 1, 0)])
    group_end = pends[tile_e] // tm
    next_e = jnp.where(group_end < n_used, tile_e[jnp.minimum(group_end, n_tiles - 1)], tile_e)

    ys = _experts(h_rows, row_tok, tile_e, n_used.reshape(1), next_e, layer, *expert_params)
    return _combine_ln(ys, dest.reshape(A), g_full, h, ln_g, ln_b)


def _expert_params(w_gu, b_gu, w_down, b_down):
    L, E, _, F2 = w_gu.shape
    blk = 2 * LANES
    pm = np.zeros((blk, blk), np.float32)
    for qq in range(LANES):
        pm[2 * qq, qq] = 1.0
        pm[2 * qq + 1, LANES + qq] = 1.0
    bgu = b_gu.reshape(L, E, F2 // blk, LANES, 2).transpose(0, 1, 2, 4, 3).reshape(L, E, 1, F2)
    return w_gu, bgu, w_down, b_down[:, :, None, :], jnp.asarray(pm, BF16)


def _diff_attention(h, w_in, lq1, lk1, lq2, lk2, subln_g, w_out, ln_g, ln_b, tabs, layer_idx):
    dv = DA_HEADS * 2 * DA_HEAD_DIM
    scale = DA_HEAD_DIM ** -0.5 * LOG2E
    segs = (Seg(0, dv, "rope", scale, BF16), Seg(dv, dv, "rope", 1.0, BF16), Seg(2 * dv, dv, "plain", 1.0, BF16))
    q, k, v = _project(h, w_in.astype(BF16), tabs["c64"], tabs["s64"], segs, DA_HEAD_DIM // 2, "da_project")
    lam_init = 0.8 - 0.6 * math.exp(-0.3 * layer_idx)
    lam4 = jnp.stack([lq1, lk1, lq2, lk2]).astype(F32)
    o = _flash("da", q, k, v.T, (lam4, subln_g.reshape(1, -1).astype(F32)), lam_init, "da_attention")
    return _out_ln(o, w_out.astype(BF16), h, ln_g, ln_b, "da_out_ln")


def _mla(h, w_in, q_norm_g, kv_norm_g, w_uq, w_ukv, w_out, ln_g, ln_b, tabs):
    H = MLA_HEADS
    D = h.shape[1]
    qk_dim = MLA_NOPE + MLA_ROPE
    pad_rope = ((0, 0), (MLA_NOPE, LANES - MLA_NOPE - MLA_ROPE))
    w_in_p = jnp.concatenate([w_in[:, :MLA_Q_RANK + MLA_KV_RANK],
                              jnp.pad(w_in[:, MLA_Q_RANK + MLA_KV_RANK:], pad_rope)], axis=1).astype(BF16)
    w_uq_p = jnp.pad(w_uq.reshape(MLA_Q_RANK, H, qk_dim), ((0, 0), (0, 0), (0, LANES - qk_dim)))
    w_uq_p = w_uq_p.reshape(MLA_Q_RANK, H * LANES).astype(BF16)
    w_ukv3 = w_ukv.reshape(MLA_KV_RANK, H, MLA_NOPE + MLA_V)
    w_uk_p = jnp.pad(w_ukv3[:, :, :MLA_NOPE], ((0, 0), (0, 0), (0, LANES - MLA_NOPE)))
    w_uk_p = w_uk_p.reshape(MLA_KV_RANK, H * LANES).astype(BF16)
    w_uv = w_ukv3[:, :, MLA_NOPE:].reshape(MLA_KV_RANK, H * MLA_V).astype(BF16)
    q, k, v = _mla_project(h, w_in_p, w_uq_p, w_uk_p, w_uv, q_norm_g.reshape(1, -1), kv_norm_g.reshape(1, -1),
                           tabs["c32"], tabs["s32"], qk_dim ** -0.5 * LOG2E)
    o = _flash("mla", q, k, v.T, (), 0.0, "mla_attention")
    return _out_ln(o, w_out.astype(BF16), h, ln_g, ln_b, "mla_out_ln")


def _nsa(h, w_in, pos_k, pos_v, ck_w1, ck_w2, cv_w1, cv_w2, w_out, ln_g, ln_b, tabs):
    S, D = h.shape
    H, G, d = NSA_HEADS, NSA_GROUPS, NSA_HEAD_DIM
    R = H // G
    st, Ls = NSA_CMP_STRIDE, NSA_SLC_LEN
    n_cmp = (S - NSA_CMP_LEN) // st + 1
    nc = S // st
    n_slc = S // Ls
    n_top = min(NSA_SLC_TOPK, n_slc)
    nslp = -(-n_slc // LANES) * LANES
    gd = G * d

    def pad_heads(w, n):
        return jnp.pad(w.reshape(D, n, d), ((0, 0), (0, 0), (0, LANES - d))).reshape(D, n * LANES)

    off = np.cumsum([0, H * d] + [gd] * 6)
    wq, wkc, wvc, wks, wvs, wkw, wvw = [w_in[:, off[t]:off[t + 1]] for t in range(7)]
    wgl = w_in[:, off[7]:].reshape(D, H, 3)
    w_a = jnp.concatenate([pad_heads(wq, H), pad_heads(wks, G), pad_heads(wkw, G), wvs, wvw, wkc, wvc], axis=1).astype(BF16)
    hq, hg = H * LANES, G * LANES
    segs_a = (Seg(0, hq, "rope", d ** -0.5 * LOG2E, BF16), Seg(hq, hg, "rope", 1.0, BF16), Seg(hq + hg, hg, "rope", 1.0, BF16),
              Seg(hq + 2 * hg, gd, "plain", 1.0, BF16), Seg(hq + 2 * hg + gd, gd, "plain", 1.0, BF16),
              Seg(hq + 2 * hg + 2 * gd, gd, "plain", 1.0, F32), Seg(hq + 2 * hg + 3 * gd, gd, "plain", 1.0, F32))
    q, ks, kw, vs, vw, kc, vc = _project(h, w_a, tabs["c64p"], tabs["s64p"], segs_a, d // 2, "nsa_project")
    w_g = jnp.concatenate([jnp.repeat(wgl[:, :, b], d, axis=1) for b in range(3)], axis=1).astype(BF16)
    segs_g = tuple(Seg(b * H * d, H * d, "sigmoid", 1.0, F32) for b in range(3))
    gc, gs, gw = _project(h, w_g, tabs["c64p"], tabs["s64p"], segs_g, d // 2, "nsa_gates")

    def chunks(t):
        return t.reshape(nc, st, G, d).transpose(2, 0, 1, 3).reshape(G, nc, st * d)

    k_cmp = _compress(chunks(kc), pos_k, ck_w1, ck_w2, tabs["ccmp"], tabs["scmp"], True, "nsa_compress_k")
    v_cmp = _compress(chunks(vc), pos_v, cv_w1, cv_w2, tabs["ccmp"], tabs["scmp"], False, "nsa_compress_v")

    ratio = Ls // st
    amat_t = np.zeros((nslp, nc), np.float32)
    for jj in range(n_slc):
        for mm in range(ratio):
            for nn in range(NSA_CMP_LEN // st):
                c = ratio * jj + mm - nn
                if 0 <= c < n_cmp:
                    amat_t[jj, c] += 1.0
    vt_cmp = v_cmp[:, :, :d].transpose(0, 2, 1)
    ocw, sel = _nsa_local(q, k_cmp, vt_cmp, kw, vw.T, gc, gw, jnp.asarray(amat_t, BF16), n_slc, n_top)

    tk = min(NSA_TK, S)
    per_tile = tk // Ls
    tiles_per_selblock = LANES // per_tile
    assert per_tile <= LANES - d
    kpat = np.zeros((tk, LANES), np.float32)
    for t in range(tk):
        kpat[t, d + t // Ls] = 1.0
    pm = np.zeros((tiles_per_selblock, LANES, LANES), np.float32)
    for u in range(tiles_per_selblock):
        for bb in range(per_tile):
            pm[u, per_tile * u + bb, d + bb] = MASK_BIG
    o = _nsa_select(q, ks, vs.T, sel, jnp.asarray(kpat, BF16), jnp.asarray(pm, BF16), ocw, gs)
    return _out_ln(o, w_out.astype(BF16), h, ln_g, ln_b, "nsa_out_ln")


def _rope_tables(S):
    pos = jnp.arange(S)
    c64, s64 = _rope_angles(pos, DA_HEAD_DIM)
    c32, s32 = _rope_angles(pos, MLA_ROPE)
    nc = S // NSA_CMP_STRIDE
    cc, sc = _rope_angles(jnp.arange(nc) * NSA_CMP_STRIDE + NSA_CMP_LEN - 1, NSA_HEAD_DIM)
    one = lambda n, w: jnp.ones((n, w), F32)
    zero = lambda n, w: jnp.zeros((n, w), F32)
    cat = lambda *a: jnp.concatenate(a, axis=1)
    return {
        "c64": cat(c64, c64, c64, c64), "s64": cat(-s64, s64, -s64, s64),
        "c64p": cat(c64, c64, one(S, 64)), "s64p": cat(-s64, s64, zero(S, 64)),
        "c32": cat(one(S, 64), c32, c32, one(S, 32)), "s32": cat(zero(S, 64), -s32, s32, zero(S, 32)),
        "ccmp": cat(cc, cc, one(nc, 64)), "scmp": cat(-sc, sc, zero(nc, 64)),
    }


def kernel(x, da_w_in, da_lambda_q1, da_lambda_k1, da_lambda_q2, da_lambda_k2, da_subln, da_w_out, mla_w_in, mla_q_norm, mla_kv_norm, mla_w_uq, mla_w_ukv, mla_w_out, nsa_w_in, nsa_cmp_pos_k, nsa_cmp_pos_v, nsa_cmp_k_w1, nsa_cmp_k_w2, nsa_cmp_v_w1, nsa_cmp_v_w2, nsa_w_out, ln1_g, ln1_b, ln2_g, ln2_b, moe_w_router, moe_b_router, moe_w_gu, moe_b_gu, moe_w_down, moe_b_down):
    B, S, D = x.shape
    assert B == 1 and D == D_MODEL
    tabs = _rope_tables(S)
    expert_params = _expert_params(moe_w_gu, moe_b_gu, moe_w_down, moe_b_down)
    h = x.reshape(S, D)
    for i in range(DEPTH):
        m, j = i % N_MIXERS, i // N_MIXERS
        if m == 0:
            h, h_rows = _diff_attention(h, da_w_in[j], da_lambda_q1[j], da_lambda_k1[j], da_lambda_q2[j],
                                        da_lambda_k2[j], da_subln[j], da_w_out[j], ln1_g[i], ln1_b[i], tabs, i)
        elif m == 1:
            h, h_rows = _mla(h, mla_w_in[j], mla_q_norm[j], mla_kv_norm[j], mla_w_uq[j], mla_w_ukv[j], mla_w_out[j],
                             ln1_g[i], ln1_b[i], tabs)
        else:
            h, h_rows = _nsa(h, nsa_w_in[j], nsa_cmp_pos_k[j], nsa_cmp_pos_v[j], nsa_cmp_k_w1[j], nsa_cmp_k_w2[j],
                             nsa_cmp_v_w1[j], nsa_cmp_v_w2[j], nsa_w_out[j], ln1_g[i], ln1_b[i], tabs)
        h = _moe_block(h, h_rows, moe_w_router[i], moe_b_router[i], i, expert_params, ln2_g[i], ln2_b[i])
    return h.reshape(B, S, D)
```
